```python
import jax, jax.numpy as jnp
from jax import lax
import numpy as np

D_MODEL = 2048
BATCH = 2
SEQ = 8192
DEPTH = 1

HEAD_DIM = 64
ATT_WIDTH = 1024
RWKV_WIDTH = 1024
MIX_WIDTH = ATT_WIDTH + RWKV_WIDTH
N_ATT_HEADS = ATT_WIDTH // HEAD_DIM
N_RWKV_HEADS = RWKV_WIDTH // HEAD_DIM
DIL_PATTERNS = ((128, 1), (512, 4), (2048, 16))
ATT_BLOCK = 128
ROPE_THETA = 10000.0
DECAY_LORA = 64
AAA_LORA = 64
GATE_LORA = 160
NORM_EPS = 1e-5
LNX_EPS = 64e-5
SHIFT_WIDTH = 3 * RWKV_WIDTH + DECAY_LORA + AAA_LORA + GATE_LORA
IN_WIDTH = 4 * ATT_WIDTH + SHIFT_WIDTH + RWKV_WIDTH

kernel_name = 'hymba_rwkv7_dilated_attention_hybrid'


def rms_norm(x, g):
    xf = x.astype(jnp.float32)
    y = xf * lax.rsqrt(jnp.mean(xf * xf, axis=-1, keepdims=True) + NORM_EPS)
    return (y * g.astype(jnp.float32)).astype(x.dtype)


def apply_rope(t, pos):
    inv_freq = ROPE_THETA ** (-jnp.arange(0, HEAD_DIM, 2, dtype=jnp.float32) / HEAD_DIM)
    ang = pos.astype(jnp.float32)[:, None] * inv_freq[None, :]
    cos = jnp.cos(ang)[None, :, None, :]
    sin = jnp.sin(ang)[None, :, None, :]
    t = t.astype(jnp.float32)
    t1, t2 = t[..., : HEAD_DIM // 2], t[..., HEAD_DIM // 2:]
    return jnp.concatenate([t1 * cos - t2 * sin, t2 * cos + t1 * sin], axis=-1)


def dilated_window_attention(q, k, v, dilation, n_back):
    B, S, H, Dh = q.shape
    L = S // dilation
    nb = -(-L // ATT_BLOCK)
    Lp = nb * ATT_BLOCK

    def to_classes(t):
        t = t.reshape(B, L, dilation, H, Dh).transpose(0, 2, 3, 1, 4)
        return jnp.pad(t, ((0, 0), (0, 0), (0, 0), (0, Lp - L), (0, 0)))

    def windows(t):
        t = jnp.pad(t, ((0, 0), (0, 0), (0, 0), (ATT_BLOCK, 0), (0, 0)))
        t = t.reshape(B, dilation, H, nb + 1, ATT_BLOCK, Dh)
        return jnp.concatenate([t[:, :, :, :-1], t[:, :, :, 1:]], axis=4)

    qb = to_classes(q).reshape(B, dilation, H, nb, ATT_BLOCK, Dh)
    kw = windows(to_classes(k))
    vw = windows(to_classes(v))
    s = jnp.einsum('bdhnqc,bdhnkc->bdhnqk', qb, kw)
    qi = jnp.arange(ATT_BLOCK)[:, None]
    ki = jnp.arange(2 * ATT_BLOCK)[None, :]
    rel = ATT_BLOCK + qi - ki
    band = (rel >= 0) & (rel <= n_back)
    blk = jnp.arange(nb)[:, None, None]
    valid = band[None] & ((blk > 0) | (ki[None] >= ATT_BLOCK))
    s = jnp.where(valid, s, -jnp.inf)
    m = jnp.max(s, axis=-1, keepdims=True)
    p = jnp.exp(s - m)
    den = jnp.sum(p, axis=-1)
    o = jnp.einsum('bdhnqk,bdhnkc->bdhnqc', p, vw) / den[..., None]
    lse = m[..., 0] + jnp.log(den)
    o = o.reshape(B, dilation, H, Lp, Dh)[:, :, :, :L].transpose(0, 3, 1, 2, 4).reshape(B, S, H, Dh)
    lse = lse.reshape(B, dilation, H, Lp)[:, :, :, :L].transpose(0, 3, 1, 2).reshape(B, S, H)
    return o, lse


def dilated_mixture_attention(q, k, v, pos):
    B, S, _ = q.shape
    q = apply_rope(q.reshape(B, S, N_ATT_HEADS, HEAD_DIM), pos) * (HEAD_DIM ** -0.5)
    k = apply_rope(k.reshape(B, S, N_ATT_HEADS, HEAD_DIM), pos)
    v = v.reshape(B, S, N_ATT_HEADS, HEAD_DIM)
    outs, lses = [], []
    for window, dilation in DIL_PATTERNS:
        o, lse = dilated_window_attention(q, k, v, dilation, window // dilation)
        outs.append(o)
        lses.append(lse)
    wts = jax.nn.softmax(jnp.stack(lses), axis=0)
    o = jnp.einsum('pbsh,pbshc->bshc', wts, jnp.stack(outs))
    return o.reshape(B, S, ATT_WIDTH)


def rwkv7_recurrence(r, w, k, v, a, b):
    B, S, H, N = r.shape

    def step(state, inp):
        r_t, w_t, k_t, v_t, a_t, b_t = inp
        sa = jnp.einsum('bhij,bhj->bhi', state, a_t)
        state = (state * w_t[:, :, None, :] + sa[..., None] * b_t[:, :, None, :]
                 + v_t[..., None] * k_t[:, :, None, :])
        y = jnp.einsum('bhij,bhj->bhi', state, r_t)
        return state, y

    xs = tuple(t.transpose(1, 0, 2, 3) for t in (r, w, k, v, a, b))
    init = jnp.zeros((B, H, N, N), jnp.float32)
    _, y = lax.scan(step, init, xs)
    return y.transpose(1, 0, 2, 3)


def rwkv7_time_mix(u, z, mu, w0, w2, a0, a2, g2, k_k, k_a, r_k, lnx_g, lnx_b):
    B, S, _ = u.shape
    R = RWKV_WIDTH
    u_prev = jnp.pad(u, ((0, 0), (1, 0), (0, 0)))[:, :S]
    u = u + (u_prev - u) * mu
    r, k, v, wl, al, gl = jnp.split(
        u, [R, 2 * R, 3 * R, 3 * R + DECAY_LORA, 3 * R + DECAY_LORA + AAA_LORA], axis=-1)
    w_log = -jax.nn.softplus(-(w0 + jnp.matmul(jnp.tanh(wl), w2))) - 0.5
    decay = jnp.exp(-jnp.exp(w_log))
    a = jax.nn.sigmoid(a0 + jnp.matmul(al, a2))
    g = jnp.matmul(jax.nn.sigmoid(gl), g2)

    def heads(t):
        return t.reshape(B, S, N_RWKV_HEADS, HEAD_DIM)

    kk = heads(k * k_k)
    kk = kk / jnp.maximum(jnp.sqrt(jnp.sum(kk * kk, axis=-1, keepdims=True)), 1e-12)
    k = k * (1.0 + (a - 1.0) * k_a)
    rh, kh, vh, ah = heads(r), heads(k), heads(v), heads(a)
    y = rwkv7_recurrence(rh, heads(decay), kh, vh, -kk, kk * ah)
    mean = jnp.mean(y, axis=-1, keepdims=True)
    var = jnp.mean(jnp.square(y - mean), axis=-1, keepdims=True)
    y = ((y - mean) * lax.rsqrt(var + LNX_EPS)).reshape(B, S, R) * lnx_g + lnx_b
    bonus = (jnp.sum(rh * kh * r_k, axis=-1, keepdims=True) * vh).reshape(B, S, R)
    return (y + bonus) * g * jax.nn.silu(z)


def setup_inputs(seed: int = 0) -> dict:
    key = jax.random.key(seed)
    ks = jax.random.split(key, 16)
    f32 = jnp.float32
    R = RWKV_WIDTH
    x = jax.random.normal(ks[0], (BATCH, SEQ, D_MODEL), f32)
    norm_g = 1.0 + 0.02 * jax.random.normal(ks[1], (DEPTH, D_MODEL), f32)
    w_in = jax.random.normal(ks[2], (DEPTH, D_MODEL, IN_WIDTH), f32) * D_MODEL ** -0.5
    shift_mu = jax.random.uniform(ks[3], (DEPTH, SHIFT_WIDTH), f32)
    w0 = jnp.linspace(-6.0, -1.0, R, dtype=f32)[None, :] + 0.1 * jax.random.normal(ks[4], (DEPTH, R), f32)
    w2 = jax.random.normal(ks[5], (DEPTH, DECAY_LORA, R), f32) * (0.1 * DECAY_LORA ** -0.5)
    a0 = 0.1 * jax.random.normal(ks[6], (DEPTH, R), f32)
    a2 = jax.random.normal(ks[7], (DEPTH, AAA_LORA, R), f32) * (0.5 * AAA_LORA ** -0.5)
    g2 = jax.random.normal(ks[8], (DEPTH, GATE_LORA, R), f32) * GATE_LORA ** -0.5
    k_k = 0.85 + 0.02 * jax.random.normal(ks[9], (DEPTH, R), f32)
    k_a = 1.0 + 0.02 * jax.random.normal(ks[10], (DEPTH, R), f32)
    r_k = -0.04 + 0.05 * jax.random.normal(ks[11], (DEPTH, N_RWKV_HEADS, HEAD_DIM), f32)
    lnx_g = 1.0 + 0.02 * jax.random.normal(ks[12], (DEPTH, R), f32)
    lnx_b = 0.02 * jax.random.normal(ks[13], (DEPTH, R), f32)
    w_out = jax.random.normal(ks[14], (DEPTH, MIX_WIDTH, D_MODEL), f32) * MIX_WIDTH ** -0.5
    final_g = 1.0 + 0.02 * jax.random.normal(ks[15], (D_MODEL,), f32)
    return {'x': x, 'norm_g': norm_g, 'w_in': w_in, 'shift_mu': shift_mu, 'w0': w0,
            'w2': w2, 'a0': a0, 'a2': a2, 'g2': g2, 'k_k': k_k, 'k_a': k_a, 'r_k': r_k,
            'lnx_g': lnx_g, 'lnx_b': lnx_b, 'w_out': w_out, 'final_g': final_g}


def reference(x, norm_g, w_in, shift_mu, w0, w2, a0, a2, g2, k_k, k_a, r_k,
              lnx_g, lnx_b, w_out, final_g):
    B, S, _ = x.shape
    pos = jnp.arange(S, dtype=jnp.int32)
    A = ATT_WIDTH
    splits = [A, 2 * A, 3 * A, 4 * A, 4 * A + SHIFT_WIDTH]
    for l in range(DEPTH):
        h = rms_norm(x, norm_g[l])
        proj = jnp.matmul(h, w_in[l]).astype(jnp.float32)
        q, k, v, z_att, u_shift, z_rwkv = jnp.split(proj, splits, axis=-1)
        att = dilated_mixture_attention(q, k, v, pos) * jax.nn.silu(z_att)
        rwk = rwkv7_time_mix(u_shift, z_rwkv, shift_mu[l], w0[l], w2[l], a0[l], a2[l],
                             g2[l], k_k[l], k_a[l], r_k[l], lnx_g[l], lnx_b[l])
        mix = jnp.concatenate([att, rwk], axis=-1).astype(x.dtype)
        x = x + jnp.matmul(mix, w_out[l])
    return rms_norm(x, final_g)
```

```python
import functools

import jax
import jax.numpy as jnp
from jax import lax
from jax.experimental import pallas as pl
from jax.experimental.pallas import tpu as pltpu

F32 = jnp.float32
BF16 = jnp.bfloat16
HIGHEST = lax.Precision.HIGHEST

D_MODEL = 2048
HEAD_DIM = 64
ATT_WIDTH = 1024
RWKV_WIDTH = 1024
DIL_PATTERNS = ((128, 1), (512, 4), (2048, 16))
ROPE_THETA = 10000.0
DECAY_LORA = 64
AAA_LORA = 64
GATE_LORA = 160
NORM_EPS = 1e-5
LNX_EPS = 64e-5

LANES = 128
N_PAIRS = ATT_WIDTH // LANES
N_BACK = 128
Q_BLOCK = 128
SUPER = 2048
CHUNK = 64
RWKV_BLOCK = 512
LORA_PAD = 512

COL_Q, COL_K, COL_V, COL_ZA = 0, 1024, 2048, 3072
COL_R, COL_RK, COL_RV, COL_ZR = 4096, 5120, 6144, 7168
COL_LORA = 8192
PROJ_WIDTH = COL_LORA + LORA_PAD

IN_TM, IN_TN = 1024, 512
N_ROPE_TILES = (COL_V - COL_Q) // IN_TN
N_Q_TILES = (COL_K - COL_Q) // IN_TN
OUT_TM = 512
VMEM_LIMIT = 56 * 1024 * 1024


def _dot(a, b, precision=None):
    return jnp.dot(a, b, preferred_element_type=F32, precision=precision)


def _dot_nt(a, b, precision=None):
    return lax.dot_general(a, b, (((1,), (1,)), ((), ())),
                           preferred_element_type=F32, precision=precision)


def _bdot(a, b):
    return _dot(a.astype(BF16), b.astype(BF16))


def _bdot_nt(a, b):
    return _dot_nt(a.astype(BF16), b.astype(BF16))


def _in_proj_kernel(x_ref, g_ref, w_ref, cos_ref, sin_ref, o_ref, h_ref):
    j = pl.program_id(1)

    @pl.when(j == 0)
    def _():
        x = x_ref[...]
        ms = jnp.mean(x * x, axis=-1, keepdims=True)
        h_ref[...] = (x * lax.rsqrt(ms + NORM_EPS) * g_ref[...]).astype(BF16)

    acc = _dot(h_ref[...], w_ref[...])

    @pl.when(j < N_ROPE_TILES)
    def _():
        reps = IN_TN // LANES
        cos = jnp.concatenate([cos_ref[...]] * reps, axis=1)
        sin = jnp.concatenate([sin_ref[...]] * reps, axis=1)
        lane = lax.broadcasted_iota(jnp.int32, acc.shape, 1)
        first_half = (lane % HEAD_DIM) < (HEAD_DIM // 2)
        partner = jnp.where(first_half,
                            pltpu.roll(acc, IN_TN - HEAD_DIM // 2, 1),
                            pltpu.roll(acc, HEAD_DIM // 2, 1))
        scale = jnp.where(j < N_Q_TILES, HEAD_DIM ** -0.5, 1.0).astype(F32)
        o_ref[...] = (acc * cos + partner * sin) * scale

    @pl.when(j >= N_ROPE_TILES)
    def _():
        o_ref[...] = acc


def _in_proj(x2, g, w_pad, cos_t, sin_t, seq):
    tokens = x2.shape[0]
    pos_blocks = seq // IN_TM
    return pl.pallas_call(
        _in_proj_kernel,
        grid=(tokens // IN_TM, PROJ_WIDTH // IN_TN),
        in_specs=[
            pl.BlockSpec((IN_TM, D_MODEL), lambda i, j: (i, 0)),
            pl.BlockSpec((1, D_MODEL), lambda i, j: (0, 0)),
            pl.BlockSpec((D_MODEL, IN_TN), lambda i, j: (0, j)),
            pl.BlockSpec((IN_TM, LANES), lambda i, j: (i % pos_blocks, 0)),
            pl.BlockSpec((IN_TM, LANES), lambda i, j: (i % pos_blocks, 0)),
        ],
        out_specs=pl.BlockSpec((IN_TM, IN_TN), lambda i, j: (i, j)),
        out_shape=jax.ShapeDtypeStruct((tokens, PROJ_WIDTH), F32),
        scratch_shapes=[pltpu.VMEM((IN_TM, D_MODEL), BF16)],
        compiler_params=pltpu.CompilerParams(
            dimension_semantics=("arbitrary", "arbitrary"),
            vmem_limit_bytes=VMEM_LIMIT),
        name="in_proj",
    )(x2, g, w_pad, cos_t, sin_t)


def _attention_kernel(q_ref, kc_ref, kp_ref, vc_ref, vp_ref, z_ref, o_ref,
                      kbuf, vbuf, acc_s, m_s, den_s):
    i = pl.program_id(2)
    kbuf[0:SUPER, :] = kp_ref[0]
    kbuf[SUPER:2 * SUPER, :] = kc_ref[0]
    vbuf[0:SUPER, :] = vp_ref[0]
    vbuf[SUPER:2 * SUPER, :] = vc_ref[0]

    qi = lax.broadcasted_iota(jnp.int32, (Q_BLOCK, 2 * Q_BLOCK), 0)
    ki = lax.broadcasted_iota(jnp.int32, (Q_BLOCK, 2 * Q_BLOCK), 1)
    rel = Q_BLOCK + qi - ki
    band = (rel >= 0) & (rel <= N_BACK)
    cur = ki >= Q_BLOCK
    lane = lax.broadcasted_iota(jnp.int32, (Q_BLOCK, LANES), 1)
    head0 = lane < HEAD_DIM

    def block(pat, dil, q_start):
        rows_q = pl.ds(q_start, Q_BLOCK, stride=dil) if dil > 1 else pl.ds(q_start, Q_BLOCK)
        k_start = SUPER + q_start - Q_BLOCK * dil
        rows_k = (pl.ds(k_start, 2 * Q_BLOCK, stride=dil) if dil > 1
                  else pl.ds(k_start, 2 * Q_BLOCK))
        qs = q_ref[0, rows_q, :]
        kw = kbuf[rows_k, :].astype(BF16)
        vw = vbuf[rows_k, :].astype(BF16)
        prev_ok = (i * SUPER + q_start) >= Q_BLOCK * dil
        valid = band & (cur | prev_ok)
        accs, ms, dens = [], [], []
        for hm in (head0, ~head0):
            qh = jnp.where(hm, qs, 0.0).astype(BF16)
            s = _dot_nt(qh, kw)
            s = jnp.where(valid, s, -jnp.inf)
            m = jnp.max(s, axis=-1, keepdims=True)
            p = jnp.exp(s - m)
            dens.append(jnp.sum(p, axis=-1, keepdims=True))
            ms.append(m)
            accs.append(_dot(p.astype(BF16), vw))
        acc_s[pat, rows_q, :] = jnp.where(head0, accs[0], accs[1])
        m_s[pat, rows_q, :] = jnp.where(head0, ms[0], ms[1])
        den_s[pat, rows_q, :] = jnp.where(head0, dens[0], dens[1])

    for pat, (window, dil) in enumerate(DIL_PATTERNS):
        span = Q_BLOCK * dil
        for s_idx in range(SUPER // span):
            for r in range(dil):
                block(pat, dil, s_idx * span + r)

    m0, m1, m2 = m_s[0], m_s[1], m_s[2]
    mx = jnp.maximum(jnp.maximum(m0, m1), m2)
    e0, e1, e2 = jnp.exp(m0 - mx), jnp.exp(m1 - mx), jnp.exp(m2 - mx)
    num = e0 * acc_s[0] + e1 * acc_s[1] + e2 * acc_s[2]
    den = e0 * den_s[0] + e1 * den_s[1] + e2 * den_s[2]
    z = z_ref[0]
    o_ref[0] = (num / den * (z * jax.nn.sigmoid(z))).astype(o_ref.dtype)


def _attention(proj3):
    batch, seq, _ = proj3.shape
    blk = (1, SUPER, LANES)

    def col(base):
        return lambda b, p, i: (b, i, base // LANES + p)

    def col_prev(base):
        return lambda b, p, i: (b, jnp.maximum(i - 1, 0), base // LANES + p)

    return pl.pallas_call(
        _attention_kernel,
        grid=(batch, N_PAIRS, seq // SUPER),
        in_specs=[
            pl.BlockSpec(blk, col(COL_Q)),
            pl.BlockSpec(blk, col(COL_K)),
            pl.BlockSpec(blk, col_prev(COL_K)),
            pl.BlockSpec(blk, col(COL_V)),
            pl.BlockSpec(blk, col_prev(COL_V)),
            pl.BlockSpec(blk, col(COL_ZA)),
        ],
        out_specs=pl.BlockSpec(blk, lambda b, p, i: (b, i, p)),
        out_shape=jax.ShapeDtypeStruct((batch, seq, ATT_WIDTH), BF16),
        scratch_shapes=[
            pltpu.VMEM((2 * SUPER, LANES), F32),
            pltpu.VMEM((2 * SUPER, LANES), F32),
            pltpu.VMEM((len(DIL_PATTERNS), SUPER, LANES), F32),
            pltpu.VMEM((len(DIL_PATTERNS), SUPER, LANES), F32),
            pltpu.VMEM((len(DIL_PATTERNS), SUPER, LANES), F32),
        ],
        compiler_params=pltpu.CompilerParams(
            dimension_semantics=("arbitrary", "arbitrary", "arbitrary"),
            vmem_limit_bytes=VMEM_LIMIT),
        name="dilated_attention",
    )(proj3, proj3, proj3, proj3, proj3, proj3)


def _rwkv_kernel(r_ref, k_ref, v_ref, z_ref, lora_ref,
                 mur_ref, muk_ref, muv_ref, mul_ref,
                 w0_ref, a0_ref, kk_ref, ka_ref, rk_ref, lng_ref, lnb_ref,
                 w2_ref, a2_ref, g2_ref,
                 o_ref,
                 carry_main, carry_lora, state,
                 r_s, k_s, v_s, a_s, b_s, lw_s, gz_s):
    i = pl.program_id(2)

    @pl.when(i == 0)
    def _():
        carry_main[...] = jnp.zeros_like(carry_main)
        carry_lora[...] = jnp.zeros_like(carry_lora)
        state[...] = jnp.zeros_like(state)

    row_m = lax.broadcasted_iota(jnp.int32, (RWKV_BLOCK, LANES), 0)
    row_l = lax.broadcasted_iota(jnp.int32, (RWKV_BLOCK, LORA_PAD), 0)
    rr = lax.broadcasted_iota(jnp.int32, (LANES, LANES), 0)
    cc = lax.broadcasted_iota(jnp.int32, (LANES, LANES), 1)
    same_head = (rr // HEAD_DIM) == (cc // HEAD_DIM)
    head_ones = same_head.astype(F32)

    def shifted(u, carry_row, row_iota, mu):
        prev = jnp.where(row_iota == 0, carry_row, pltpu.roll(u, 1, 0))
        return u + (prev - u) * mu

    r_raw, k_raw, v_raw = r_ref[0], k_ref[0], v_ref[0]
    lora_raw = lora_ref[0]
    r = shifted(r_raw, carry_main[0:1, :], row_m, mur_ref[...])
    k = shifted(k_raw, carry_main[1:2, :], row_m, muk_ref[...])
    v = shifted(v_raw, carry_main[2:3, :], row_m, muv_ref[...])
    lora = shifted(lora_raw, carry_lora[...], row_l, mul_ref[...])
    carry_main[0:1, :] = r_raw[RWKV_BLOCK - 1:RWKV_BLOCK, :]
    carry_main[1:2, :] = k_raw[RWKV_BLOCK - 1:RWKV_BLOCK, :]
    carry_main[2:3, :] = v_raw[RWKV_BLOCK - 1:RWKV_BLOCK, :]
    carry_lora[...] = lora_raw[RWKV_BLOCK - 1:RWKV_BLOCK, :]

    wa = lora[:, 0:LANES]
    gl = lora[:, LANES:LORA_PAD]
    w_pre = w0_ref[...] + _dot(jnp.tanh(wa), w2_ref[...], HIGHEST)
    w_log = -jax.nn.softplus(-w_pre) - 0.5
    lw_s[...] = -jnp.exp(w_log)
    a = jax.nn.sigmoid(a0_ref[...] + _dot(wa, a2_ref[...], HIGHEST))
    g = _dot(jax.nn.sigmoid(gl), g2_ref[...], HIGHEST)
    z = z_ref[0]
    gz_s[...] = g * (z * jax.nn.sigmoid(z))

    kk = k * kk_ref[...]
    ss = _dot(kk * kk, head_ones, HIGHEST)
    kk = kk / jnp.maximum(jnp.sqrt(ss), 1e-12)
    r_s[...] = r
    k_s[...] = k * (1.0 + (a - 1.0) * ka_ref[...])
    v_s[...] = v
    a_s[...] = -kk
    b_s[...] = kk * a

    head0 = lax.broadcasted_iota(jnp.int32, (CHUNK, LANES), 1) < HEAD_DIM
    tt = rr % CHUNK
    ts = cc % CHUNK
    strict = same_head & (ts < tt)
    incl = same_head & (ts <= tt)
    tril = (lax.broadcasted_iota(jnp.int32, (CHUNK, CHUNK), 1)
            <= lax.broadcasted_iota(jnp.int32, (CHUNK, CHUNK), 0)).astype(F32)
    eye = (rr == cc).astype(F32)

    def bd(x):
        return jnp.concatenate([jnp.where(head0, x, 0.0), jnp.where(head0, 0.0, x)], axis=0)

    def st(x):
        return jnp.concatenate([x, x], axis=0)

    def chunk_step(c, carry):
        rows = pl.ds(pl.multiple_of(c * CHUNK, CHUNK), CHUNK)
        rc, kc, vc = r_s[rows, :], k_s[rows, :], v_s[rows, :]
        ac, bc, lw = a_s[rows, :], b_s[rows, :], lw_s[rows, :]
        cs = _dot(tril, lw, HIGHEST)
        cs_end = cs[CHUNK - 1:CHUNK, :]
        e_neg = jnp.exp(-cs)
        a_t = ac * jnp.exp(cs - lw)
        r_t = rc * jnp.exp(cs)
        k_t = kc * e_neg
        b_t = bc * e_neg
        e_end = jnp.exp(cs_end - cs)
        k_h = kc * e_end
        b_h = bc * e_end

        lhs = jnp.concatenate([bd(a_t), bd(r_t)], axis=0)
        rhs = jnp.concatenate([st(k_t), st(b_t)], axis=0)
        sc = _bdot_nt(lhs, rhs)
        a_ak = jnp.where(strict, sc[0:LANES, 0:LANES], 0.0)
        a_ab = jnp.where(strict, sc[0:LANES, LANES:], 0.0)
        a_rk = jnp.where(incl, sc[LANES:, 0:LANES], 0.0)
        a_rb = jnp.where(incl, sc[LANES:, LANES:], 0.0)

        p_acc = eye + a_ab
        q_pow = a_ab
        for _ in range(5):
            q_pow = _bdot(q_pow, q_pow)
            p_acc = p_acc + _bdot(p_acc, q_pow)

        vb = bd(vc)
        ht = state[...]
        w_mat = _bdot(p_acc, bd(a_t))
        u0 = _bdot(p_acc, _bdot(a_ak, vb))
        u = _bdot_nt(w_mat, ht) + u0
        y = _bdot_nt(bd(r_t), ht) + _bdot(a_rk, vb) + _bdot(a_rb, u)
        state[...] = (ht * jnp.exp(cs_end)
                      + _bdot(vb.T, bd(k_h)) + _bdot(u.T, bd(b_h)))
        y = y[0:CHUNK, :] + y[CHUNK:, :]

        mean = _dot(y, head_ones, HIGHEST) * (1.0 / HEAD_DIM)
        yc = y - mean
        var = _dot(yc * yc, head_ones, HIGHEST) * (1.0 / HEAD_DIM)
        yn = yc * lax.rsqrt(var + LNX_EPS) * lng_ref[...] + lnb_ref[...]
        bonus = _dot(rc * kc * rk_ref[...], head_ones, HIGHEST) * vc
        o_ref[0, rows, :] = ((yn + bonus) * gz_s[rows, :]).astype(o_ref.dtype)
        return carry

    lax.fori_loop(0, RWKV_BLOCK // CHUNK, chunk_step, 0)


def _rwkv(proj3, mu_r, mu_k, mu_v, mu_l, w0, a0, k_k, k_a, r_k, lnx_g, lnx_b,
          w2p, a2p, g2p):
    batch, seq, _ = proj3.shape
    blk = (1, RWKV_BLOCK, LANES)

    def col(base):
        return lambda b, p, i: (b, i, base // LANES + p)

    vec = pl.BlockSpec((1, LANES), lambda b, p, i: (0, p))
    return pl.pallas_call(
        _rwkv_kernel,
        grid=(batch, N_PAIRS, seq // RWKV_BLOCK),
        in_specs=[
            pl.BlockSpec(blk, col(COL_R)),
            pl.BlockSpec(blk, col(COL_RK)),
            pl.BlockSpec(blk, col(COL_RV)),
            pl.BlockSpec(blk, col(COL_ZR)),
            pl.BlockSpec((1, RWKV_BLOCK, LORA_PAD),
                         lambda b, p, i: (b, i, COL_LORA // LORA_PAD)),
            vec, vec, vec,
            pl.BlockSpec((1, LORA_PAD), lambda b, p, i: (0, 0)),
            vec, vec, vec, vec, vec, vec, vec,
            pl.BlockSpec((LANES, LANES), lambda b, p, i: (0, p)),
            pl.BlockSpec((LANES, LANES), lambda b, p, i: (0, p)),
            pl.BlockSpec((LORA_PAD - LANES, LANES), lambda b, p, i: (0, p)),
        ],
        out_specs=pl.BlockSpec(blk, lambda b, p, i: (b, i, p)),
        out_shape=jax.ShapeDtypeStruct((batch, seq, RWKV_WIDTH), BF16),
        scratch_shapes=[
            pltpu.VMEM((8, LANES), F32),
            pltpu.VMEM((1, LORA_PAD), F32),
            pltpu.VMEM((LANES, LANES), F32),
        ] + [pltpu.VMEM((RWKV_BLOCK, LANES), F32)] * 7,
        compiler_params=pltpu.CompilerParams(
            dimension_semantics=("arbitrary", "arbitrary", "arbitrary"),
            vmem_limit_bytes=VMEM_LIMIT),
        name="rwkv7_time_mix",
    )(proj3, proj3, proj3, proj3, proj3, mu_r, mu_k, mu_v, mu_l,
      w0, a0, k_k, k_a, r_k, lnx_g, lnx_b, w2p, a2p, g2p)


def _out_proj_kernel(att_ref, rwk_ref, wa_ref, wr_ref, x_ref, g_ref, o_ref, *, normalize):
    y = x_ref[...] + _dot(att_ref[...], wa_ref[...]) + _dot(rwk_ref[...], wr_ref[...])
    if normalize:
        ms = jnp.mean(y * y, axis=-1, keepdims=True)
        y = y * lax.rsqrt(ms + NORM_EPS) * g_ref[...]
    o_ref[...] = y


def _out_proj(att2, rwk2, w_att, w_rwk, x2, g, normalize):
    tokens = x2.shape[0]
    return pl.pallas_call(
        functools.partial(_out_proj_kernel, normalize=normalize),
        grid=(tokens // OUT_TM,),
        in_specs=[
            pl.BlockSpec((OUT_TM, ATT_WIDTH), lambda i: (i, 0)),
            pl.BlockSpec((OUT_TM, RWKV_WIDTH), lambda i: (i, 0)),
            pl.BlockSpec((ATT_WIDTH, D_MODEL), lambda i: (0, 0)),
            pl.BlockSpec((RWKV_WIDTH, D_MODEL), lambda i: (0, 0)),
            pl.BlockSpec((OUT_TM, D_MODEL), lambda i: (i, 0)),
            pl.BlockSpec((1, D_MODEL), lambda i: (0, 0)),
        ],
        out_specs=pl.BlockSpec((OUT_TM, D_MODEL), lambda i: (i, 0)),
        out_shape=jax.ShapeDtypeStruct((tokens, D_MODEL), F32),
        compiler_params=pltpu.CompilerParams(
            dimension_semantics=("arbitrary",),
            vmem_limit_bytes=VMEM_LIMIT),
        name="out_proj",
    )(att2, rwk2, w_att, w_rwk, x2, g)


def _rope_tables(seq):
    inv_freq = ROPE_THETA ** (-jnp.arange(0, HEAD_DIM, 2, dtype=F32) / HEAD_DIM)
    ang = jnp.arange(seq, dtype=jnp.int32).astype(F32)[:, None] * inv_freq[None, :]
    cos, sin = jnp.cos(ang), jnp.sin(ang)
    reps = LANES // HEAD_DIM
    cos_t = jnp.tile(jnp.concatenate([cos, cos], axis=1), (1, reps))
    sin_t = jnp.tile(jnp.concatenate([-sin, sin], axis=1), (1, reps))
    return cos_t, sin_t


def _layer(x, norm_g, w_in, shift_mu, w0, w2, a0, a2, g2, k_k, k_a, r_k,
           lnx_g, lnx_b, w_out, cos_t, sin_t):
    batch, seq, _ = x.shape
    a_w, r_w = ATT_WIDTH, RWKV_WIDTH
    lo = 4 * a_w + 3 * r_w
    n_lora = DECAY_LORA + AAA_LORA + GATE_LORA
    w_pad = jnp.concatenate([
        w_in[:, :lo],
        w_in[:, lo + n_lora:],
        w_in[:, lo:lo + n_lora],
        jnp.zeros((D_MODEL, LORA_PAD - n_lora), w_in.dtype)], axis=1).astype(BF16)
    mu_r = shift_mu[None, 0:r_w]
    mu_k = shift_mu[None, r_w:2 * r_w]
    mu_v = shift_mu[None, 2 * r_w:3 * r_w]
    mu_l = jnp.pad(shift_mu[None, 3 * r_w:], ((0, 0), (0, LORA_PAD - n_lora)))
    w2p = jnp.pad(w2, ((0, LANES - DECAY_LORA), (0, 0)))
    a2p = jnp.pad(a2, ((DECAY_LORA, LANES - DECAY_LORA - AAA_LORA), (0, 0)))
    g2p = jnp.pad(g2, ((0, LORA_PAD - LANES - GATE_LORA), (0, 0)))

    x2 = x.reshape(batch * seq, D_MODEL)
    proj = _in_proj(x2, norm_g[None, :], w_pad, cos_t, sin_t, seq)
    proj3 = proj.reshape(batch, seq, PROJ_WIDTH)
    att = _attention(proj3)
    rwk = _rwkv(proj3, mu_r, mu_k, mu_v, mu_l, w0[None, :], a0[None, :],
                k_k[None, :], k_a[None, :], r_k.reshape(1, r_w), lnx_g[None, :],
                lnx_b[None, :], w2p, a2p, g2p)
    w_out_b = w_out.astype(BF16)
    return att.reshape(batch * seq, a_w), rwk.reshape(batch * seq, r_w), w_out_b, x2


def kernel(x, norm_g, w_in, shift_mu, w0, w2, a0, a2, g2, k_k, k_a, r_k,
           lnx_g, lnx_b, w_out, final_g):
    batch, seq, _ = x.shape
    depth = norm_g.shape[0]
    assert seq % SUPER == 0 and seq % IN_TM == 0 and (batch * seq) % OUT_TM == 0
    cos_t, sin_t = _rope_tables(seq)
    for l in range(depth):
        att2, rwk2, w_out_b, x2 = _layer(
            x, norm_g[l], w_in[l], shift_mu[l], w0[l], w2[l], a0[l], a2[l], g2[l],
            k_k[l], k_a[l], r_k[l], lnx_g[l], lnx_b[l], w_out[l], cos_t, sin_t)
        y2 = _out_proj(att2, rwk2, w_out_b[:ATT_WIDTH], w_out_b[ATT_WIDTH:], x2,
                       final_g[None, :], normalize=(l == depth - 1))
        x = y2.reshape(batch, seq, D_MODEL)
    return x
```

```python
import functools

import jax
import jax.numpy as jnp
from jax import lax
from jax.experimental import pallas as pl
from jax.experimental.pallas import tpu as pltpu

F32 = jnp.float32
BF16 = jnp.bfloat16

D_MODEL = 2048
HEAD_DIM = 64
ATT_WIDTH = 1024
RWKV_WIDTH = 1024
DIL_PATTERNS = ((128, 1), (512, 4), (2048, 16))
ROPE_THETA = 10000.0
DECAY_LORA = 64
AAA_LORA = 64
GATE_LORA = 160
NORM_EPS = 1e-5
LNX_EPS = 64e-5
DECAY_SCALE = 0.6065306597126334

LANES = 128
N_PAIRS = ATT_WIDTH // LANES
N_BACK = 128
Q_BLOCK = 128
SUPER = 2048
CHUNK = 64
RWKV_BLOCK = 512
LORA_PAD = 512

COL_Q, COL_K, COL_V, COL_ZA = 0, 1024, 2048, 3072
COL_R, COL_RK, COL_RV, COL_ZR = 4096, 5120, 6144, 7168
COL_LORA = 8192
PROJ_WIDTH = COL_LORA + LORA_PAD

IN_TM, IN_TN = 1024, 512
N_ROPE_TILES = (COL_V - COL_Q) // IN_TN
N_Q_TILES = (COL_K - COL_Q) // IN_TN
OUT_TM = 512
VMEM_LIMIT = 56 * 1024 * 1024


def _dot(a, b, precision=None):
    return jnp.dot(a, b, preferred_element_type=F32, precision=precision)


def _dot_nt(a, b, precision=None):
    return lax.dot_general(a, b, (((1,), (1,)), ((), ())),
                           preferred_element_type=F32, precision=precision)


def _bdot(a, b):
    return _dot(a.astype(BF16), b.astype(BF16))


def _bdot_nt(a, b):
    return _dot_nt(a.astype(BF16), b.astype(BF16))


def _bf16_pieces(x, terms):
    pieces = []
    for _ in range(terms):
        piece = x.astype(BF16)
        pieces.append(piece)
        x = x - piece.astype(F32)
    return pieces


def _dot_split(x, m, terms):
    return sum(_dot(piece, m) for piece in _bf16_pieces(x, terms))


def _split_dot(m, x, terms):
    return sum(_dot(m, piece) for piece in _bf16_pieces(x, terms))


def _in_proj_kernel(x_ref, g_ref, w_ref, cos_ref, sin_ref, o_ref, h_ref):
    j = pl.program_id(1)

    @pl.when(j == 0)
    def _():
        x = x_ref[...]
        ms = jnp.mean(x * x, axis=-1, keepdims=True)
        h_ref[...] = (x * lax.rsqrt(ms + NORM_EPS) * g_ref[...]).astype(BF16)

    acc = _dot(h_ref[...], w_ref[...])

    @pl.when(j < N_ROPE_TILES)
    def _():
        reps = IN_TN // LANES
        cos = jnp.concatenate([cos_ref[...]] * reps, axis=1)
        sin = jnp.concatenate([sin_ref[...]] * reps, axis=1)
        lane = lax.broadcasted_iota(jnp.int32, acc.shape, 1)
        first_half = (lane % HEAD_DIM) < (HEAD_DIM // 2)
        partner = jnp.where(first_half,
                            pltpu.roll(acc, IN_TN - HEAD_DIM // 2, 1),
                            pltpu.roll(acc, HEAD_DIM // 2, 1))
        scale = jnp.where(j < N_Q_TILES, HEAD_DIM ** -0.5, 1.0).astype(F32)
        o_ref[...] = (acc * cos + partner * sin) * scale

    @pl.when(j >= N_ROPE_TILES)
    def _():
        o_ref[...] = acc


def _in_proj(x2, g, w_pad, cos_t, sin_t, seq):
    tokens = x2.shape[0]
    pos_blocks = seq // IN_TM
    return pl.pallas_call(
        _in_proj_kernel,
        grid=(tokens // IN_TM, PROJ_WIDTH // IN_TN),
        in_specs=[
            pl.BlockSpec((IN_TM, D_MODEL), lambda i, j: (i, 0)),
            pl.BlockSpec((1, D_MODEL), lambda i, j: (0, 0)),
            pl.BlockSpec((D_MODEL, IN_TN), lambda i, j: (0, j)),
            pl.BlockSpec((IN_TM, LANES), lambda i, j: (i % pos_blocks, 0)),
            pl.BlockSpec((IN_TM, LANES), lambda i, j: (i % pos_blocks, 0)),
        ],
        out_specs=pl.BlockSpec((IN_TM, IN_TN), lambda i, j: (i, j)),
        out_shape=jax.ShapeDtypeStruct((tokens, PROJ_WIDTH), F32),
        scratch_shapes=[pltpu.VMEM((IN_TM, D_MODEL), BF16)],
        compiler_params=pltpu.CompilerParams(
            dimension_semantics=("arbitrary", "arbitrary"),
            vmem_limit_bytes=VMEM_LIMIT),
        name="in_proj",
    )(x2, g, w_pad, cos_t, sin_t)


def _attention_kernel(q_ref, kc_ref, kp_ref, vc_ref, vp_ref, z_ref, o_ref,
                      kbuf, vbuf, acc_s, m_s, den_s):
    i = pl.program_id(2)
    kbuf[0:SUPER, :] = kp_ref[0]
    kbuf[SUPER:2 * SUPER, :] = kc_ref[0]
    vbuf[0:SUPER, :] = vp_ref[0]
    vbuf[SUPER:2 * SUPER, :] = vc_ref[0]

    qi = lax.broadcasted_iota(jnp.int32, (Q_BLOCK, 2 * Q_BLOCK), 0)
    ki = lax.broadcasted_iota(jnp.int32, (Q_BLOCK, 2 * Q_BLOCK), 1)
    rel = Q_BLOCK + qi - ki
    band = (rel >= 0) & (rel <= N_BACK)
    cur = ki >= Q_BLOCK
    lane = lax.broadcasted_iota(jnp.int32, (Q_BLOCK, LANES), 1)
    head0 = lane < HEAD_DIM

    def block(pat, dil, q_start):
        rows_q = pl.ds(q_start, Q_BLOCK, stride=dil) if dil > 1 else pl.ds(q_start, Q_BLOCK)
        k_start = SUPER + q_start - Q_BLOCK * dil
        rows_k = (pl.ds(k_start, 2 * Q_BLOCK, stride=dil) if dil > 1
                  else pl.ds(k_start, 2 * Q_BLOCK))
        qs = q_ref[0, rows_q, :]
        kw = kbuf[rows_k, :].astype(BF16)
        vw = vbuf[rows_k, :].astype(BF16)
        prev_ok = (i * SUPER + q_start) >= Q_BLOCK * dil
        valid = band & (cur | prev_ok)
        accs, ms, dens = [], [], []
        for hm in (head0, ~head0):
            qh = jnp.where(hm, qs, 0.0).astype(BF16)
            s = _dot_nt(qh, kw)
            s = jnp.where(valid, s, -jnp.inf)
            m = jnp.max(s, axis=-1, keepdims=True)
            p = jnp.exp(s - m)
            dens.append(jnp.sum(p, axis=-1, keepdims=True))
            ms.append(m)
            accs.append(_dot(p.astype(BF16), vw))
        acc_s[pat, rows_q, :] = jnp.where(head0, accs[0], accs[1])
        m_s[pat, rows_q, :] = jnp.where(head0, ms[0], ms[1])
        den_s[pat, rows_q, :] = jnp.where(head0, dens[0], dens[1])

    for pat, (window, dil) in enumerate(DIL_PATTERNS):
        span = Q_BLOCK * dil
        for s_idx in range(SUPER // span):
            for r in range(dil):
                block(pat, dil, s_idx * span + r)

    m0, m1, m2 = m_s[0], m_s[1], m_s[2]
    mx = jnp.maximum(jnp.maximum(m0, m1), m2)
    e0, e1, e2 = jnp.exp(m0 - mx), jnp.exp(m1 - mx), jnp.exp(m2 - mx)
    num = e0 * acc_s[0] + e1 * acc_s[1] + e2 * acc_s[2]
    den = e0 * den_s[0] + e1 * den_s[1] + e2 * den_s[2]
    z = z_ref[0]
    o_ref[0] = (num / den * (z * jax.nn.sigmoid(z))).astype(o_ref.dtype)


def _attention(proj3):
    batch, seq, _ = proj3.shape
    blk = (1, SUPER, LANES)

    def col(base):
        return lambda b, p, i: (b, i, base // LANES + p)

    def col_prev(base):
        return lambda b, p, i: (b, jnp.maximum(i - 1, 0), base // LANES + p)

    return pl.pallas_call(
        _attention_kernel,
        grid=(batch, N_PAIRS, seq // SUPER),
        in_specs=[
            pl.BlockSpec(blk, col(COL_Q)),
            pl.BlockSpec(blk, col(COL_K)),
            pl.BlockSpec(blk, col_prev(COL_K)),
            pl.BlockSpec(blk, col(COL_V)),
            pl.BlockSpec(blk, col_prev(COL_V)),
            pl.BlockSpec(blk, col(COL_ZA)),
        ],
        out_specs=pl.BlockSpec(blk, lambda b, p, i: (b, i, p)),
        out_shape=jax.ShapeDtypeStruct((batch, seq, ATT_WIDTH), BF16),
        scratch_shapes=[
            pltpu.VMEM((2 * SUPER, LANES), F32),
            pltpu.VMEM((2 * SUPER, LANES), F32),
            pltpu.VMEM((len(DIL_PATTERNS), SUPER, LANES), F32),
            pltpu.VMEM((len(DIL_PATTERNS), SUPER, LANES), F32),
            pltpu.VMEM((len(DIL_PATTERNS), SUPER, LANES), F32),
        ],
        compiler_params=pltpu.CompilerParams(
            dimension_semantics=("arbitrary", "arbitrary", "arbitrary"),
            vmem_limit_bytes=VMEM_LIMIT),
        name="dilated_attention",
    )(proj3, proj3, proj3, proj3, proj3, proj3)


def _rwkv_kernel(r_ref, k_ref, v_ref, z_ref, lora_ref,
                 mur_ref, muk_ref, muv_ref, mul_ref,
                 w0_ref, a0_ref, kk_ref, ka_ref, rk_ref, lng_ref, lnb_ref,
                 w2_ref, a2_ref, g2_ref,
                 o_ref,
                 carry_main, carry_lora, state,
                 r_s, k_s, v_s, a_s, b_s, lw_s, gz_s, y_s):
    i = pl.program_id(2)

    @pl.when(i == 0)
    def _():
        carry_main[...] = jnp.zeros_like(carry_main)
        carry_lora[...] = jnp.zeros_like(carry_lora)
        state[...] = jnp.zeros_like(state)

    row_m = lax.broadcasted_iota(jnp.int32, (RWKV_BLOCK, LANES), 0)
    row_l = lax.broadcasted_iota(jnp.int32, (RWKV_BLOCK, LORA_PAD), 0)
    rr = lax.broadcasted_iota(jnp.int32, (LANES, LANES), 0)
    cc = lax.broadcasted_iota(jnp.int32, (LANES, LANES), 1)
    same_head = (rr // HEAD_DIM) == (cc // HEAD_DIM)
    head_ones = same_head.astype(BF16)

    def shifted(u, carry_row, row_iota, mu):
        prev = jnp.where(row_iota == 0, carry_row, pltpu.roll(u, 1, 0))
        return u + (prev - u) * mu

    r_raw, k_raw, v_raw = r_ref[0], k_ref[0], v_ref[0]
    lora_raw = lora_ref[0]
    r = shifted(r_raw, carry_main[0:1, :], row_m, mur_ref[...])
    k = shifted(k_raw, carry_main[1:2, :], row_m, muk_ref[...])
    v = shifted(v_raw, carry_main[2:3, :], row_m, muv_ref[...])
    lora = shifted(lora_raw, carry_lora[...], row_l, mul_ref[...])
    carry_main[0:1, :] = r_raw[RWKV_BLOCK - 1:RWKV_BLOCK, :]
    carry_main[1:2, :] = k_raw[RWKV_BLOCK - 1:RWKV_BLOCK, :]
    carry_main[2:3, :] = v_raw[RWKV_BLOCK - 1:RWKV_BLOCK, :]
    carry_lora[...] = lora_raw[RWKV_BLOCK - 1:RWKV_BLOCK, :]

    wa = lora[:, 0:LANES]
    gl = lora[:, LANES:LORA_PAD]
    w_pre = w0_ref[...] + _bdot(jnp.tanh(wa), w2_ref[...])
    lw_s[...] = -(DECAY_SCALE * jax.nn.sigmoid(w_pre))
    a = jax.nn.sigmoid(a0_ref[...] + _bdot(wa, a2_ref[...]))
    g = _bdot(jax.nn.sigmoid(gl), g2_ref[...])
    z = z_ref[0]
    gz_s[...] = g * (z * jax.nn.sigmoid(z))

    kk = k * kk_ref[...]
    ss = _dot_split(kk * kk, head_ones, 2)
    kk = kk / jnp.maximum(jnp.sqrt(ss), 1e-12)
    r_s[...] = r
    k_s[...] = k * (1.0 + (a - 1.0) * ka_ref[...])
    v_s[...] = v
    a_s[...] = -kk
    b_s[...] = kk * a

    head0 = lax.broadcasted_iota(jnp.int32, (CHUNK, LANES), 1) < HEAD_DIM
    tt = rr % CHUNK
    ts = cc % CHUNK
    strict = same_head & (ts < tt)
    incl = same_head & (ts <= tt)
    tril = (lax.broadcasted_iota(jnp.int32, (CHUNK, CHUNK), 1)
            <= lax.broadcasted_iota(jnp.int32, (CHUNK, CHUNK), 0)).astype(BF16)
    eye = (rr == cc).astype(F32)

    def bd(x):
        return jnp.concatenate([jnp.where(head0, x, 0.0), jnp.where(head0, 0.0, x)], axis=0)

    def st(x):
        return jnp.concatenate([x, x], axis=0)

    chunks = range(RWKV_BLOCK // CHUNK)
    rows = [slice(c * CHUNK, (c + 1) * CHUNK) for c in chunks]
    lw = [lw_s[r, :] for r in rows]
    cs = [_split_dot(tril, x, 3) for x in lw]
    e_pos = [jnp.exp(x) for x in cs]
    e_neg = [jnp.exp(-x) for x in cs]
    e_excl = [jnp.exp(x - y) for x, y in zip(cs, lw)]
    decay_end = [x[CHUNK - 1:CHUNK, :] for x in e_pos]
    e_end = [d * x for d, x in zip(decay_end, e_neg)]
    a_t = [bd(a_s[r, :] * x).astype(BF16) for r, x in zip(rows, e_excl)]
    r_t = [bd(r_s[r, :] * x) for r, x in zip(rows, e_pos)]
    k_t = [k_s[r, :] * x for r, x in zip(rows, e_neg)]
    b_t = [b_s[r, :] * x for r, x in zip(rows, e_neg)]
    kb_h = [jnp.concatenate([bd(b_s[r, :] * x), bd(k_s[r, :] * x)], axis=0).astype(BF16)
            for r, x in zip(rows, e_end)]
    vb = [bd(v_s[r, :]).astype(BF16) for r in rows]

    sc = [_dot_nt(jnp.concatenate([a, r.astype(BF16)], axis=0),
                  jnp.concatenate([st(k), st(b)], axis=0).astype(BF16))
          for a, r, k, b in zip(a_t, r_t, k_t, b_t)]
    a_ak = [jnp.where(strict, x[0:LANES, 0:LANES], 0.0).astype(BF16) for x in sc]
    a_ab = [jnp.where(strict, x[0:LANES, LANES:], 0.0) for x in sc]
    a_r = [jnp.concatenate([jnp.where(incl, x[LANES:, 0:LANES], 0.0),
                            jnp.where(incl, x[LANES:, LANES:], 0.0)], axis=1).astype(BF16)
           for x in sc]

    p_acc = [eye + x for x in a_ab]
    q_pow = [x.astype(BF16) for x in a_ab]
    for step in range(5):
        q_new = [_dot(q, q) for q in q_pow]
        q_pow = [q.astype(BF16) for q in q_new]
        p_acc = [p + _dot(p.astype(BF16), q) for p, q in zip(p_acc, q_pow)]
    t_inv = [p.astype(BF16) for p in p_acc]

    ak_v = [_dot(x, v) for x, v in zip(a_ak, vb)]
    wu = [_dot(t, jnp.concatenate([a, x.astype(BF16)], axis=1))
          for t, a, x in zip(t_inv, a_t, ak_v)]
    w_mat = [x[:, 0:LANES] for x in wu]
    u0 = [x[:, LANES:] for x in wu]
    m0 = [_dot(w.T.astype(BF16), x[0:LANES, :]).astype(BF16) for w, x in zip(w_mat, kb_h)]
    g_add = [_dot(jnp.concatenate([u, v.astype(F32)], axis=0).T.astype(BF16), x)
             for u, v, x in zip(u0, vb, kb_h)]
    rw = [(r + _dot(x[:, LANES:], w.astype(BF16))).astype(BF16)
          for r, x, w in zip(r_t, a_r, w_mat)]
    y0 = [_dot(x, jnp.concatenate([v, u.astype(BF16)], axis=0))
          for x, v, u in zip(a_r, vb, u0)]

    ht = state[...]
    for c in chunks:
        ht_b = ht.astype(BF16)
        y = _dot_nt(rw[c], ht_b) + y0[c]
        y_s[rows[c], :] = y[0:CHUNK, :] + y[CHUNK:, :]
        ht = ht * decay_end[c] + _dot(ht_b, m0[c]) + g_add[c]
    state[...] = ht

    y = y_s[...]
    mean = _dot_split(y, head_ones, 2) * (1.0 / HEAD_DIM)
    yc = y - mean
    var = _dot_split(yc * yc, head_ones, 2) * (1.0 / HEAD_DIM)
    yn = yc * lax.rsqrt(var + LNX_EPS) * lng_ref[...] + lnb_ref[...]
    bonus = _dot_split(r_s[...] * k_s[...] * rk_ref[...], head_ones, 2) * v_s[...]
    o_ref[0] = ((yn + bonus) * gz_s[...]).astype(o_ref.dtype)


def _rwkv(proj3, mu_r, mu_k, mu_v, mu_l, w0, a0, k_k, k_a, r_k, lnx_g, lnx_b,
          w2p, a2p, g2p):
    batch, seq, _ = proj3.shape
    blk = (1, RWKV_BLOCK, LANES)

    def col(base):
        return lambda b, p, i: (b, i, base // LANES + p)

    vec = pl.BlockSpec((1, LANES), lambda b, p, i: (0, p))
    return pl.pallas_call(
        _rwkv_kernel,
        grid=(batch, N_PAIRS, seq // RWKV_BLOCK),
        in_specs=[
            pl.BlockSpec(blk, col(COL_R)),
            pl.BlockSpec(blk, col(COL_RK)),
            pl.BlockSpec(blk, col(COL_RV)),
            pl.BlockSpec(blk, col(COL_ZR)),
            pl.BlockSpec((1, RWKV_BLOCK, LORA_PAD),
                         lambda b, p, i: (b, i, COL_LORA // LORA_PAD)),
            vec, vec, vec,
            pl.BlockSpec((1, LORA_PAD), lambda b, p, i: (0, 0)),
            vec, vec, vec, vec, vec, vec, vec,
            pl.BlockSpec((LANES, LANES), lambda b, p, i: (0, p)),
            pl.BlockSpec((LANES, LANES), lambda b, p, i: (0, p)),
            pl.BlockSpec((LORA_PAD - LANES, LANES), lambda b, p, i: (0, p)),
        ],
        out_specs=pl.BlockSpec(blk, lambda b, p, i: (b, i, p)),
        out_shape=jax.ShapeDtypeStruct((batch, seq, RWKV_WIDTH), BF16),
        scratch_shapes=[
            pltpu.VMEM((8, LANES), F32),
            pltpu.VMEM((1, LORA_PAD), F32),
            pltpu.VMEM((LANES, LANES), F32),
        ] + [pltpu.VMEM((RWKV_BLOCK, LANES), F32)] * 8,
        compiler_params=pltpu.CompilerParams(
            dimension_semantics=("arbitrary", "arbitrary", "arbitrary"),
            vmem_limit_bytes=VMEM_LIMIT),
        name="rwkv7_time_mix",
    )(proj3, proj3, proj3, proj3, proj3, mu_r, mu_k, mu_v, mu_l,
      w0, a0, k_k, k_a, r_k, lnx_g, lnx_b, w2p, a2p, g2p)


def _out_proj_kernel(att_ref, rwk_ref, wa_ref, wr_ref, x_ref, g_ref, o_ref, *, normalize):
    y = x_ref[...] + _dot(att_ref[...], wa_ref[...]) + _dot(rwk_ref[...], wr_ref[...])
    if normalize:
        ms = jnp.mean(y * y, axis=-1, keepdims=True)
        y = y * lax.rsqrt(ms + NORM_EPS) * g_ref[...]
    o_ref[...] = y


def _out_proj(att2, rwk2, w_att, w_rwk, x2, g, normalize):
    tokens = x2.shape[0]
    return pl.pallas_call(
        functools.partial(_out_proj_kernel, normalize=normalize),
        grid=(tokens // OUT_TM,),
        in_specs=[
            pl.BlockSpec((OUT_TM, ATT_WIDTH), lambda i: (i, 0)),
            pl.BlockSpec((OUT_TM, RWKV_WIDTH), lambda i: (i, 0)),
            pl.BlockSpec((ATT_WIDTH, D_MODEL), lambda i: (0, 0)),
            pl.BlockSpec((RWKV_WIDTH, D_MODEL), lambda i: (0, 0)),
            pl.BlockSpec((OUT_TM, D_MODEL), lambda i: (i, 0)),
            pl.BlockSpec((1, D_MODEL), lambda i: (0, 0)),
        ],
        out_specs=pl.BlockSpec((OUT_TM, D_MODEL), lambda i: (i, 0)),
        out_shape=jax.ShapeDtypeStruct((tokens, D_MODEL), F32),
        compiler_params=pltpu.CompilerParams(
            dimension_semantics=("arbitrary",),
            vmem_limit_bytes=VMEM_LIMIT),
        name="out_proj",
    )(att2, rwk2, w_att, w_rwk, x2, g)


def _rope_tables(seq):
    inv_freq = ROPE_THETA ** (-jnp.arange(0, HEAD_DIM, 2, dtype=F32) / HEAD_DIM)
    ang = jnp.arange(seq, dtype=jnp.int32).astype(F32)[:, None] * inv_freq[None, :]
    cos, sin = jnp.cos(ang), jnp.sin(ang)
    reps = LANES // HEAD_DIM
    cos_t = jnp.tile(jnp.concatenate([cos, cos], axis=1), (1, reps))
    sin_t = jnp.tile(jnp.concatenate([-sin, sin], axis=1), (1, reps))
    return cos_t, sin_t


def _layer(x, norm_g, w_in, shift_mu, w0, w2, a0, a2, g2, k_k, k_a, r_k,
           lnx_g, lnx_b, w_out, cos_t, sin_t):
    batch, seq, _ = x.shape
    a_w, r_w = ATT_WIDTH, RWKV_WIDTH
    lo = 4 * a_w + 3 * r_w
    n_lora = DECAY_LORA + AAA_LORA + GATE_LORA
    w_pad = jnp.concatenate([
        w_in[:, :lo],
        w_in[:, lo + n_lora:],
        w_in[:, lo:lo + n_lora],
        jnp.zeros((D_MODEL, LORA_PAD - n_lora), w_in.dtype)], axis=1).astype(BF16)
    mu_r = shift_mu[None, 0:r_w]
    mu_k = shift_mu[None, r_w:2 * r_w]
    mu_v = shift_mu[None, 2 * r_w:3 * r_w]
    mu_l = jnp.pad(shift_mu[None, 3 * r_w:], ((0, 0), (0, LORA_PAD - n_lora)))
    w2p = jnp.pad(w2, ((0, LANES - DECAY_LORA), (0, 0)))
    a2p = jnp.pad(a2, ((DECAY_LORA, LANES - DECAY_LORA - AAA_LORA), (0, 0)))
    g2p = jnp.pad(g2, ((0, LORA_PAD - LANES - GATE_LORA), (0, 0)))

    x2 = x.reshape(batch * seq, D_MODEL)
    proj = _in_proj(x2, norm_g[None, :], w_pad, cos_t, sin_t, seq)
    proj3 = proj.reshape(batch, seq, PROJ_WIDTH)
    att = _attention(proj3)
    rwk = _rwkv(proj3, mu_r, mu_k, mu_v, mu_l, w0[None, :], a0[None, :],
                k_k[None, :], k_a[None, :], r_k.reshape(1, r_w), lnx_g[None, :],
                lnx_b[None, :], w2p, a2p, g2p)
    w_out_b = w_out.astype(BF16)
    return att.reshape(batch * seq, a_w), rwk.reshape(batch * seq, r_w), w_out_b, x2


def kernel(x, norm_g, w_in, shift_mu, w0, w2, a0, a2, g2, k_k, k_a, r_k,
           lnx_g, lnx_b, w_out, final_g):
    batch, seq, _ = x.shape
    depth = norm_g.shape[0]
    assert seq % SUPER == 0 and seq % IN_TM == 0 and (batch * seq) % OUT_TM == 0
    cos_t, sin_t = _rope_tables(seq)
    for l in range(depth):
        att2, rwk2, w_out_b, x2 = _layer(
            x, norm_g[l], w_in[l], shift_mu[l], w0[l], w2[l], a0[l], a2[l], g2[l],
            k_k[l], k_a[l], r_k[l], lnx_g[l], lnx_b[l], w_out[l], cos_t, sin_t)
        y2 = _out_proj(att2, rwk2, w_out_b[:ATT_WIDTH], w_out_b[ATT_WIDTH:], x2,
                       final_g[None, :], normalize=(l == depth - 1))
        x = y2.reshape(batch, seq, D_MODEL)
    return x
```

```python
import functools

import jax
import jax.numpy as jnp
from jax import lax
from jax.experimental import pallas as pl
from jax.experimental.pallas import tpu as pltpu

F32 = jnp.float32
BF16 = jnp.bfloat16

D_MODEL = 2048
HEAD_DIM = 64
ATT_WIDTH = 1024
RWKV_WIDTH = 1024
DIL_PATTERNS = ((128, 1), (512, 4), (2048, 16))
ROPE_THETA = 10000.0
DECAY_LORA = 64
AAA_LORA = 64
GATE_LORA = 160
NORM_EPS = 1e-5
LNX_EPS = 64e-5
DECAY_SCALE = 0.6065306597126334

LANES = 128
N_PAIRS = ATT_WIDTH // LANES
N_BACK = 128
Q_BLOCK = 128
SUPER = 2048
CHUNK = 64
RWKV_BLOCK = 512
LORA_PAD = 512
GATE_PAD = 256

COL_Q, COL_K, COL_V, COL_ZA = 0, 1024, 2048, 3072
COL_R, COL_RK, COL_RV, COL_ZR = 4096, 5120, 6144, 7168
PROJ_WIDTH = 8192

IN_TM, IN_TN = 1024, 1024
IN_SUB = 512
N_ROPE_STEPS = (COL_V - COL_Q) // IN_TN
N_Q_STEPS = (COL_K - COL_Q) // IN_TN
OUT_TM = 512
VMEM_LIMIT = 56 * 1024 * 1024


def _dot(a, b, precision=None):
    return jnp.dot(a, b, preferred_element_type=F32, precision=precision)


def _dot_nt(a, b, precision=None):
    return lax.dot_general(a, b, (((1,), (1,)), ((), ())),
                           preferred_element_type=F32, precision=precision)


def _bdot(a, b):
    return _dot(a.astype(BF16), b.astype(BF16))


def _bdot_nt(a, b):
    return _dot_nt(a.astype(BF16), b.astype(BF16))


def _bf16_pieces(x, terms):
    pieces = []
    for _ in range(terms):
        piece = x.astype(BF16)
        pieces.append(piece)
        x = x - piece.astype(F32)
    return pieces


def _head_sum(x, ones2):
    return _dot(jnp.concatenate(_bf16_pieces(x, 2), axis=1), ones2)


def _in_proj_kernel(x_ref, g_ref, w_ref, wl_ref, cos_ref, sin_ref, o_ref, lora_ref, h_ref):
    j = pl.program_id(1)
    subs = [slice(s * IN_SUB, (s + 1) * IN_SUB) for s in range(IN_TN // IN_SUB)]

    @pl.when(j == 0)
    def _():
        x = x_ref[...]
        ms = jnp.mean(x * x, axis=-1, keepdims=True)
        h_ref[...] = (x * lax.rsqrt(ms + NORM_EPS) * g_ref[...]).astype(BF16)
        lora_ref[...] = _dot(h_ref[...], wl_ref[...])

    @pl.when(j < N_ROPE_STEPS)
    def _():
        reps = IN_SUB // LANES
        cos = jnp.concatenate([cos_ref[...]] * reps, axis=1)
        sin = jnp.concatenate([sin_ref[...]] * reps, axis=1)
        lane = lax.broadcasted_iota(jnp.int32, (IN_TM, IN_SUB), 1)
        first_half = (lane % HEAD_DIM) < (HEAD_DIM // 2)
        scale = jnp.where(j < N_Q_STEPS, HEAD_DIM ** -0.5, 1.0).astype(F32)
        for sub in subs:
            acc = _dot(h_ref[...], w_ref[:, sub])
            partner = jnp.where(first_half,
                                pltpu.roll(acc, IN_SUB - HEAD_DIM // 2, 1),
                                pltpu.roll(acc, HEAD_DIM // 2, 1))
            o_ref[:, sub] = (acc * cos + partner * sin) * scale

    @pl.when(j >= N_ROPE_STEPS)
    def _():
        for sub in subs:
            o_ref[:, sub] = _dot(h_ref[...], w_ref[:, sub])


def _in_proj(x2, g, w_main, w_lora, cos_t, sin_t, seq):
    tokens = x2.shape[0]
    pos_blocks = seq // IN_TM
    return pl.pallas_call(
        _in_proj_kernel,
        grid=(tokens // IN_TM, PROJ_WIDTH // IN_TN),
        in_specs=[
            pl.BlockSpec((IN_TM, D_MODEL), lambda i, j: (i, 0)),
            pl.BlockSpec((1, D_MODEL), lambda i, j: (0, 0)),
            pl.BlockSpec((D_MODEL, IN_TN), lambda i, j: (0, j)),
            pl.BlockSpec((D_MODEL, LORA_PAD), lambda i, j: (0, 0)),
            pl.BlockSpec((IN_TM, LANES), lambda i, j: (i % pos_blocks, 0)),
            pl.BlockSpec((IN_TM, LANES), lambda i, j: (i % pos_blocks, 0)),
        ],
        out_specs=[
            pl.BlockSpec((IN_TM, IN_TN), lambda i, j: (i, j)),
            pl.BlockSpec((IN_TM, LORA_PAD), lambda i, j: (i, 0)),
        ],
        out_shape=[
            jax.ShapeDtypeStruct((tokens, PROJ_WIDTH), F32),
            jax.ShapeDtypeStruct((tokens, LORA_PAD), F32),
        ],
        scratch_shapes=[pltpu.VMEM((IN_TM, D_MODEL), BF16)],
        compiler_params=pltpu.CompilerParams(
            dimension_semantics=("arbitrary", "arbitrary"),
            vmem_limit_bytes=VMEM_LIMIT),
        name="in_proj",
    )(x2, g, w_main, w_lora, cos_t, sin_t)


def _attention_kernel(q_ref, kc_ref, kp_ref, vc_ref, vp_ref, z_ref, o_ref,
                      kbuf, vbuf, acc_s, m_s, den_s):
    i = pl.program_id(2)
    kbuf[0:SUPER, :] = kp_ref[0]
    kbuf[SUPER:2 * SUPER, :] = kc_ref[0]
    vbuf[0:SUPER, :] = vp_ref[0]
    vbuf[SUPER:2 * SUPER, :] = vc_ref[0]

    qi = lax.broadcasted_iota(jnp.int32, (Q_BLOCK, 2 * Q_BLOCK), 0)
    ki = lax.broadcasted_iota(jnp.int32, (Q_BLOCK, 2 * Q_BLOCK), 1)
    rel = Q_BLOCK + qi - ki
    band = (rel >= 0) & (rel <= N_BACK)
    cur = ki >= Q_BLOCK
    lane = lax.broadcasted_iota(jnp.int32, (Q_BLOCK, LANES), 1)
    head0 = lane < HEAD_DIM

    def block(pat, dil, q_start):
        rows_q = pl.ds(q_start, Q_BLOCK, stride=dil) if dil > 1 else pl.ds(q_start, Q_BLOCK)
        k_start = SUPER + q_start - Q_BLOCK * dil
        rows_k = (pl.ds(k_start, 2 * Q_BLOCK, stride=dil) if dil > 1
                  else pl.ds(k_start, 2 * Q_BLOCK))
        qs = q_ref[0, rows_q, :]
        kw = kbuf[rows_k, :].astype(BF16)
        vw = vbuf[rows_k, :].astype(BF16)
        prev_ok = (i * SUPER + q_start) >= Q_BLOCK * dil
        valid = band & (cur | prev_ok)
        accs, ms, dens = [], [], []
        for hm in (head0, ~head0):
            qh = jnp.where(hm, qs, 0.0).astype(BF16)
            s = _dot_nt(qh, kw)
            s = jnp.where(valid, s, -jnp.inf)
            m = jnp.max(s, axis=-1, keepdims=True)
            p = jnp.exp(s - m)
            dens.append(jnp.sum(p, axis=-1, keepdims=True))
            ms.append(m)
            accs.append(_dot(p.astype(BF16), vw))
        acc_s[pat, rows_q, :] = jnp.where(head0, accs[0], accs[1])
        m_s[pat, rows_q, :] = jnp.where(head0, ms[0], ms[1])
        den_s[pat, rows_q, :] = jnp.where(head0, dens[0], dens[1])

    for pat, (window, dil) in enumerate(DIL_PATTERNS):
        span = Q_BLOCK * dil
        for s_idx in range(SUPER // span):
            for r in range(dil):
                block(pat, dil, s_idx * span + r)

    m0, m1, m2 = m_s[0], m_s[1], m_s[2]
    mx = jnp.maximum(jnp.maximum(m0, m1), m2)
    e0, e1, e2 = jnp.exp(m0 - mx), jnp.exp(m1 - mx), jnp.exp(m2 - mx)
    num = e0 * acc_s[0] + e1 * acc_s[1] + e2 * acc_s[2]
    den = e0 * den_s[0] + e1 * den_s[1] + e2 * den_s[2]
    z = z_ref[0]
    o_ref[0] = (num / den * (z * jax.nn.sigmoid(z))).astype(o_ref.dtype)


def _attention(proj3):
    batch, seq, _ = proj3.shape
    blk = (1, SUPER, LANES)

    def col(base):
        return lambda b, p, i: (b, i, base // LANES + p)

    def col_prev(base):
        return lambda b, p, i: (b, jnp.maximum(i - 1, 0), base // LANES + p)

    return pl.pallas_call(
        _attention_kernel,
        grid=(batch, N_PAIRS, seq // SUPER),
        in_specs=[
            pl.BlockSpec(blk, col(COL_Q)),
            pl.BlockSpec(blk, col(COL_K)),
            pl.BlockSpec(blk, col_prev(COL_K)),
            pl.BlockSpec(blk, col(COL_V)),
            pl.BlockSpec(blk, col_prev(COL_V)),
            pl.BlockSpec(blk, col(COL_ZA)),
        ],
        out_specs=pl.BlockSpec(blk, lambda b, p, i: (b, i, p)),
        out_shape=jax.ShapeDtypeStruct((batch, seq, ATT_WIDTH), BF16),
        scratch_shapes=[
            pltpu.VMEM((2 * SUPER, LANES), F32),
            pltpu.VMEM((2 * SUPER, LANES), F32),
            pltpu.VMEM((len(DIL_PATTERNS), SUPER, LANES), F32),
            pltpu.VMEM((len(DIL_PATTERNS), SUPER, LANES), F32),
            pltpu.VMEM((len(DIL_PATTERNS), SUPER, LANES), F32),
        ],
        compiler_params=pltpu.CompilerParams(
            dimension_semantics=("arbitrary", "arbitrary", "arbitrary"),
            vmem_limit_bytes=VMEM_LIMIT),
        name="dilated_attention",
    )(proj3, proj3, proj3, proj3, proj3, proj3)


def _rwkv_kernel(r_ref, k_ref, v_ref, z_ref, lora_ref,
                 mur_ref, muk_ref, muv_ref, mul_ref,
                 w0_ref, a0_ref, kk_ref, ka_ref, rk_ref, lng_ref, lnb_ref,
                 wa2_ref, g2_ref,
                 o_ref,
                 carry_main, carry_lora, state,
                 r_s, k_s, v_s, a_s, b_s, lw_s, gz_s, y_s):
    i = pl.program_id(2)

    @pl.when(i == 0)
    def _():
        carry_main[...] = jnp.zeros_like(carry_main)
        carry_lora[...] = jnp.zeros_like(carry_lora)
        state[...] = jnp.zeros_like(state)

    row_m = lax.broadcasted_iota(jnp.int32, (RWKV_BLOCK, LANES), 0)
    row_l = lax.broadcasted_iota(jnp.int32, (RWKV_BLOCK, LORA_PAD), 0)
    rr = lax.broadcasted_iota(jnp.int32, (LANES, LANES), 0)
    cc = lax.broadcasted_iota(jnp.int32, (LANES, LANES), 1)
    same_head = (rr // HEAD_DIM) == (cc // HEAD_DIM)
    head_ones2 = ((lax.broadcasted_iota(jnp.int32, (2 * LANES, LANES), 0) % LANES) // HEAD_DIM
                  == lax.broadcasted_iota(jnp.int32, (2 * LANES, LANES), 1) // HEAD_DIM
                  ).astype(BF16)

    def shifted(u, carry_row, row_iota, mu):
        prev = jnp.where(row_iota == 0, carry_row, pltpu.roll(u, 1, 0))
        return u + (prev - u) * mu

    r_raw, k_raw, v_raw = r_ref[0], k_ref[0], v_ref[0]
    lora_raw = lora_ref[0]
    r = shifted(r_raw, carry_main[0:1, :], row_m, mur_ref[...])
    k = shifted(k_raw, carry_main[1:2, :], row_m, muk_ref[...])
    v = shifted(v_raw, carry_main[2:3, :], row_m, muv_ref[...])
    lora = shifted(lora_raw, carry_lora[...], row_l, mul_ref[...])
    carry_main[0:1, :] = r_raw[RWKV_BLOCK - 1:RWKV_BLOCK, :]
    carry_main[1:2, :] = k_raw[RWKV_BLOCK - 1:RWKV_BLOCK, :]
    carry_main[2:3, :] = v_raw[RWKV_BLOCK - 1:RWKV_BLOCK, :]
    carry_lora[...] = lora_raw[RWKV_BLOCK - 1:RWKV_BLOCK, :]

    wa = lora[:, 0:LANES]
    lane_m = lax.broadcasted_iota(jnp.int32, (RWKV_BLOCK, LANES), 1)
    wa_act = jnp.where(lane_m < DECAY_LORA, jnp.tanh(wa), wa)
    wa_out = _bdot(wa_act, wa2_ref[...])
    w_pre = w0_ref[...] + wa_out[:, 0:LANES]
    lw_s[...] = -(DECAY_SCALE * jax.nn.sigmoid(w_pre))
    a = jax.nn.sigmoid(a0_ref[...] + wa_out[:, LANES:])
    gl = lora[:, LANES:LANES + GATE_PAD]
    g = _bdot(jax.nn.sigmoid(gl), g2_ref[...])
    z = z_ref[0]
    gz_s[...] = g * (z * jax.nn.sigmoid(z))

    kk = k * kk_ref[...]
    ss = _head_sum(kk * kk, head_ones2)
    kk = kk / jnp.maximum(jnp.sqrt(ss), 1e-12)
    r_s[...] = r
    k_s[...] = k * (1.0 + (a - 1.0) * ka_ref[...])
    v_s[...] = v
    a_s[...] = -kk
    b_s[...] = kk * a

    head0 = lax.broadcasted_iota(jnp.int32, (CHUNK, LANES), 1) < HEAD_DIM
    tt = rr % CHUNK
    ts = cc % CHUNK
    strict = same_head & (ts < tt)
    incl = same_head & (ts <= tt)
    tril = (lax.broadcasted_iota(jnp.int32, (CHUNK, CHUNK), 1)
            <= lax.broadcasted_iota(jnp.int32, (CHUNK, CHUNK), 0)).astype(BF16)
    eye = (rr == cc).astype(F32)

    def bd(x):
        return jnp.concatenate([jnp.where(head0, x, 0.0), jnp.where(head0, 0.0, x)], axis=0)

    def st(x):
        return jnp.concatenate([x, x], axis=0)

    chunks = range(RWKV_BLOCK // CHUNK)
    rows = [slice(c * CHUNK, (c + 1) * CHUNK) for c in chunks]
    lw = [lw_s[r, :] for r in rows]
    cs = []
    for x in lw:
        parts = _dot(tril, jnp.concatenate(_bf16_pieces(x, 3), axis=1))
        cs.append(parts[:, 0:LANES] + parts[:, LANES:2 * LANES] + parts[:, 2 * LANES:])
    e_pos = [jnp.exp(x) for x in cs]
    e_neg = [jnp.exp(-x) for x in cs]
    e_excl = [jnp.exp(x - y) for x, y in zip(cs, lw)]
    decay_end = [x[CHUNK - 1:CHUNK, :] for x in e_pos]
    e_end = [d * x for d, x in zip(decay_end, e_neg)]
    a_t = [bd(a_s[r, :] * x).astype(BF16) for r, x in zip(rows, e_excl)]
    r_t = [bd(r_s[r, :] * x) for r, x in zip(rows, e_pos)]
    k_t = [k_s[r, :] * x for r, x in zip(rows, e_neg)]
    b_t = [b_s[r, :] * x for r, x in zip(rows, e_neg)]
    bkd_t = [jnp.concatenate([bd(b_s[r, :] * x), bd(k_s[r, :] * x),
                              jnp.broadcast_to(d, (LANES, LANES))], axis=0).T
             for r, x, d in zip(rows, e_end, decay_end)]
    vb = [bd(v_s[r, :]).astype(BF16) for r in rows]

    sc = [_dot_nt(jnp.concatenate([a, r.astype(BF16)], axis=0),
                  jnp.concatenate([st(k), st(b)], axis=0).astype(BF16))
          for a, r, k, b in zip(a_t, r_t, k_t, b_t)]
    a_ak = [jnp.where(strict, x[0:LANES, 0:LANES], 0.0).astype(BF16) for x in sc]
    a_ab = [jnp.where(strict, x[0:LANES, LANES:], 0.0) for x in sc]
    a_r = [jnp.concatenate([jnp.where(incl, x[LANES:, 0:LANES], 0.0),
                            jnp.where(incl, x[LANES:, LANES:], 0.0)], axis=1).astype(BF16)
           for x in sc]

    q_pow = [x.astype(BF16) for x in a_ab]
    p_acc = [eye + x for x in a_ab]
    q_pow = [_dot(q, q).astype(BF16) for q in q_pow]
    for step in range(4):
        qp = [_dot(q, jnp.concatenate([q, p.astype(BF16)], axis=1))
              for q, p in zip(q_pow, p_acc)]
        q_pow = [x[:, 0:LANES].astype(BF16) for x in qp]
        p_acc = [p + x[:, LANES:] for p, x in zip(p_acc, qp)]
    p_acc = [p + _dot(q, p.astype(BF16)) for q, p in zip(q_pow, p_acc)]
    t_inv = [p.astype(BF16) for p in p_acc]

    ak_v = [_dot(x, v) for x, v in zip(a_ak, vb)]
    wu = [_dot(t, jnp.concatenate([a, x.astype(BF16)], axis=1)).astype(BF16)
          for t, a, x in zip(t_inv, a_t, ak_v)]
    zero = jnp.zeros((LANES, LANES), BF16)
    yw = [_dot(x, jnp.concatenate(
              [jnp.concatenate([v, zero], axis=1),
               jnp.concatenate([wu_c[:, LANES:], wu_c[:, 0:LANES]], axis=1)], axis=0))
          for x, v, wu_c in zip(a_r, vb, wu)]
    y0 = [x[:, 0:LANES] for x in yw]
    rw = [(r + x[:, LANES:]).astype(BF16) for r, x in zip(r_t, yw)]
    mg = [_dot(x[:, 0:2 * LANES].astype(BF16),
               jnp.concatenate([wu_c, jnp.concatenate([zero, v], axis=1)], axis=0))
          for x, wu_c, v in zip(bkd_t, wu, vb)]
    m_mat = [x[:, 0:LANES].astype(BF16) for x in mg]
    g_add = [x[:, LANES:] for x in mg]
    decay_col = [x[:, 2 * LANES:] for x in bkd_t]

    hs = state[...]
    for c in chunks:
        hs_b = hs.astype(BF16)
        y = _dot(rw[c], hs_b) + y0[c]
        y_s[rows[c], :] = y[0:CHUNK, :] + y[CHUNK:, :]
        hs = hs * decay_col[c] + _dot(m_mat[c], hs_b) + g_add[c]
    state[...] = hs

    y = y_s[...]
    mean = _head_sum(y, head_ones2) * (1.0 / HEAD_DIM)
    yc = y - mean
    var = _head_sum(yc * yc, head_ones2) * (1.0 / HEAD_DIM)
    yn = yc * lax.rsqrt(var + LNX_EPS) * lng_ref[...] + lnb_ref[...]
    bonus = _head_sum(r_s[...] * k_s[...] * rk_ref[...], head_ones2) * v_s[...]
    o_ref[0] = ((yn + bonus) * gz_s[...]).astype(o_ref.dtype)


def _rwkv(proj3, lora3, mu_r, mu_k, mu_v, mu_l, w0, a0, k_k, k_a, r_k, lnx_g, lnx_b,
          wa2, g2p):
    batch, seq, _ = proj3.shape
    blk = (1, RWKV_BLOCK, LANES)

    def col(base):
        return lambda b, p, i: (b, i, base // LANES + p)

    vec = pl.BlockSpec((1, LANES), lambda b, p, i: (0, p))
    return pl.pallas_call(
        _rwkv_kernel,
        grid=(batch, N_PAIRS, seq // RWKV_BLOCK),
        in_specs=[
            pl.BlockSpec(blk, col(COL_R)),
            pl.BlockSpec(blk, col(COL_RK)),
            pl.BlockSpec(blk, col(COL_RV)),
            pl.BlockSpec(blk, col(COL_ZR)),
            pl.BlockSpec((1, RWKV_BLOCK, LORA_PAD), lambda b, p, i: (b, i, 0)),
            vec, vec, vec,
            pl.BlockSpec((1, LORA_PAD), lambda b, p, i: (0, 0)),
            vec, vec, vec, vec, vec, vec, vec,
            pl.BlockSpec((LANES, 2 * LANES), lambda b, p, i: (0, p)),
            pl.BlockSpec((GATE_PAD, LANES), lambda b, p, i: (0, p)),
        ],
        out_specs=pl.BlockSpec(blk, lambda b, p, i: (b, i, p)),
        out_shape=jax.ShapeDtypeStruct((batch, seq, RWKV_WIDTH), BF16),
        scratch_shapes=[
            pltpu.VMEM((8, LANES), F32),
            pltpu.VMEM((1, LORA_PAD), F32),
            pltpu.VMEM((LANES, LANES), F32),
        ] + [pltpu.VMEM((RWKV_BLOCK, LANES), F32)] * 8,
        compiler_params=pltpu.CompilerParams(
            dimension_semantics=("arbitrary", "arbitrary", "arbitrary"),
            vmem_limit_bytes=VMEM_LIMIT),
        name="rwkv7_time_mix",
    )(proj3, proj3, proj3, proj3, lora3, mu_r, mu_k, mu_v, mu_l,
      w0, a0, k_k, k_a, r_k, lnx_g, lnx_b, wa2, g2p)


def _out_proj_kernel(att_ref, rwk_ref, wa_ref, wr_ref, x_ref, g_ref, o_ref, *, normalize):
    y = x_ref[...] + _dot(att_ref[...], wa_ref[...]) + _dot(rwk_ref[...], wr_ref[...])
    if normalize:
        ms = jnp.mean(y * y, axis=-1, keepdims=True)
        y = y * lax.rsqrt(ms + NORM_EPS) * g_ref[...]
    o_ref[...] = y


def _out_proj(att2, rwk2, w_out_b, x2, g, normalize):
    tokens = x2.shape[0]
    return pl.pallas_call(
        functools.partial(_out_proj_kernel, normalize=normalize),
        grid=(tokens // OUT_TM,),
        in_specs=[
            pl.BlockSpec((OUT_TM, ATT_WIDTH), lambda i: (i, 0)),
            pl.BlockSpec((OUT_TM, RWKV_WIDTH), lambda i: (i, 0)),
            pl.BlockSpec((ATT_WIDTH, D_MODEL), lambda i: (0, 0)),
            pl.BlockSpec((RWKV_WIDTH, D_MODEL), lambda i: (ATT_WIDTH // RWKV_WIDTH, 0)),
            pl.BlockSpec((OUT_TM, D_MODEL), lambda i: (i, 0)),
            pl.BlockSpec((1, D_MODEL), lambda i: (0, 0)),
        ],
        out_specs=pl.BlockSpec((OUT_TM, D_MODEL), lambda i: (i, 0)),
        out_shape=jax.ShapeDtypeStruct((tokens, D_MODEL), F32),
        compiler_params=pltpu.CompilerParams(
            dimension_semantics=("arbitrary",),
            vmem_limit_bytes=VMEM_LIMIT),
        name="out_proj",
    )(att2, rwk2, w_out_b, w_out_b, x2, g)


def _rope_tables(seq):
    inv_freq = ROPE_THETA ** (-jnp.arange(0, HEAD_DIM, 2, dtype=F32) / HEAD_DIM)
    ang = jnp.arange(seq, dtype=jnp.int32).astype(F32)[:, None] * inv_freq[None, :]
    cos, sin = jnp.cos(ang), jnp.sin(ang)
    reps = LANES // HEAD_DIM
    cos_t = jnp.tile(jnp.concatenate([cos, cos], axis=1), (1, reps))
    sin_t = jnp.tile(jnp.concatenate([-sin, sin], axis=1), (1, reps))
    return cos_t, sin_t


def _layer(x, norm_g, w_in, shift_mu, w0, w2, a0, a2, g2, k_k, k_a, r_k,
           lnx_g, lnx_b, w_out, cos_t, sin_t):
    batch, seq, _ = x.shape
    a_w, r_w = ATT_WIDTH, RWKV_WIDTH
    lo = 4 * a_w + 3 * r_w
    n_lora = DECAY_LORA + AAA_LORA + GATE_LORA
    w_main = jnp.concatenate([w_in[:, :lo], w_in[:, lo + n_lora:]], axis=1).astype(BF16)
    w_lora = jnp.pad(w_in[:, lo:lo + n_lora], ((0, 0), (0, LORA_PAD - n_lora))).astype(BF16)
    mu_r = shift_mu[None, 0:r_w]
    mu_k = shift_mu[None, r_w:2 * r_w]
    mu_v = shift_mu[None, 2 * r_w:3 * r_w]
    mu_l = jnp.pad(shift_mu[None, 3 * r_w:], ((0, 0), (0, LORA_PAD - n_lora)))
    w2p = jnp.pad(w2, ((0, LANES - DECAY_LORA), (0, 0))).reshape(LANES, N_PAIRS, LANES)
    a2p = jnp.pad(a2, ((DECAY_LORA, LANES - DECAY_LORA - AAA_LORA), (0, 0))
                  ).reshape(LANES, N_PAIRS, LANES)
    wa2 = jnp.concatenate([w2p, a2p], axis=2).reshape(LANES, 2 * r_w)
    g2p = jnp.pad(g2, ((0, GATE_PAD - GATE_LORA), (0, 0)))

    x2 = x.reshape(batch * seq, D_MODEL)
    proj, lora = _in_proj(x2, norm_g[None, :], w_main, w_lora, cos_t, sin_t, seq)
    proj3 = proj.reshape(batch, seq, PROJ_WIDTH)
    lora3 = lora.reshape(batch, seq, LORA_PAD)
    att = _attention(proj3)
    rwk = _rwkv(proj3, lora3, mu_r, mu_k, mu_v, mu_l, w0[None, :], a0[None, :],
                k_k[None, :], k_a[None, :], r_k.reshape(1, r_w), lnx_g[None, :],
                lnx_b[None, :], wa2, g2p)
    w_out_b = w_out.astype(BF16)
    return att.reshape(batch * seq, a_w), rwk.reshape(batch * seq, r_w), w_out_b, x2


def kernel(x, norm_g, w_in, shift_mu, w0, w2, a0, a2, g2, k_k, k_a, r_k,
           lnx_g, lnx_b, w_out, final_g):
    batch, seq, _ = x.shape
    depth = norm_g.shape[0]
    assert seq % SUPER == 0 and seq % IN_TM == 0 and (batch * seq) % OUT_TM == 0
    cos_t, sin_t = _rope_tables(seq)
    for l in range(depth):
        att2, rwk2, w_out_b, x2 = _layer(
            x, norm_g[l], w_in[l], shift_mu[l], w0[l], w2[l], a0[l], a2[l], g2[l],
            k_k[l], k_a[l], r_k[l], lnx_g[l], lnx_b[l], w_out[l], cos_t, sin_t)
        y2 = _out_proj(att2, rwk2, w_out_b, x2, final_g[None, :],
                       normalize=(l == depth - 1))
        x = y2.reshape(batch, seq, D_MODEL)
    return x
```

```python
import functools

import jax
import jax.numpy as jnp
from jax import lax
from jax.experimental import pallas as pl
from jax.experimental.pallas import tpu as pltpu

F32 = jnp.float32
BF16 = jnp.bfloat16

D_MODEL = 2048
HEAD_DIM = 64
ATT_WIDTH = 1024
RWKV_WIDTH = 1024
DIL_PATTERNS = ((128, 1), (512, 4), (2048, 16))
ROPE_THETA = 10000.0
DECAY_LORA = 64
AAA_LORA = 64
GATE_LORA = 160
NORM_EPS = 1e-5
LNX_EPS = 64e-5
DECAY_SCALE = 0.6065306597126334
KK_NORM_FLOOR = 1e-12

LANES = 128
N_PAIRS = ATT_WIDTH // LANES
N_BACK = 128
Q_BLOCK = 128
SUPER = 2048
CHUNK = 64
RWKV_BLOCK = 512
N_CHUNKS = RWKV_BLOCK // CHUNK
LORA_PAD = 512
LORA_USED = 384
GATE_PAD = 256

COL_Q, COL_K, COL_V, COL_ZA = 0, 1024, 2048, 3072
COL_R, COL_RK, COL_RV, COL_ZR = 4096, 5120, 6144, 7168
PROJ_WIDTH = 8192

IN_TM, IN_TN = 1024, 1024
IN_SUB = 512
N_ROPE_STEPS = (COL_V - COL_Q) // IN_TN
N_Q_STEPS = (COL_K - COL_Q) // IN_TN
OUT_TM = 512
VMEM_LIMIT = 56 * 1024 * 1024


def _dot(a, b):
    return jnp.dot(a, b, preferred_element_type=F32)


def _dot_nt(a, b):
    return lax.dot_general(a, b, (((1,), (1,)), ((), ())), preferred_element_type=F32)


def _sigmoid(x):
    return 0.5 * jnp.tanh(0.5 * x) + 0.5


def _bf16_pieces(x, terms):
    pieces = []
    for _ in range(terms):
        piece = x.astype(BF16)
        pieces.append(piece)
        x = x - piece.astype(F32)
    return pieces


def _head_sum(x, ones2):
    return _dot(jnp.concatenate(_bf16_pieces(x, 2), axis=1), ones2)


def _in_proj_kernel(x_ref, g_ref, w_ref, wl_ref, cos_ref, sin_ref, o_ref, lora_ref, h_ref):
    j = pl.program_id(1)
    subs = [slice(s * IN_SUB, (s + 1) * IN_SUB) for s in range(IN_TN // IN_SUB)]

    @pl.when(j == 0)
    def _():
        x = x_ref[...]
        ms = jnp.mean(x * x, axis=-1, keepdims=True)
        h_ref[...] = (x * lax.rsqrt(ms + NORM_EPS) * g_ref[...]).astype(BF16)
        lora_ref[...] = _dot(h_ref[...], wl_ref[...])

    @pl.when(j < N_ROPE_STEPS)
    def _():
        reps = IN_SUB // LANES
        cos = jnp.concatenate([cos_ref[...]] * reps, axis=1)
        sin = jnp.concatenate([sin_ref[...]] * reps, axis=1)
        lane = lax.broadcasted_iota(jnp.int32, (IN_TM, IN_SUB), 1)
        first_half = (lane % HEAD_DIM) < (HEAD_DIM // 2)
        scale = jnp.where(j < N_Q_STEPS, HEAD_DIM ** -0.5, 1.0).astype(F32)
        for sub in subs:
            acc = _dot(h_ref[...], w_ref[:, sub])
            partner = jnp.where(first_half,
                                pltpu.roll(acc, IN_SUB - HEAD_DIM // 2, 1),
                                pltpu.roll(acc, HEAD_DIM // 2, 1))
            o_ref[:, sub] = (acc * cos + partner * sin) * scale

    @pl.when(j >= N_ROPE_STEPS)
    def _():
        for sub in subs:
            o_ref[:, sub] = _dot(h_ref[...], w_ref[:, sub])


def _in_proj(x2, g, w_main, w_lora, cos_t, sin_t, seq):
    tokens = x2.shape[0]
    pos_blocks = seq // IN_TM
    return pl.pallas_call(
        _in_proj_kernel,
        grid=(tokens // IN_TM, PROJ_WIDTH // IN_TN),
        in_specs=[
            pl.BlockSpec((IN_TM, D_MODEL), lambda i, j: (i, 0)),
            pl.BlockSpec((1, D_MODEL), lambda i, j: (0, 0)),
            pl.BlockSpec((D_MODEL, IN_TN), lambda i, j: (0, j)),
            pl.BlockSpec((D_MODEL, LORA_PAD), lambda i, j: (0, 0)),
            pl.BlockSpec((IN_TM, LANES), lambda i, j: (i % pos_blocks, 0)),
            pl.BlockSpec((IN_TM, LANES), lambda i, j: (i % pos_blocks, 0)),
        ],
        out_specs=[
            pl.BlockSpec((IN_TM, IN_TN), lambda i, j: (i, j)),
            pl.BlockSpec((IN_TM, LORA_PAD), lambda i, j: (i, 0)),
        ],
        out_shape=[
            jax.ShapeDtypeStruct((tokens, PROJ_WIDTH), F32),
            jax.ShapeDtypeStruct((tokens, LORA_PAD), F32),
        ],
        scratch_shapes=[pltpu.VMEM((IN_TM, D_MODEL), BF16)],
        compiler_params=pltpu.CompilerParams(
            dimension_semantics=("arbitrary", "arbitrary"),
            vmem_limit_bytes=VMEM_LIMIT),
        name="in_proj",
    )(x2, g, w_main, w_lora, cos_t, sin_t)


def _attention_kernel(q_ref, kc_ref, kp_ref, vc_ref, vp_ref, z_ref, o_ref,
                      kbuf, vbuf, acc_s, m_s, den_s):
    i = pl.program_id(2)
    kbuf[0:SUPER, :] = kp_ref[0]
    kbuf[SUPER:2 * SUPER, :] = kc_ref[0]
    vbuf[0:SUPER, :] = vp_ref[0]
    vbuf[SUPER:2 * SUPER, :] = vc_ref[0]

    qi = lax.broadcasted_iota(jnp.int32, (Q_BLOCK, 2 * Q_BLOCK), 0)
    ki = lax.broadcasted_iota(jnp.int32, (Q_BLOCK, 2 * Q_BLOCK), 1)
    rel = Q_BLOCK + qi - ki
    band = (rel >= 0) & (rel <= N_BACK)
    cur = ki >= Q_BLOCK
    lane = lax.broadcasted_iota(jnp.int32, (Q_BLOCK, LANES), 1)
    head0 = lane < HEAD_DIM

    def block(pat, dil, q_start):
        rows_q = pl.ds(q_start, Q_BLOCK, stride=dil) if dil > 1 else pl.ds(q_start, Q_BLOCK)
        k_start = SUPER + q_start - Q_BLOCK * dil
        rows_k = (pl.ds(k_start, 2 * Q_BLOCK, stride=dil) if dil > 1
                  else pl.ds(k_start, 2 * Q_BLOCK))
        qs = q_ref[0, rows_q, :]
        kw = kbuf[rows_k, :].astype(BF16)
        vw = vbuf[rows_k, :].astype(BF16)
        prev_ok = (i * SUPER + q_start) >= Q_BLOCK * dil
        valid = band & (cur | prev_ok)
        accs, ms, dens = [], [], []
        for hm in (head0, ~head0):
            qh = jnp.where(hm, qs, 0.0).astype(BF16)
            s = _dot_nt(qh, kw)
            s = jnp.where(valid, s, -jnp.inf)
            m = jnp.max(s, axis=-1, keepdims=True)
            p = jnp.exp(s - m)
            dens.append(jnp.sum(p, axis=-1, keepdims=True))
            ms.append(m)
            accs.append(_dot(p.astype(BF16), vw))
        acc_s[pat, rows_q, :] = jnp.where(head0, accs[0], accs[1])
        m_s[pat, rows_q, :] = jnp.where(head0, ms[0], ms[1])
        den_s[pat, rows_q, :] = jnp.where(head0, dens[0], dens[1])

    for pat, (window, dil) in enumerate(DIL_PATTERNS):
        span = Q_BLOCK * dil
        for s_idx in range(SUPER // span):
            for r in range(dil):
                block(pat, dil, s_idx * span + r)

    m0, m1, m2 = m_s[0], m_s[1], m_s[2]
    mx = jnp.maximum(jnp.maximum(m0, m1), m2)
    e0, e1, e2 = jnp.exp(m0 - mx), jnp.exp(m1 - mx), jnp.exp(m2 - mx)
    num = e0 * acc_s[0] + e1 * acc_s[1] + e2 * acc_s[2]
    den = e0 * den_s[0] + e1 * den_s[1] + e2 * den_s[2]
    z = z_ref[0]
    o_ref[0] = (num / den * (z * _sigmoid(z))).astype(o_ref.dtype)


def _attention(proj3):
    batch, seq, _ = proj3.shape
    blk = (1, SUPER, LANES)

    def col(base):
        return lambda b, p, i: (b, i, base // LANES + p)

    def col_prev(base):
        return lambda b, p, i: (b, jnp.maximum(i - 1, 0), base // LANES + p)

    return pl.pallas_call(
        _attention_kernel,
        grid=(batch, N_PAIRS, seq // SUPER),
        in_specs=[
            pl.BlockSpec(blk, col(COL_Q)),
            pl.BlockSpec(blk, col(COL_K)),
            pl.BlockSpec(blk, col_prev(COL_K)),
            pl.BlockSpec(blk, col(COL_V)),
            pl.BlockSpec(blk, col_prev(COL_V)),
            pl.BlockSpec(blk, col(COL_ZA)),
        ],
        out_specs=pl.BlockSpec(blk, lambda b, p, i: (b, i, p)),
        out_shape=jax.ShapeDtypeStruct((batch, seq, ATT_WIDTH), BF16),
        scratch_shapes=[
            pltpu.VMEM((2 * SUPER, LANES), F32),
            pltpu.VMEM((2 * SUPER, LANES), F32),
            pltpu.VMEM((len(DIL_PATTERNS), SUPER, LANES), F32),
            pltpu.VMEM((len(DIL_PATTERNS), SUPER, LANES), F32),
            pltpu.VMEM((len(DIL_PATTERNS), SUPER, LANES), F32),
        ],
        compiler_params=pltpu.CompilerParams(
            dimension_semantics=("arbitrary", "arbitrary", "arbitrary"),
            vmem_limit_bytes=VMEM_LIMIT),
        name="dilated_attention",
    )(proj3, proj3, proj3, proj3, proj3, proj3)


def _rwkv_kernel(r_ref, k_ref, v_ref, z_ref, lora_ref,
                 mur_ref, muk_ref, muv_ref, mul_ref,
                 w0_ref, a0_ref, kk_ref, ka_ref, wa2_ref, g2_ref,
                 rk_ref, lng_ref, lnb_ref,
                 o_ref,
                 carry_main, carry_lora, state,
                 pm_a, pm_r, pm_kb, pm_bk, pm_d, pm_v,
                 pc_r, pc_k, pc_v, pc_gz,
                 mc_rw, mc_y0, mc_m, mc_g, mc_d,
                 *, blocks_per_stream):
    t = pl.program_id(0)
    n_steps = pl.num_programs(0)
    first_prep = ((2 * jnp.minimum(t, n_steps - 2)) % blocks_per_stream) == 0
    first_chain = ((2 * jnp.maximum(t - 1, 0)) % blocks_per_stream) == 0

    @pl.when(t == 0)
    def _():
        for ref in (carry_main, carry_lora, state, pm_a, pm_r, pm_kb, pm_bk, pm_d, pm_v,
                    pc_r, pc_k, pc_v, pc_gz, mc_rw, mc_y0, mc_m, mc_g, mc_d):
            ref[...] = jnp.zeros_like(ref)

    rr = lax.broadcasted_iota(jnp.int32, (LANES, LANES), 0)
    cc = lax.broadcasted_iota(jnp.int32, (LANES, LANES), 1)
    same_head = (rr // HEAD_DIM) == (cc // HEAD_DIM)
    strict = same_head & ((cc % CHUNK) < (rr % CHUNK))
    incl = same_head & ((cc % CHUNK) <= (rr % CHUNK))
    eye = (rr == cc).astype(F32)
    head_ones2 = ((lax.broadcasted_iota(jnp.int32, (2 * LANES, LANES), 0) % LANES) // HEAD_DIM
                  == lax.broadcasted_iota(jnp.int32, (2 * LANES, LANES), 1) // HEAD_DIM
                  ).astype(BF16)
    tril = (lax.broadcasted_iota(jnp.int32, (CHUNK, CHUNK), 1)
            <= lax.broadcasted_iota(jnp.int32, (CHUNK, CHUNK), 0)).astype(BF16)
    head0 = lax.broadcasted_iota(jnp.int32, (CHUNK, LANES), 1) < HEAD_DIM
    row0_m = lax.broadcasted_iota(jnp.int32, (CHUNK, LANES), 0) == 0
    row0_l = lax.broadcasted_iota(jnp.int32, (CHUNK, LORA_USED), 0) == 0
    decay_lane = lax.broadcasted_iota(jnp.int32, (CHUNK, LANES), 1) < DECAY_LORA
    zero_blk = jnp.zeros((LANES, LANES), BF16)
    last = slice(CHUNK - 1, CHUNK)

    def bd(x):
        return jnp.concatenate([jnp.where(head0, x, 0.0), jnp.where(head0, 0.0, x)], axis=0)

    rows = [slice(c * CHUNK, (c + 1) * CHUNK) for c in range(N_CHUNKS)]
    cur = {"half": 0, "slot_p": 0, "slot_m": 1}

    def block_rows(c):
        start = cur["half"] * RWKV_BLOCK + c * CHUNK
        return slice(start, start + CHUNK)

    def carried(row):
        return jnp.where(first_prep, 0.0, row)

    prev_rows = {"r": carried(carry_main[0:1, :]), "k": carried(carry_main[1:2, :]),
                 "v": carried(carry_main[2:3, :]), "l": carried(carry_lora[:, 0:LORA_USED])}
    pv = [dict() for _ in range(N_CHUNKS)]

    def shifted(u, prev_row, row0, mu):
        prev = jnp.where(row0, prev_row, pltpu.roll(u, 1, 0))
        return u + (prev - u) * mu

    def prep_shift_main(c):
        win = block_rows(c)
        r_raw, k_raw, v_raw = r_ref[0, win, :], k_ref[0, win, :], v_ref[0, win, :]
        pv[c]["r"] = shifted(r_raw, prev_rows["r"], row0_m, mur_ref[...])
        pv[c]["k"] = shifted(k_raw, prev_rows["k"], row0_m, muk_ref[...])
        pv[c]["v"] = shifted(v_raw, prev_rows["v"], row0_m, muv_ref[...])
        prev_rows.update(r=r_raw[last, :], k=k_raw[last, :], v=v_raw[last, :])

    def prep_shift_lora(c):
        l_raw = lora_ref[0, block_rows(c), 0:LORA_USED]
        lora = shifted(l_raw, prev_rows["l"], row0_l, mul_ref[:, 0:LORA_USED])
        prev_rows["l"] = l_raw[last, :]
        wa = lora[:, 0:LANES]
        pv[c]["wa_act"] = jnp.where(decay_lane, jnp.tanh(wa), wa).astype(BF16)
        pv[c]["gate_act"] = _sigmoid(lora[:, LANES:LANES + GATE_PAD]).astype(BF16)

    def prep_lora_matmuls(c):
        d = pv[c]
        wa_out = _dot(d["wa_act"], wa2_ref[...].astype(BF16))
        d["lw"] = -(DECAY_SCALE * _sigmoid(w0_ref[...] + wa_out[:, 0:LANES]))
        d["a"] = _sigmoid(a0_ref[...] + wa_out[:, LANES:])
        g = _dot(d["gate_act"], g2_ref[...].astype(BF16))
        z = z_ref[0, block_rows(c), :]
        d["gz"] = g * (z * _sigmoid(z))

    def prep_key_norm(c):
        d = pv[c]
        kk = d["k"] * kk_ref[...]
        kk = kk * lax.rsqrt(jnp.maximum(_head_sum(kk * kk, head_ones2), KK_NORM_FLOOR ** 2))
        d["kk"] = kk
        d["k2"] = d["k"] * (1.0 + (d["a"] - 1.0) * ka_ref[...])
        d["b"] = kk * d["a"]
        parts = _dot(tril, jnp.concatenate(_bf16_pieces(d["lw"], 3), axis=1))
        d["cs"] = parts[:, 0:LANES] + parts[:, LANES:2 * LANES] + parts[:, 2 * LANES:]

    def prep_exp(c):
        d = pv[c]
        d["e_pos"] = jnp.exp(d["cs"])
        d["e_neg"] = jnp.exp(-d["cs"])
        d["e_excl"] = jnp.exp(d["cs"] - d["lw"])
        d["decay_end"] = d["e_pos"][last, :]

    def prep_store_ar(c):
        d = pv[c]
        idx = cur["slot_p"] * N_CHUNKS + c
        pm_a[idx] = bd(-d["kk"] * d["e_excl"]).astype(BF16)
        pm_r[idx] = bd(d["r"] * d["e_pos"])

    def prep_store_kbv(c):
        d = pv[c]
        slot_p = cur["slot_p"]
        idx = slot_p * N_CHUNKS + c
        pm_kb[idx] = jnp.concatenate([d["k2"] * d["e_neg"], d["b"] * d["e_neg"]],
                                     axis=0).astype(BF16)
        pm_v[idx] = bd(d["v"]).astype(BF16)

    def prep_store_pc(c):
        d = pv[c]
        slot_p = cur["slot_p"]
        pc_r[slot_p, rows[c], :] = d["r"]
        pc_k[slot_p, rows[c], :] = d["k2"]
        pc_v[slot_p, rows[c], :] = d["v"]
        pc_gz[slot_p, rows[c], :] = d["gz"]

    def prep_store_bk(c):
        d = pv[c]
        idx = cur["slot_p"] * N_CHUNKS + c
        e_end = d["decay_end"] * d["e_neg"]
        bkd_t = jnp.concatenate([bd(d["b"] * e_end), bd(d["k2"] * e_end),
                                 jnp.broadcast_to(d["decay_end"], (LANES, LANES))], axis=0).T
        pm_bk[idx] = bkd_t[:, 0:2 * LANES].astype(BF16)
        pm_d[idx] = bkd_t[:, 2 * LANES:]

    prep_groups = [prep_shift_main, prep_shift_lora, prep_lora_matmuls, prep_key_norm,
                   None, prep_exp, prep_store_ar, None, prep_store_kbv, prep_store_bk,
                   prep_store_pc]

    mm = [dict() for _ in range(N_CHUNKS)]

    def mm_scores(c):
        idx = cur["slot_m"] * N_CHUNKS + c
        kb = pm_kb[idx]
        k_t, b_t = kb[0:CHUNK, :], kb[CHUNK:, :]
        lhs = jnp.concatenate([pm_a[idx], pm_r[idx].astype(BF16)], axis=0)
        x = _dot_nt(lhs, jnp.concatenate([k_t, k_t, b_t, b_t], axis=0))
        d = mm[c]
        d["a_ak"] = jnp.where(strict, x[0:LANES, 0:LANES], 0.0).astype(BF16)
        a_ab = jnp.where(strict, x[0:LANES, LANES:], 0.0)
        d["a_r"] = jnp.concatenate([jnp.where(incl, x[LANES:, 0:LANES], 0.0),
                                    jnp.where(incl, x[LANES:, LANES:], 0.0)],
                                   axis=1).astype(BF16)
        d["q"] = a_ab.astype(BF16)
        d["p"] = eye + a_ab

    def mm_square(c):
        d = mm[c]
        d["q"] = _dot(d["q"], d["q"]).astype(BF16)

    def mm_double(c):
        d = mm[c]
        qp = _dot(d["q"], jnp.concatenate([d["q"], d["p"].astype(BF16)], axis=1))
        d["q"] = qp[:, 0:LANES].astype(BF16)
        d["p"] = d["p"] + qp[:, LANES:]

    def mm_inverse(c):
        d = mm[c]
        d["t"] = (d["p"] + _dot(d["q"], d["p"].astype(BF16))).astype(BF16)

    def mm_akv(c):
        d = mm[c]
        d["ak_v"] = _dot(d["a_ak"], pm_v[cur["slot_m"] * N_CHUNKS + c]).astype(BF16)

    def mm_wu(c):
        d = mm[c]
        rhs = jnp.concatenate([pm_a[cur["slot_m"] * N_CHUNKS + c], d["ak_v"]], axis=1)
        d["wu"] = _dot(d["t"], rhs).astype(BF16)

    def mm_yw(c):
        d = mm[c]
        idx = cur["slot_m"] * N_CHUNKS + c
        wu = d["wu"]
        rhs = jnp.concatenate(
            [jnp.concatenate([pm_v[idx], zero_blk], axis=1),
             jnp.concatenate([wu[:, LANES:], wu[:, 0:LANES]], axis=1)], axis=0)
        yw = _dot(d["a_r"], rhs)
        mc_y0[idx] = yw[:, 0:LANES]
        mc_rw[idx] = (pm_r[idx] + yw[:, LANES:]).astype(BF16)

    def mm_mg(c):
        idx = cur["slot_m"] * N_CHUNKS + c
        rhs = jnp.concatenate([mm[c]["wu"],
                               jnp.concatenate([zero_blk, pm_v[idx]], axis=1)], axis=0)
        mg = _dot(pm_bk[idx], rhs)
        mc_m[idx] = mg[:, 0:LANES].astype(BF16)
        mc_g[idx] = mg[:, LANES:]
        mc_d[idx] = pm_d[idx]

    matmul_stages = [mm_scores, mm_square, mm_double, mm_double, mm_double, mm_double,
                     mm_inverse, mm_akv, mm_wu, mm_yw, mm_mg]

    chain = {"hs": jnp.where(first_chain, 0.0, state[...])}

    cv = [dict() for _ in range(N_CHUNKS)]

    def chain_core(c):
        idx = cur["slot_p"] * N_CHUNKS + c
        hs = chain["hs"]
        hs_b = hs.astype(BF16)
        y = _dot(mc_rw[idx], hs_b) + mc_y0[idx]
        chain["hs"] = hs * mc_d[idx] + _dot(mc_m[idx], hs_b) + mc_g[idx]
        cv[c]["y"] = y[0:CHUNK, :] + y[CHUNK:, :]

    def chain_mean(c):
        slot_p, rws, d = cur["slot_p"], rows[c], cv[c]
        d["yc"] = d["y"] - _head_sum(d["y"], head_ones2) * (1.0 / HEAD_DIM)
        d["bonus"] = _head_sum(pc_r[slot_p, rws, :] * pc_k[slot_p, rws, :] * rk_ref[...],
                               head_ones2) * pc_v[slot_p, rws, :]

    def chain_finish(c):
        slot_p, rws, d = cur["slot_p"], rows[c], cv[c]
        var = _head_sum(d["yc"] * d["yc"], head_ones2) * (1.0 / HEAD_DIM)
        yn = d["yc"] * lax.rsqrt(var + LNX_EPS) * lng_ref[...] + lnb_ref[...]
        o_ref[0, block_rows(c), :] = ((yn + d["bonus"]) * pc_gz[slot_p, rws, :]
                                      ).astype(o_ref.dtype)

    assert len(prep_groups) == len(matmul_stages) >= N_CHUNKS + 2
    for half in (0, 1):
        cur.update(half=half, slot_p=half, slot_m=1 - half)
        for rnd, (mm_stage, prep_group) in enumerate(zip(matmul_stages, prep_groups)):
            if rnd < N_CHUNKS:
                chain_core(rnd)
            if 1 <= rnd <= N_CHUNKS:
                chain_mean(rnd - 1)
            if 2 <= rnd <= N_CHUNKS + 1:
                chain_finish(rnd - 2)
            for c in range(N_CHUNKS):
                mm_stage(c)
                if prep_group is not None:
                    prep_group(c)
    state[...] = chain["hs"]
    carry_main[0:1, :] = prev_rows["r"]
    carry_main[1:2, :] = prev_rows["k"]
    carry_main[2:3, :] = prev_rows["v"]
    carry_lora[:, 0:LORA_USED] = prev_rows["l"]


def _rwkv(proj3, lora3, mu_r, mu_k, mu_v, mu_l, w0, a0, k_k, k_a, r_k, lnx_g, lnx_b,
          wa2, g2p):
    batch, seq, _ = proj3.shape
    bps = seq // RWKV_BLOCK
    assert bps % 2 == 0
    wps = bps // 2
    n_total = batch * N_PAIRS * wps
    blk = (1, 2 * RWKV_BLOCK, LANES)

    def split(n):
        return n // (N_PAIRS * wps), n % wps, (n // wps) % N_PAIRS

    def prep_n(s):
        return jnp.minimum(s, n_total - 1)

    def chain_n(s):
        return jnp.maximum(s - 1, 0)

    def col(base):
        def index(s):
            b, i, p = split(prep_n(s))
            return b, i, base // LANES + p
        return index

    def lora_index(s):
        b, i, _ = split(prep_n(s))
        return b, i, 0

    def out_index(s):
        return split(chain_n(s))

    vec_prep = pl.BlockSpec((1, LANES), lambda s: (0, split(prep_n(s))[2]))
    vec_chain = pl.BlockSpec((1, LANES), lambda s: (0, split(chain_n(s))[2]))
    n_slots = 2 * N_CHUNKS
    return pl.pallas_call(
        functools.partial(_rwkv_kernel, blocks_per_stream=bps),
        grid=(n_total + 1,),
        in_specs=[
            pl.BlockSpec(blk, col(COL_R)),
            pl.BlockSpec(blk, col(COL_RK)),
            pl.BlockSpec(blk, col(COL_RV)),
            pl.BlockSpec(blk, col(COL_ZR)),
            pl.BlockSpec((1, 2 * RWKV_BLOCK, LORA_PAD), lora_index),
            vec_prep, vec_prep, vec_prep,
            pl.BlockSpec((1, LORA_PAD), lambda s: (0, 0)),
            vec_prep, vec_prep, vec_prep, vec_prep,
            pl.BlockSpec((LANES, 2 * LANES), lambda s: (0, split(prep_n(s))[2])),
            pl.BlockSpec((GATE_PAD, LANES), lambda s: (0, split(prep_n(s))[2])),
            vec_chain, vec_chain, vec_chain,
        ],
        out_specs=pl.BlockSpec(blk, out_index),
        out_shape=jax.ShapeDtypeStruct((batch, seq, RWKV_WIDTH), BF16),
        scratch_shapes=[
            pltpu.VMEM((8, LANES), F32),
            pltpu.VMEM((1, LORA_PAD), F32),
            pltpu.VMEM((LANES, LANES), F32),
            pltpu.VMEM((n_slots, LANES, LANES), BF16),
            pltpu.VMEM((n_slots, LANES, LANES), F32),
            pltpu.VMEM((n_slots, LANES, LANES), BF16),
            pltpu.VMEM((n_slots, LANES, 2 * LANES), BF16),
            pltpu.VMEM((n_slots, LANES, LANES), F32),
            pltpu.VMEM((n_slots, LANES, LANES), BF16),
            pltpu.VMEM((2, RWKV_BLOCK, LANES), F32),
            pltpu.VMEM((2, RWKV_BLOCK, LANES), F32),
            pltpu.VMEM((2, RWKV_BLOCK, LANES), F32),
            pltpu.VMEM((2, RWKV_BLOCK, LANES), F32),
            pltpu.VMEM((n_slots, LANES, LANES), BF16),
            pltpu.VMEM((n_slots, LANES, LANES), F32),
            pltpu.VMEM((n_slots, LANES, LANES), BF16),
            pltpu.VMEM((n_slots, LANES, LANES), F32),
            pltpu.VMEM((n_slots, LANES, LANES), F32),
        ],
        compiler_params=pltpu.CompilerParams(
            dimension_semantics=("arbitrary",),
            vmem_limit_bytes=VMEM_LIMIT),
        name="rwkv7_time_mix",
    )(proj3, proj3, proj3, proj3, lora3, mu_r, mu_k, mu_v, mu_l,
      w0, a0, k_k, k_a, wa2, g2p, r_k, lnx_g, lnx_b)


def _out_proj_kernel(att_ref, rwk_ref, wa_ref, wr_ref, x_ref, g_ref, o_ref, *, normalize):
    y = x_ref[...] + _dot(att_ref[...], wa_ref[...]) + _dot(rwk_ref[...], wr_ref[...])
    if normalize:
        ms = jnp.mean(y * y, axis=-1, keepdims=True)
        y = y * lax.rsqrt(ms + NORM_EPS) * g_ref[...]
    o_ref[...] = y


def _out_proj(att2, rwk2, w_out_b, x2, g, normalize):
    tokens = x2.shape[0]
    return pl.pallas_call(
        functools.partial(_out_proj_kernel, normalize=normalize),
        grid=(tokens // OUT_TM,),
        in_specs=[
            pl.BlockSpec((OUT_TM, ATT_WIDTH), lambda i: (i, 0)),
            pl.BlockSpec((OUT_TM, RWKV_WIDTH), lambda i: (i, 0)),
            pl.BlockSpec((ATT_WIDTH, D_MODEL), lambda i: (0, 0)),
            pl.BlockSpec((RWKV_WIDTH, D_MODEL), lambda i: (ATT_WIDTH // RWKV_WIDTH, 0)),
            pl.BlockSpec((OUT_TM, D_MODEL), lambda i: (i, 0)),
            pl.BlockSpec((1, D_MODEL), lambda i: (0, 0)),
        ],
        out_specs=pl.BlockSpec((OUT_TM, D_MODEL), lambda i: (i, 0)),
        out_shape=jax.ShapeDtypeStruct((tokens, D_MODEL), F32),
        compiler_params=pltpu.CompilerParams(
            dimension_semantics=("arbitrary",),
            vmem_limit_bytes=VMEM_LIMIT),
        name="out_proj",
    )(att2, rwk2, w_out_b, w_out_b, x2, g)


def _rope_tables(seq):
    inv_freq = ROPE_THETA ** (-jnp.arange(0, HEAD_DIM, 2, dtype=F32) / HEAD_DIM)
    ang = jnp.arange(seq, dtype=jnp.int32).astype(F32)[:, None] * inv_freq[None, :]
    cos, sin = jnp.cos(ang), jnp.sin(ang)
    reps = LANES // HEAD_DIM
    cos_t = jnp.tile(jnp.concatenate([cos, cos], axis=1), (1, reps))
    sin_t = jnp.tile(jnp.concatenate([-sin, sin], axis=1), (1, reps))
    return cos_t, sin_t


def _layer(x, norm_g, w_in, shift_mu, w0, w2, a0, a2, g2, k_k, k_a, r_k,
           lnx_g, lnx_b, w_out, cos_t, sin_t):
    batch, seq, _ = x.shape
    a_w, r_w = ATT_WIDTH, RWKV_WIDTH
    lo = 4 * a_w + 3 * r_w
    n_lora = DECAY_LORA + AAA_LORA + GATE_LORA
    w_main = jnp.concatenate([w_in[:, :lo], w_in[:, lo + n_lora:]], axis=1).astype(BF16)
    w_lora = jnp.pad(w_in[:, lo:lo + n_lora], ((0, 0), (0, LORA_PAD - n_lora))).astype(BF16)
    mu_r = shift_mu[None, 0:r_w]
    mu_k = shift_mu[None, r_w:2 * r_w]
    mu_v = shift_mu[None, 2 * r_w:3 * r_w]
    mu_l = jnp.pad(shift_mu[None, 3 * r_w:], ((0, 0), (0, LORA_PAD - n_lora)))
    w2p = jnp.pad(w2, ((0, LANES - DECAY_LORA), (0, 0))).reshape(LANES, N_PAIRS, LANES)
    a2p = jnp.pad(a2, ((DECAY_LORA, LANES - DECAY_LORA - AAA_LORA), (0, 0))
                  ).reshape(LANES, N_PAIRS, LANES)
    wa2 = jnp.concatenate([w2p, a2p], axis=2).reshape(LANES, 2 * r_w)
    g2p = jnp.pad(g2, ((0, GATE_PAD - GATE_LORA), (0, 0)))

    x2 = x.reshape(batch * seq, D_MODEL)
    proj, lora = _in_proj(x2, norm_g[None, :], w_main, w_lora, cos_t, sin_t, seq)
    proj3 = proj.reshape(batch, seq, PROJ_WIDTH)
    lora3 = lora.reshape(batch, seq, LORA_PAD)
    att = _attention(proj3)
    rwk = _rwkv(proj3, lora3, mu_r, mu_k, mu_v, mu_l, w0[None, :], a0[None, :],
                k_k[None, :], k_a[None, :], r_k.reshape(1, r_w), lnx_g[None, :],
                lnx_b[None, :], wa2, g2p)
    w_out_b = w_out.astype(BF16)
    return att.reshape(batch * seq, a_w), rwk.reshape(batch * seq, r_w), w_out_b, x2


def kernel(x, norm_g, w_in, shift_mu, w0, w2, a0, a2, g2, k_k, k_a, r_k,
           lnx_g, lnx_b, w_out, final_g):
    batch, seq, _ = x.shape
    depth = norm_g.shape[0]
    assert seq % SUPER == 0 and seq % IN_TM == 0 and (batch * seq) % OUT_TM == 0
    cos_t, sin_t = _rope_tables(seq)
    for l in range(depth):
        att2, rwk2, w_out_b, x2 = _layer(
            x, norm_g[l], w_in[l], shift_mu[l], w0[l], w2[l], a0[l], a2[l], g2[l],
            k_k[l], k_a[l], r_k[l], lnx_g[l], lnx_b[l], w_out[l], cos_t, sin_t)
        y2 = _out_proj(att2, rwk2, w_out_b, x2, final_g[None, :],
                       normalize=(l == depth - 1))
        x = y2.reshape(batch, seq, D_MODEL)
    return x
```

```python
import functools

import jax
import jax.numpy as jnp
from jax import lax
from jax.experimental import pallas as pl
from jax.experimental.pallas import tpu as pltpu

F32 = jnp.float32
BF16 = jnp.bfloat16

D_MODEL = 2048
HEAD_DIM = 64
ATT_WIDTH = 1024
RWKV_WIDTH = 1024
DIL_PATTERNS = ((128, 1), (512, 4), (2048, 16))
ROPE_THETA = 10000.0
DECAY_LORA = 64
AAA_LORA = 64
GATE_LORA = 160
NORM_EPS = 1e-5
LNX_EPS = 64e-5
DECAY_SCALE = 0.6065306597126334
KK_NORM_FLOOR = 1e-12
LOG2_E = 1.4426950408889634

LANES = 128
N_PAIRS = ATT_WIDTH // LANES
N_BACK = 128
Q_BLOCK = 128
SUPER = 2048
CHUNK = 64
RWKV_BLOCK = 512
N_CHUNKS = RWKV_BLOCK // CHUNK
LORA_PAD = 512
LORA_USED = 384
GATE_PAD = 256

COL_Q, COL_K, COL_V, COL_ZA = 0, 1024, 2048, 3072
COL_R, COL_RK, COL_RV, COL_ZR = 4096, 5120, 6144, 7168
PROJ_WIDTH = 8192

IN_TM, IN_TN = 1024, 1024
IN_SUB = 512
N_ROPE_STEPS = (COL_V - COL_Q) // IN_TN
N_Q_STEPS = (COL_K - COL_Q) // IN_TN
N_ALIGNED_STEPS = COL_ZR // IN_TN
OUT_TM = 512
VMEM_LIMIT = 56 * 1024 * 1024


def _dot(a, b):
    return jnp.dot(a, b, preferred_element_type=F32)


def _dot_nt(a, b):
    return lax.dot_general(a, b, (((1,), (1,)), ((), ())), preferred_element_type=F32)


def _sigmoid(x):
    return 0.5 * jnp.tanh(0.5 * x) + 0.5


def _bf16_pieces(x, terms):
    pieces = []
    for _ in range(terms):
        piece = x.astype(BF16)
        pieces.append(piece)
        x = x - piece.astype(F32)
    return pieces


def _head_sum(x, ones2):
    return _dot(jnp.concatenate(_bf16_pieces(x, 2), axis=1), ones2)


def _in_proj_kernel(x_ref, g_ref, w_ref, wz_ref, wl_ref, cos_ref, sin_ref,
                    o_ref, lora_ref, h_ref):
    j = pl.program_id(1)
    subs = [slice(s * IN_SUB, (s + 1) * IN_SUB) for s in range(IN_TN // IN_SUB)]

    @pl.when(j == 0)
    def _():
        x = x_ref[...]
        ms = jnp.mean(x * x, axis=-1, keepdims=True)
        h_ref[...] = (x * lax.rsqrt(ms + NORM_EPS) * g_ref[...]).astype(BF16)
        lora_ref[...] = _dot(h_ref[...], wl_ref[...])

    @pl.when(j < N_ROPE_STEPS)
    def _():
        reps = IN_SUB // LANES
        cos = jnp.concatenate([cos_ref[...]] * reps, axis=1)
        sin = jnp.concatenate([sin_ref[...]] * reps, axis=1)
        lane = lax.broadcasted_iota(jnp.int32, (IN_TM, IN_SUB), 1)
        first_half = (lane % HEAD_DIM) < (HEAD_DIM // 2)
        scale = jnp.where(j < N_Q_STEPS, LOG2_E * HEAD_DIM ** -0.5, 1.0).astype(F32)
        for sub in subs:
            acc = _dot(h_ref[...], w_ref[:, sub])
            partner = jnp.where(first_half,
                                pltpu.roll(acc, IN_SUB - HEAD_DIM // 2, 1),
                                pltpu.roll(acc, HEAD_DIM // 2, 1))
            o_ref[:, sub] = (acc * cos + partner * sin) * scale

    @pl.when((j >= N_ROPE_STEPS) & (j < N_ALIGNED_STEPS))
    def _():
        for sub in subs:
            o_ref[:, sub] = _dot(h_ref[...], w_ref[:, sub])

    @pl.when(j >= N_ALIGNED_STEPS)
    def _():
        for sub in subs:
            o_ref[:, sub] = _dot(h_ref[...], wz_ref[:, sub])


def _in_proj(x2, g, w_all, w_gate, w_lora, cos_t, sin_t, seq):
    tokens = x2.shape[0]
    pos_blocks = seq // IN_TM
    return pl.pallas_call(
        _in_proj_kernel,
        grid=(tokens // IN_TM, PROJ_WIDTH // IN_TN),
        in_specs=[
            pl.BlockSpec((IN_TM, D_MODEL), lambda i, j: (i, 0)),
            pl.BlockSpec((1, D_MODEL), lambda i, j: (0, 0)),
            pl.BlockSpec((D_MODEL, IN_TN), lambda i, j: (0, jnp.minimum(j, N_ALIGNED_STEPS - 1))),
            pl.BlockSpec((D_MODEL, IN_TN), lambda i, j: (0, 0)),
            pl.BlockSpec((D_MODEL, LORA_PAD), lambda i, j: (0, 0)),
            pl.BlockSpec((IN_TM, LANES), lambda i, j: (i % pos_blocks, 0)),
            pl.BlockSpec((IN_TM, LANES), lambda i, j: (i % pos_blocks, 0)),
        ],
        out_specs=[
            pl.BlockSpec((IN_TM, IN_TN), lambda i, j: (i, j)),
            pl.BlockSpec((IN_TM, LORA_PAD), lambda i, j: (i, 0)),
        ],
        out_shape=[
            jax.ShapeDtypeStruct((tokens, PROJ_WIDTH), F32),
            jax.ShapeDtypeStruct((tokens, LORA_PAD), F32),
        ],
        scratch_shapes=[pltpu.VMEM((IN_TM, D_MODEL), BF16)],
        compiler_params=pltpu.CompilerParams(
            dimension_semantics=("arbitrary", "arbitrary"),
            vmem_limit_bytes=VMEM_LIMIT),
        name="in_proj",
    )(x2, g, w_all, w_gate, w_lora, cos_t, sin_t)


def _attention_kernel(q_ref, kc_ref, kp_ref, vc_ref, vp_ref, z_ref, o_ref,
                      kbuf, vbuf, acc_s, m_s, den_s):
    i = pl.program_id(2)
    kbuf[0:SUPER, :] = kp_ref[0]
    kbuf[SUPER:2 * SUPER, :] = kc_ref[0]
    vbuf[0:SUPER, :] = vp_ref[0]
    vbuf[SUPER:2 * SUPER, :] = vc_ref[0]

    qi = lax.broadcasted_iota(jnp.int32, (Q_BLOCK, 2 * Q_BLOCK), 0)
    ki = lax.broadcasted_iota(jnp.int32, (Q_BLOCK, 2 * Q_BLOCK), 1)
    rel = Q_BLOCK + qi - ki
    band = (rel >= 0) & (rel <= N_BACK)
    cur = ki >= Q_BLOCK
    lane = lax.broadcasted_iota(jnp.int32, (Q_BLOCK, LANES), 1)
    head0 = lane < HEAD_DIM

    def block(pat, dil, q_start):
        rows_q = pl.ds(q_start, Q_BLOCK, stride=dil) if dil > 1 else pl.ds(q_start, Q_BLOCK)
        k_start = SUPER + q_start - Q_BLOCK * dil
        rows_k = (pl.ds(k_start, 2 * Q_BLOCK, stride=dil) if dil > 1
                  else pl.ds(k_start, 2 * Q_BLOCK))
        qs = q_ref[0, rows_q, :]
        kw = kbuf[rows_k, :].astype(BF16)
        vw = vbuf[rows_k, :].astype(BF16)
        prev_ok = (i * SUPER + q_start) >= Q_BLOCK * dil
        valid = band & (cur | prev_ok)
        accs, ms, dens = [], [], []
        for hm in (head0, ~head0):
            qh = jnp.where(hm, qs, 0.0).astype(BF16)
            s = _dot_nt(qh, kw)
            s = jnp.where(valid, s, -jnp.inf)
            m = jnp.max(s, axis=-1, keepdims=True)
            p = jnp.exp2(s - m)
            dens.append(jnp.sum(p, axis=-1, keepdims=True))
            ms.append(m)
            accs.append(_dot(p.astype(BF16), vw))
        acc_s[pat, rows_q, :] = jnp.where(head0, accs[0], accs[1])
        m_s[pat, rows_q, :] = jnp.where(head0, ms[0], ms[1])
        den_s[pat, rows_q, :] = jnp.where(head0, dens[0], dens[1])

    for pat, (window, dil) in enumerate(DIL_PATTERNS):
        span = Q_BLOCK * dil
        for s_idx in range(SUPER // span):
            for r in range(dil):
                block(pat, dil, s_idx * span + r)

    m0, m1, m2 = m_s[0], m_s[1], m_s[2]
    mx = jnp.maximum(jnp.maximum(m0, m1), m2)
    e0, e1, e2 = jnp.exp2(m0 - mx), jnp.exp2(m1 - mx), jnp.exp2(m2 - mx)
    num = e0 * acc_s[0] + e1 * acc_s[1] + e2 * acc_s[2]
    den = e0 * den_s[0] + e1 * den_s[1] + e2 * den_s[2]
    z = z_ref[0]
    o_ref[0] = (num / den * (z * _sigmoid(z))).astype(o_ref.dtype)


def _attention(proj3):
    batch, seq, _ = proj3.shape
    blk = (1, SUPER, LANES)

    def col(base):
        return lambda b, p, i: (b, i, base // LANES + p)

    def col_prev(base):
        return lambda b, p, i: (b, jnp.maximum(i - 1, 0), base // LANES + p)

    return pl.pallas_call(
        _attention_kernel,
        grid=(batch, N_PAIRS, seq // SUPER),
        in_specs=[
            pl.BlockSpec(blk, col(COL_Q)),
            pl.BlockSpec(blk, col(COL_K)),
            pl.BlockSpec(blk, col_prev(COL_K)),
            pl.BlockSpec(blk, col(COL_V)),
            pl.BlockSpec(blk, col_prev(COL_V)),
            pl.BlockSpec(blk, col(COL_ZA)),
        ],
        out_specs=pl.BlockSpec(blk, lambda b, p, i: (b, i, p)),
        out_shape=jax.ShapeDtypeStruct((batch, seq, ATT_WIDTH), BF16),
        scratch_shapes=[
            pltpu.VMEM((2 * SUPER, LANES), F32),
            pltpu.VMEM((2 * SUPER, LANES), F32),
            pltpu.VMEM((len(DIL_PATTERNS), SUPER, LANES), F32),
            pltpu.VMEM((len(DIL_PATTERNS), SUPER, LANES), F32),
            pltpu.VMEM((len(DIL_PATTERNS), SUPER, LANES), F32),
        ],
        compiler_params=pltpu.CompilerParams(
            dimension_semantics=("arbitrary", "arbitrary", "arbitrary"),
            vmem_limit_bytes=VMEM_LIMIT),
        name="dilated_attention",
    )(proj3, proj3, proj3, proj3, proj3, proj3)


def _rwkv_kernel(r_ref, k_ref, v_ref, z_ref, lora_ref,
                 mur_ref, muk_ref, muv_ref, mul_ref,
                 w0_ref, a0_ref, kk_ref, ka_ref, wa2_ref, g2_ref,
                 rk_ref, lng_ref, lnb_ref,
                 o_ref,
                 carry_main, carry_lora, state,
                 pm_a, pm_r, pm_kb, pm_bk, pm_d, pm_v,
                 pc_r, pc_k, pc_v, pc_gz,
                 mc_rw, mc_y0, mc_m, mc_g, mc_d,
                 *, blocks_per_stream):
    t = pl.program_id(0)
    n_steps = pl.num_programs(0)
    first_prep = ((2 * jnp.minimum(t, n_steps - 2)) % blocks_per_stream) == 0
    first_chain = ((2 * jnp.maximum(t - 1, 0)) % blocks_per_stream) == 0

    @pl.when(t == 0)
    def _():
        for ref in (carry_main, carry_lora, state, pm_a, pm_r, pm_kb, pm_bk, pm_d, pm_v,
                    pc_r, pc_k, pc_v, pc_gz, mc_rw, mc_y0, mc_m, mc_g, mc_d):
            ref[...] = jnp.zeros_like(ref)

    rr = lax.broadcasted_iota(jnp.int32, (LANES, LANES), 0)
    cc = lax.broadcasted_iota(jnp.int32, (LANES, LANES), 1)
    same_head = (rr // HEAD_DIM) == (cc // HEAD_DIM)
    strict = same_head & ((cc % CHUNK) < (rr % CHUNK))
    incl = same_head & ((cc % CHUNK) <= (rr % CHUNK))
    eye = (rr == cc).astype(F32)
    head_ones2 = ((lax.broadcasted_iota(jnp.int32, (2 * LANES, LANES), 0) % LANES) // HEAD_DIM
                  == lax.broadcasted_iota(jnp.int32, (2 * LANES, LANES), 1) // HEAD_DIM
                  ).astype(BF16)
    tril = (lax.broadcasted_iota(jnp.int32, (CHUNK, CHUNK), 1)
            <= lax.broadcasted_iota(jnp.int32, (CHUNK, CHUNK), 0)).astype(BF16)
    head0 = lax.broadcasted_iota(jnp.int32, (CHUNK, LANES), 1) < HEAD_DIM
    row0_m = lax.broadcasted_iota(jnp.int32, (CHUNK, LANES), 0) == 0
    row0_l = lax.broadcasted_iota(jnp.int32, (CHUNK, LORA_USED), 0) == 0
    decay_lane = lax.broadcasted_iota(jnp.int32, (CHUNK, LANES), 1) < DECAY_LORA
    zero_blk = jnp.zeros((LANES, LANES), BF16)
    last = slice(CHUNK - 1, CHUNK)

    def bd(x):
        return jnp.concatenate([jnp.where(head0, x, 0.0), jnp.where(head0, 0.0, x)], axis=0)

    rows = [slice(c * CHUNK, (c + 1) * CHUNK) for c in range(N_CHUNKS)]
    cur = {"half": 0, "slot_p": 0, "slot_m": 1}

    def block_rows(c):
        start = cur["half"] * RWKV_BLOCK + c * CHUNK
        return slice(start, start + CHUNK)

    def carried(row):
        return jnp.where(first_prep, 0.0, row)

    prev_rows = {"r": carried(carry_main[0:1, :]), "k": carried(carry_main[1:2, :]),
                 "v": carried(carry_main[2:3, :]), "l": carried(carry_lora[:, 0:LORA_USED])}
    pv = [dict() for _ in range(N_CHUNKS)]

    def shifted(u, prev_row, row0, mu):
        prev = jnp.where(row0, prev_row, pltpu.roll(u, 1, 0))
        return u + (prev - u) * mu

    def prep_shift_main(c):
        win = block_rows(c)
        r_raw, k_raw, v_raw = r_ref[0, win, :], k_ref[0, win, :], v_ref[0, win, :]
        pv[c]["r"] = shifted(r_raw, prev_rows["r"], row0_m, mur_ref[...])
        pv[c]["k"] = shifted(k_raw, prev_rows["k"], row0_m, muk_ref[...])
        pv[c]["v"] = shifted(v_raw, prev_rows["v"], row0_m, muv_ref[...])
        prev_rows.update(r=r_raw[last, :], k=k_raw[last, :], v=v_raw[last, :])

    def prep_shift_lora(c):
        l_raw = lora_ref[0, block_rows(c), 0:LORA_USED]
        lora = shifted(l_raw, prev_rows["l"], row0_l, mul_ref[:, 0:LORA_USED])
        prev_rows["l"] = l_raw[last, :]
        wa = lora[:, 0:LANES]
        pv[c]["wa_act"] = jnp.where(decay_lane, jnp.tanh(wa), wa).astype(BF16)
        pv[c]["gate_act"] = _sigmoid(lora[:, LANES:LANES + GATE_PAD]).astype(BF16)

    def prep_lora_matmuls(c):
        d = pv[c]
        wa_out = _dot(d["wa_act"], wa2_ref[...].astype(BF16))
        d["lw"] = -(DECAY_SCALE * _sigmoid(w0_ref[...] + wa_out[:, 0:LANES]))
        d["a"] = _sigmoid(a0_ref[...] + wa_out[:, LANES:])
        g = _dot(d["gate_act"], g2_ref[...].astype(BF16))
        z = z_ref[0, block_rows(c), :]
        d["gz"] = g * (z * _sigmoid(z))

    def prep_key_norm(c):
        d = pv[c]
        kk = d["k"] * kk_ref[...]
        kk = kk * lax.rsqrt(jnp.maximum(_head_sum(kk * kk, head_ones2), KK_NORM_FLOOR ** 2))
        d["kk"] = kk
        d["k2"] = d["k"] * (1.0 + (d["a"] - 1.0) * ka_ref[...])
        d["b"] = kk * d["a"]
        parts = _dot(tril, jnp.concatenate(_bf16_pieces(d["lw"], 3), axis=1))
        d["cs"] = parts[:, 0:LANES] + parts[:, LANES:2 * LANES] + parts[:, 2 * LANES:]

    def prep_exp(c):
        d = pv[c]
        d["e_pos"] = jnp.exp(d["cs"])
        d["e_neg"] = jnp.exp(-d["cs"])
        d["e_excl"] = jnp.exp(d["cs"] - d["lw"])
        d["decay_end"] = d["e_pos"][last, :]

    def prep_store_ar(c):
        d = pv[c]
        idx = cur["slot_p"] * N_CHUNKS + c
        pm_a[idx] = bd(-d["kk"] * d["e_excl"]).astype(BF16)
        pm_r[idx] = bd(d["r"] * d["e_pos"])

    def prep_store_kbv(c):
        d = pv[c]
        slot_p = cur["slot_p"]
        idx = slot_p * N_CHUNKS + c
        pm_kb[idx] = jnp.concatenate([d["k2"] * d["e_neg"], d["b"] * d["e_neg"]],
                                     axis=0).astype(BF16)
        pm_v[idx] = bd(d["v"]).astype(BF16)

    def prep_store_pc(c):
        d = pv[c]
        slot_p = cur["slot_p"]
        pc_r[slot_p, rows[c], :] = d["r"]
        pc_k[slot_p, rows[c], :] = d["k2"]
        pc_v[slot_p, rows[c], :] = d["v"]
        pc_gz[slot_p, rows[c], :] = d["gz"]

    def prep_store_bk(c):
        d = pv[c]
        idx = cur["slot_p"] * N_CHUNKS + c
        e_end = d["decay_end"] * d["e_neg"]
        bkd_t = jnp.concatenate([bd(d["b"] * e_end), bd(d["k2"] * e_end),
                                 jnp.broadcast_to(d["decay_end"], (LANES, LANES))], axis=0).T
        pm_bk[idx] = bkd_t[:, 0:2 * LANES].astype(BF16)
        pm_d[idx] = bkd_t[:, 2 * LANES:]

    prep_groups = [(prep_shift_main, 144), (prep_shift_lora, 250), (prep_lora_matmuls, 110),
                   (prep_key_norm, 150), (prep_exp, 50), (prep_store_ar, 56),
                   (prep_store_kbv, 50), (prep_store_bk, 75)]

    mm = [dict() for _ in range(N_CHUNKS)]

    def mm_scores(c):
        idx = cur["slot_m"] * N_CHUNKS + c
        kb = pm_kb[idx]
        k_t, b_t = kb[0:CHUNK, :], kb[CHUNK:, :]
        lhs = jnp.concatenate([pm_a[idx], pm_r[idx].astype(BF16)], axis=0)
        x = _dot_nt(lhs, jnp.concatenate([k_t, k_t, b_t, b_t], axis=0))
        d = mm[c]
        d["a_ak"] = jnp.where(strict, x[0:LANES, 0:LANES], 0.0).astype(BF16)
        a_ab = jnp.where(strict, x[0:LANES, LANES:], 0.0)
        d["a_r"] = jnp.concatenate([jnp.where(incl, x[LANES:, 0:LANES], 0.0),
                                    jnp.where(incl, x[LANES:, LANES:], 0.0)],
                                   axis=1).astype(BF16)
        d["q"] = a_ab.astype(BF16)
        d["p"] = eye + a_ab

    def mm_square(c):
        d = mm[c]
        d["q"] = _dot(d["q"], d["q"]).astype(BF16)

    def mm_double(c):
        d = mm[c]
        qp = _dot(d["q"], jnp.concatenate([d["q"], d["p"].astype(BF16)], axis=1))
        d["q"] = qp[:, 0:LANES].astype(BF16)
        d["p"] = d["p"] + qp[:, LANES:]

    def mm_inverse(c):
        d = mm[c]
        d["t"] = (d["p"] + _dot(d["q"], d["p"].astype(BF16))).astype(BF16)

    def mm_akv(c):
        d = mm[c]
        d["ak_v"] = _dot(d["a_ak"], pm_v[cur["slot_m"] * N_CHUNKS + c]).astype(BF16)

    def mm_wu(c):
        d = mm[c]
        rhs = jnp.concatenate([pm_a[cur["slot_m"] * N_CHUNKS + c], d["ak_v"]], axis=1)
        d["wu"] = _dot(d["t"], rhs).astype(BF16)

    def mm_yw(c):
        d = mm[c]
        idx = cur["slot_m"] * N_CHUNKS + c
        wu = d["wu"]
        rhs = jnp.concatenate(
            [jnp.concatenate([pm_v[idx], zero_blk], axis=1),
             jnp.concatenate([wu[:, LANES:], wu[:, 0:LANES]], axis=1)], axis=0)
        yw = _dot(d["a_r"], rhs)
        mc_y0[idx] = yw[:, 0:LANES]
        mc_rw[idx] = (pm_r[idx] + yw[:, LANES:]).astype(BF16)

    def mm_mg(c):
        idx = cur["slot_m"] * N_CHUNKS + c
        rhs = jnp.concatenate([mm[c]["wu"],
                               jnp.concatenate([zero_blk, pm_v[idx]], axis=1)], axis=0)
        mg = _dot(pm_bk[idx], rhs)
        mc_m[idx] = mg[:, 0:LANES].astype(BF16)
        mc_g[idx] = mg[:, LANES:]
        mc_d[idx] = pm_d[idx]

    matmul_stages = [mm_scores, mm_square, mm_double, mm_double, mm_double, mm_double,
                     mm_inverse, mm_akv, mm_wu, mm_yw, mm_mg]

    chain = {"hs": jnp.where(first_chain, 0.0, state[...])}

    cv = [dict() for _ in range(N_CHUNKS)]

    def chain_core(c):
        idx = cur["slot_p"] * N_CHUNKS + c
        hs = chain["hs"]
        hs_b = hs.astype(BF16)
        y = _dot(mc_rw[idx], hs_b) + mc_y0[idx]
        chain["hs"] = hs * mc_d[idx] + _dot(mc_m[idx], hs_b) + mc_g[idx]
        cv[c]["y"] = y[0:CHUNK, :] + y[CHUNK:, :]

    def chain_mean(c):
        slot_p, rws, d = cur["slot_p"], rows[c], cv[c]
        d["yc"] = d["y"] - _head_sum(d["y"], head_ones2) * (1.0 / HEAD_DIM)
        d["bonus"] = _head_sum(pc_r[slot_p, rws, :] * pc_k[slot_p, rws, :] * rk_ref[...],
                               head_ones2) * pc_v[slot_p, rws, :]

    def chain_finish(c):
        slot_p, rws, d = cur["slot_p"], rows[c], cv[c]
        var = _head_sum(d["yc"] * d["yc"], head_ones2) * (1.0 / HEAD_DIM)
        yn = d["yc"] * lax.rsqrt(var + LNX_EPS) * lng_ref[...] + lnb_ref[...]
        o_ref[0, block_rows(c), :] = ((yn + d["bonus"]) * pc_gz[slot_p, rws, :]
                                      ).astype(o_ref.dtype)

    n_rounds = len(matmul_stages)
    assert n_rounds >= N_CHUNKS + 3
    n_slots = (n_rounds - 1) * N_CHUNKS
    total = float(N_CHUNKS * sum(w for _, w in prep_groups))
    prep_slots = [[] for _ in range(n_slots)]
    done = 0.0
    for fn, weight in prep_groups:
        for c in range(N_CHUNKS):
            prep_slots[min(int(done / total * n_slots), n_slots - 1)].append((fn, c))
            done += weight
    for half in (0, 1):
        cur.update(half=half, slot_p=half, slot_m=1 - half)
        for rnd, mm_stage in enumerate(matmul_stages):
            if rnd < N_CHUNKS:
                chain_core(rnd)
            if 1 <= rnd <= N_CHUNKS:
                chain_mean(rnd - 1)
            if 2 <= rnd <= N_CHUNKS + 1:
                chain_finish(rnd - 2)
            for c in range(N_CHUNKS):
                mm_stage(c)
                if rnd == n_rounds - 1:
                    prep_store_pc(c)
                else:
                    for fn, pc in prep_slots[rnd * N_CHUNKS + c]:
                        fn(pc)
    state[...] = chain["hs"]
    carry_main[0:1, :] = prev_rows["r"]
    carry_main[1:2, :] = prev_rows["k"]
    carry_main[2:3, :] = prev_rows["v"]
    carry_lora[:, 0:LORA_USED] = prev_rows["l"]


def _rwkv(proj3, lora3, mu_r, mu_k, mu_v, mu_l, w0, a0, k_k, k_a, r_k, lnx_g, lnx_b,
          wa2, g2p):
    batch, seq, _ = proj3.shape
    bps = seq // RWKV_BLOCK
    assert bps % 2 == 0
    wps = bps // 2
    n_total = batch * N_PAIRS * wps
    blk = (1, 2 * RWKV_BLOCK, LANES)

    def split(n):
        return n // (N_PAIRS * wps), n % wps, (n // wps) % N_PAIRS

    def prep_n(s):
        return jnp.minimum(s, n_total - 1)

    def chain_n(s):
        return jnp.maximum(s - 1, 0)

    def col(base):
        def index(s):
            b, i, p = split(prep_n(s))
            return b, i, base // LANES + p
        return index

    def lora_index(s):
        b, i, _ = split(prep_n(s))
        return b, i, 0

    def out_index(s):
        return split(chain_n(s))

    vec_prep = pl.BlockSpec((1, LANES), lambda s: (0, split(prep_n(s))[2]))
    vec_chain = pl.BlockSpec((1, LANES), lambda s: (0, split(chain_n(s))[2]))
    n_slots = 2 * N_CHUNKS
    return pl.pallas_call(
        functools.partial(_rwkv_kernel, blocks_per_stream=bps),
        grid=(n_total + 1,),
        in_specs=[
            pl.BlockSpec(blk, col(COL_R)),
            pl.BlockSpec(blk, col(COL_RK)),
            pl.BlockSpec(blk, col(COL_RV)),
            pl.BlockSpec(blk, col(COL_ZR)),
            pl.BlockSpec((1, 2 * RWKV_BLOCK, LORA_PAD), lora_index),
            vec_prep, vec_prep, vec_prep,
            pl.BlockSpec((1, LORA_PAD), lambda s: (0, 0)),
            vec_prep, vec_prep, vec_prep, vec_prep,
            pl.BlockSpec((LANES, 2 * LANES), lambda s: (0, split(prep_n(s))[2])),
            pl.BlockSpec((GATE_PAD, LANES), lambda s: (0, split(prep_n(s))[2])),
            vec_chain, vec_chain, vec_chain,
        ],
        out_specs=pl.BlockSpec(blk, out_index),
        out_shape=jax.ShapeDtypeStruct((batch, seq, RWKV_WIDTH), BF16),
        scratch_shapes=[
            pltpu.VMEM((8, LANES), F32),
            pltpu.VMEM((1, LORA_PAD), F32),
            pltpu.VMEM((LANES, LANES), F32),
            pltpu.VMEM((n_slots, LANES, LANES), BF16),
            pltpu.VMEM((n_slots, LANES, LANES), F32),
            pltpu.VMEM((n_slots, LANES, LANES), BF16),
            pltpu.VMEM((n_slots, LANES, 2 * LANES), BF16),
            pltpu.VMEM((n_slots, LANES, LANES), F32),
            pltpu.VMEM((n_slots, LANES, LANES), BF16),
            pltpu.VMEM((2, RWKV_BLOCK, LANES), F32),
            pltpu.VMEM((2, RWKV_BLOCK, LANES), F32),
            pltpu.VMEM((2, RWKV_BLOCK, LANES), F32),
            pltpu.VMEM((2, RWKV_BLOCK, LANES), F32),
            pltpu.VMEM((n_slots, LANES, LANES), BF16),
            pltpu.VMEM((n_slots, LANES, LANES), F32),
            pltpu.VMEM((n_slots, LANES, LANES), BF16),
            pltpu.VMEM((n_slots, LANES, LANES), F32),
            pltpu.VMEM((n_slots, LANES, LANES), F32),
        ],
        compiler_params=pltpu.CompilerParams(
            dimension_semantics=("arbitrary",),
            vmem_limit_bytes=VMEM_LIMIT),
        name="rwkv7_time_mix",
    )(proj3, proj3, proj3, proj3, lora3, mu_r, mu_k, mu_v, mu_l,
      w0, a0, k_k, k_a, wa2, g2p, r_k, lnx_g, lnx_b)


def _out_proj_kernel(att_ref, rwk_ref, wa_ref, wr_ref, x_ref, g_ref, o_ref, *, normalize):
    y = x_ref[...] + _dot(att_ref[...], wa_ref[...]) + _dot(rwk_ref[...], wr_ref[...])
    if normalize:
        ms = jnp.mean(y * y, axis=-1, keepdims=True)
        y = y * lax.rsqrt(ms + NORM_EPS) * g_ref[...]
    o_ref[...] = y


def _out_proj(att2, rwk2, w_out_b, x2, g, normalize):
    tokens = x2.shape[0]
    return pl.pallas_call(
        functools.partial(_out_proj_kernel, normalize=normalize),
        grid=(tokens // OUT_TM,),
        in_specs=[
            pl.BlockSpec((OUT_TM, ATT_WIDTH), lambda i: (i, 0)),
            pl.BlockSpec((OUT_TM, RWKV_WIDTH), lambda i: (i, 0)),
            pl.BlockSpec((ATT_WIDTH, D_MODEL), lambda i: (0, 0)),
            pl.BlockSpec((RWKV_WIDTH, D_MODEL), lambda i: (ATT_WIDTH // RWKV_WIDTH, 0)),
            pl.BlockSpec((OUT_TM, D_MODEL), lambda i: (i, 0)),
            pl.BlockSpec((1, D_MODEL), lambda i: (0, 0)),
        ],
        out_specs=pl.BlockSpec((OUT_TM, D_MODEL), lambda i: (i, 0)),
        out_shape=jax.ShapeDtypeStruct((tokens, D_MODEL), F32),
        compiler_params=pltpu.CompilerParams(
            dimension_semantics=("arbitrary",),
            vmem_limit_bytes=VMEM_LIMIT),
        name="out_proj",
    )(att2, rwk2, w_out_b, w_out_b, x2, g)


def _rope_tables(seq):
    inv_freq = ROPE_THETA ** (-jnp.arange(0, HEAD_DIM, 2, dtype=F32) / HEAD_DIM)
    ang = jnp.arange(seq, dtype=jnp.int32).astype(F32)[:, None] * inv_freq[None, :]
    cos, sin = jnp.cos(ang), jnp.sin(ang)
    reps = LANES // HEAD_DIM
    cos_t = jnp.tile(jnp.concatenate([cos, cos], axis=1), (1, reps))
    sin_t = jnp.tile(jnp.concatenate([-sin, sin], axis=1), (1, reps))
    return cos_t, sin_t


def _layer(x, norm_g, w_in, shift_mu, w0, w2, a0, a2, g2, k_k, k_a, r_k,
           lnx_g, lnx_b, w_out, cos_t, sin_t):
    batch, seq, _ = x.shape
    a_w, r_w = ATT_WIDTH, RWKV_WIDTH
    lo = 4 * a_w + 3 * r_w
    n_lora = DECAY_LORA + AAA_LORA + GATE_LORA
    assert lo == COL_ZR
    w_all = w_in.astype(BF16)
    w_gate = w_all[:, lo + n_lora:]
    w_lora = jnp.pad(w_all[:, lo:lo + n_lora], ((0, 0), (0, LORA_PAD - n_lora)))
    mu_r = shift_mu[None, 0:r_w]
    mu_k = shift_mu[None, r_w:2 * r_w]
    mu_v = shift_mu[None, 2 * r_w:3 * r_w]
    mu_l = jnp.pad(shift_mu[None, 3 * r_w:], ((0, 0), (0, LORA_PAD - n_lora)))
    w2p = jnp.pad(w2, ((0, LANES - DECAY_LORA), (0, 0))).reshape(LANES, N_PAIRS, LANES)
    a2p = jnp.pad(a2, ((DECAY_LORA, LANES - DECAY_LORA - AAA_LORA), (0, 0))
                  ).reshape(LANES, N_PAIRS, LANES)
    wa2 = jnp.concatenate([w2p, a2p], axis=2).reshape(LANES, 2 * r_w)
    g2p = jnp.pad(g2, ((0, GATE_PAD - GATE_LORA), (0, 0)))

    x2 = x.reshape(batch * seq, D_MODEL)
    proj, lora = _in_proj(x2, norm_g[None, :], w_all, w_gate, w_lora, cos_t, sin_t, seq)
    proj3 = proj.reshape(batch, seq, PROJ_WIDTH)
    lora3 = lora.reshape(batch, seq, LORA_PAD)
    att = _attention(proj3)
    rwk = _rwkv(proj3, lora3, mu_r, mu_k, mu_v, mu_l, w0[None, :], a0[None, :],
                k_k[None, :], k_a[None, :], r_k.reshape(1, r_w), lnx_g[None, :],
                lnx_b[None, :], wa2, g2p)
    w_out_b = w_out.astype(BF16)
    return att.reshape(batch * seq, a_w), rwk.reshape(batch * seq, r_w), w_out_b, x2


def kernel(x, norm_g, w_in, shift_mu, w0, w2, a0, a2, g2, k_k, k_a, r_k,
           lnx_g, lnx_b, w_out, final_g):
    batch, seq, _ = x.shape
    depth = norm_g.shape[0]
    assert seq % SUPER == 0 and seq % IN_TM == 0 and (batch * seq) % OUT_TM == 0
    cos_t, sin_t = _rope_tables(seq)
    for l in range(depth):
        att2, rwk2, w_out_b, x2 = _layer(
            x, norm_g[l], w_in[l], shift_mu[l], w0[l], w2[l], a0[l], a2[l], g2[l],
            k_k[l], k_a[l], r_k[l], lnx_g[l], lnx_b[l], w_out[l], cos_t, sin_t)
        y2 = _out_proj(att2, rwk2, w_out_b, x2, final_g[None, :],
                       normalize=(l == depth - 1))
        x = y2.reshape(batch, seq, D_MODEL)
    return x
```

```python
import functools

import jax
import jax.numpy as jnp
from jax import lax
from jax.experimental import pallas as pl
from jax.experimental.pallas import tpu as pltpu

F32 = jnp.float32
BF16 = jnp.bfloat16

D_MODEL = 2048
HEAD_DIM = 64
ATT_WIDTH = 1024
RWKV_WIDTH = 1024
DIL_PATTERNS = ((128, 1), (512, 4), (2048, 16))
ROPE_THETA = 10000.0
DECAY_LORA = 64
AAA_LORA = 64
GATE_LORA = 160
NORM_EPS = 1e-5
LNX_EPS = 64e-5
DECAY_SCALE = 0.6065306597126334
KK_NORM_FLOOR = 1e-12
LOG2_E = 1.4426950408889634

LANES = 128
N_PAIRS = ATT_WIDTH // LANES
N_BACK = 128
Q_BLOCK = 128
SUPER = 2048
CHUNK = 64
RWKV_BLOCK = 512
N_CHUNKS = RWKV_BLOCK // CHUNK
LORA_PAD = 512
LORA_USED = 384
GATE_PAD = 256

COL_Q, COL_K, COL_V, COL_ZA = 0, 1024, 2048, 3072
COL_R, COL_RK, COL_RV, COL_ZR = 4096, 5120, 6144, 7168
PROJ_WIDTH = 8192

IN_TM, IN_TN = 1024, 1024
IN_SUB = 512
N_ROPE_STEPS = (COL_V - COL_Q) // IN_TN
N_Q_STEPS = (COL_K - COL_Q) // IN_TN
N_ALIGNED_STEPS = COL_ZR // IN_TN
N_SHIFT_LO = COL_R // IN_TN
OUT_TM = 512
VMEM_LIMIT = 56 * 1024 * 1024


def _dot(a, b):
    return jnp.dot(a, b, preferred_element_type=F32)


def _dot_nt(a, b):
    return lax.dot_general(a, b, (((1,), (1,)), ((), ())), preferred_element_type=F32)


def _sigmoid(x):
    return 0.5 * jnp.tanh(0.5 * x) + 0.5


def _bf16_pieces(x, terms):
    pieces = []
    for _ in range(terms):
        piece = x.astype(BF16)
        pieces.append(piece)
        x = x - piece.astype(F32)
    return pieces


def _head_sum(x, ones2):
    return _dot(jnp.concatenate(_bf16_pieces(x, 2), axis=1), ones2)


def _in_proj_kernel(x_ref, g_ref, w_ref, wz_ref, wl_ref, cos_ref, sin_ref,
                    mu_ref, mul_ref, o_ref, lora_ref, h_ref, carry_ref, carry_lora_ref,
                    *, pos_blocks):
    i = pl.program_id(0)
    j = pl.program_id(1)
    subs = [slice(s * IN_SUB, (s + 1) * IN_SUB) for s in range(IN_TN // IN_SUB)]
    seq_start = (i % pos_blocks) == 0
    last_row = slice(IN_TM - 1, IN_TM)

    def token_shift(u, carry_row, mu):
        row0 = lax.broadcasted_iota(jnp.int32, u.shape, 0) == 0
        prev = jnp.where(row0, jnp.where(seq_start, 0.0, carry_row), pltpu.roll(u, 1, 0))
        return u + (prev - u) * mu

    @pl.when((i == 0) & (j == 0))
    def _():
        carry_ref[...] = jnp.zeros_like(carry_ref)
        carry_lora_ref[...] = jnp.zeros_like(carry_lora_ref)

    @pl.when(j == 0)
    def _():
        x = x_ref[...]
        ms = jnp.mean(x * x, axis=-1, keepdims=True)
        h_ref[...] = (x * lax.rsqrt(ms + NORM_EPS) * g_ref[...]).astype(BF16)
        lora = _dot(h_ref[...], wl_ref[...])
        lora_ref[...] = token_shift(lora, carry_lora_ref[0:1, :], mul_ref[...])
        carry_lora_ref[0:1, :] = lora[last_row, :]

    @pl.when(j < N_ROPE_STEPS)
    def _():
        reps = IN_SUB // LANES
        cos = jnp.concatenate([cos_ref[...]] * reps, axis=1)
        sin = jnp.concatenate([sin_ref[...]] * reps, axis=1)
        lane = lax.broadcasted_iota(jnp.int32, (IN_TM, IN_SUB), 1)
        first_half = (lane % HEAD_DIM) < (HEAD_DIM // 2)
        scale = jnp.where(j < N_Q_STEPS, LOG2_E * HEAD_DIM ** -0.5, 1.0).astype(F32)
        for sub in subs:
            acc = _dot(h_ref[...], w_ref[:, sub])
            partner = jnp.where(first_half,
                                pltpu.roll(acc, IN_SUB - HEAD_DIM // 2, 1),
                                pltpu.roll(acc, HEAD_DIM // 2, 1))
            o_ref[:, sub] = (acc * cos + partner * sin) * scale

    @pl.when((j >= N_ROPE_STEPS) & (j < N_SHIFT_LO))
    def _():
        for sub in subs:
            o_ref[:, sub] = _dot(h_ref[...], w_ref[:, sub])

    @pl.when((j >= N_SHIFT_LO) & (j < N_ALIGNED_STEPS))
    def _():
        slot = j - N_SHIFT_LO
        for sub in subs:
            acc = _dot(h_ref[...], w_ref[:, sub])
            o_ref[:, sub] = token_shift(acc, carry_ref[slot, 0:1, sub], mu_ref[:, sub])
            carry_ref[slot, 0:1, sub] = acc[last_row, :]

    @pl.when(j >= N_ALIGNED_STEPS)
    def _():
        for sub in subs:
            o_ref[:, sub] = _dot(h_ref[...], wz_ref[:, sub])


def _in_proj(x2, g, w_all, w_gate, w_lora, cos_t, sin_t, mu_main, mu_lora, seq):
    tokens = x2.shape[0]
    pos_blocks = seq // IN_TM
    n_shift = N_ALIGNED_STEPS - N_SHIFT_LO

    def mu_index(i, j):
        return 0, jnp.clip(j - N_SHIFT_LO, 0, n_shift - 1)

    return pl.pallas_call(
        functools.partial(_in_proj_kernel, pos_blocks=pos_blocks),
        grid=(tokens // IN_TM, PROJ_WIDTH // IN_TN),
        in_specs=[
            pl.BlockSpec((IN_TM, D_MODEL), lambda i, j: (i, 0)),
            pl.BlockSpec((1, D_MODEL), lambda i, j: (0, 0)),
            pl.BlockSpec((D_MODEL, IN_TN), lambda i, j: (0, jnp.minimum(j, N_ALIGNED_STEPS - 1))),
            pl.BlockSpec((D_MODEL, IN_TN), lambda i, j: (0, 0)),
            pl.BlockSpec((D_MODEL, LORA_PAD), lambda i, j: (0, 0)),
            pl.BlockSpec((IN_TM, LANES), lambda i, j: (i % pos_blocks, 0)),
            pl.BlockSpec((IN_TM, LANES), lambda i, j: (i % pos_blocks, 0)),
            pl.BlockSpec((1, IN_TN), mu_index),
            pl.BlockSpec((1, LORA_PAD), lambda i, j: (0, 0)),
        ],
        out_specs=[
            pl.BlockSpec((IN_TM, IN_TN), lambda i, j: (i, j)),
            pl.BlockSpec((IN_TM, LORA_PAD), lambda i, j: (i, 0)),
        ],
        out_shape=[
            jax.ShapeDtypeStruct((tokens, PROJ_WIDTH), F32),
            jax.ShapeDtypeStruct((tokens, LORA_PAD), F32),
        ],
        scratch_shapes=[
            pltpu.VMEM((IN_TM, D_MODEL), BF16),
            pltpu.VMEM((n_shift, 8, IN_TN), F32),
            pltpu.VMEM((8, LORA_PAD), F32),
        ],
        compiler_params=pltpu.CompilerParams(
            dimension_semantics=("arbitrary", "arbitrary"),
            vmem_limit_bytes=VMEM_LIMIT),
        name="in_proj",
    )(x2, g, w_all, w_gate, w_lora, cos_t, sin_t, mu_main, mu_lora)


def _attention_kernel(q_ref, kc_ref, kp_ref, vc_ref, vp_ref, z_ref, o_ref,
                      kbuf, vbuf, acc_s, m_s, den_s):
    i = pl.program_id(2)
    kbuf[0:SUPER, :] = kp_ref[0]
    kbuf[SUPER:2 * SUPER, :] = kc_ref[0]
    vbuf[0:SUPER, :] = vp_ref[0]
    vbuf[SUPER:2 * SUPER, :] = vc_ref[0]

    qi = lax.broadcasted_iota(jnp.int32, (Q_BLOCK, 2 * Q_BLOCK), 0)
    ki = lax.broadcasted_iota(jnp.int32, (Q_BLOCK, 2 * Q_BLOCK), 1)
    rel = Q_BLOCK + qi - ki
    band = (rel >= 0) & (rel <= N_BACK)
    cur = ki >= Q_BLOCK
    lane = lax.broadcasted_iota(jnp.int32, (Q_BLOCK, LANES), 1)
    head0 = lane < HEAD_DIM

    def block(pat, dil, q_start):
        rows_q = pl.ds(q_start, Q_BLOCK, stride=dil) if dil > 1 else pl.ds(q_start, Q_BLOCK)
        k_start = SUPER + q_start - Q_BLOCK * dil
        rows_k = (pl.ds(k_start, 2 * Q_BLOCK, stride=dil) if dil > 1
                  else pl.ds(k_start, 2 * Q_BLOCK))
        qs = q_ref[0, rows_q, :]
        kw = kbuf[rows_k, :].astype(BF16)
        vw = vbuf[rows_k, :].astype(BF16)
        prev_ok = (i * SUPER + q_start) >= Q_BLOCK * dil
        valid = band & (cur | prev_ok)
        accs, ms, dens = [], [], []
        for hm in (head0, ~head0):
            qh = jnp.where(hm, qs, 0.0).astype(BF16)
            s = _dot_nt(qh, kw)
            s = jnp.where(valid, s, -jnp.inf)
            m = jnp.max(s, axis=-1, keepdims=True)
            p = jnp.exp2(s - m)
            dens.append(jnp.sum(p, axis=-1, keepdims=True))
            ms.append(m)
            accs.append(_dot(p.astype(BF16), vw))
        acc_s[pat, rows_q, :] = jnp.where(head0, accs[0], accs[1])
        m_s[pat, rows_q, :] = jnp.where(head0, ms[0], ms[1])
        den_s[pat, rows_q, :] = jnp.where(head0, dens[0], dens[1])

    for pat, (window, dil) in enumerate(DIL_PATTERNS):
        span = Q_BLOCK * dil
        for s_idx in range(SUPER // span):
            for r in range(dil):
                block(pat, dil, s_idx * span + r)

    m0, m1, m2 = m_s[0], m_s[1], m_s[2]
    mx = jnp.maximum(jnp.maximum(m0, m1), m2)
    e0, e1, e2 = jnp.exp2(m0 - mx), jnp.exp2(m1 - mx), jnp.exp2(m2 - mx)
    num = e0 * acc_s[0] + e1 * acc_s[1] + e2 * acc_s[2]
    den = e0 * den_s[0] + e1 * den_s[1] + e2 * den_s[2]
    z = z_ref[0]
    o_ref[0] = (num / den * (z * _sigmoid(z))).astype(o_ref.dtype)


def _attention(proj3):
    batch, seq, _ = proj3.shape
    blk = (1, SUPER, LANES)

    def col(base):
        return lambda b, p, i: (b, i, base // LANES + p)

    def col_prev(base):
        return lambda b, p, i: (b, jnp.maximum(i - 1, 0), base // LANES + p)

    return pl.pallas_call(
        _attention_kernel,
        grid=(batch, N_PAIRS, seq // SUPER),
        in_specs=[
            pl.BlockSpec(blk, col(COL_Q)),
            pl.BlockSpec(blk, col(COL_K)),
            pl.BlockSpec(blk, col_prev(COL_K)),
            pl.BlockSpec(blk, col(COL_V)),
            pl.BlockSpec(blk, col_prev(COL_V)),
            pl.BlockSpec(blk, col(COL_ZA)),
        ],
        out_specs=pl.BlockSpec(blk, lambda b, p, i: (b, i, p)),
        out_shape=jax.ShapeDtypeStruct((batch, seq, ATT_WIDTH), BF16),
        scratch_shapes=[
            pltpu.VMEM((2 * SUPER, LANES), F32),
            pltpu.VMEM((2 * SUPER, LANES), F32),
            pltpu.VMEM((len(DIL_PATTERNS), SUPER, LANES), F32),
            pltpu.VMEM((len(DIL_PATTERNS), SUPER, LANES), F32),
            pltpu.VMEM((len(DIL_PATTERNS), SUPER, LANES), F32),
        ],
        compiler_params=pltpu.CompilerParams(
            dimension_semantics=("arbitrary", "arbitrary", "arbitrary"),
            vmem_limit_bytes=VMEM_LIMIT),
        name="dilated_attention",
    )(proj3, proj3, proj3, proj3, proj3, proj3)


def _rwkv_kernel(r_ref, k_ref, v_ref, z_ref, lora_ref,
                 w0_ref, a0_ref, kk_ref, ka_ref, wa2_ref, g2_ref,
                 rk_ref, lng_ref, lnb_ref,
                 o_ref,
                 state,
                 pm_a, pm_r, pm_kb, pm_bk, pm_d, pm_v,
                 pc_r, pc_k, pc_v, pc_gz,
                 mc_rw, mc_y0, mc_m, mc_g, mc_d,
                 *, blocks_per_stream):
    t = pl.program_id(0)
    first_chain = ((2 * jnp.maximum(t - 1, 0)) % blocks_per_stream) == 0

    @pl.when(t == 0)
    def _():
        for ref in (state, pm_a, pm_r, pm_kb, pm_bk, pm_d, pm_v,
                    pc_r, pc_k, pc_v, pc_gz, mc_rw, mc_y0, mc_m, mc_g, mc_d):
            ref[...] = jnp.zeros_like(ref)

    rr = lax.broadcasted_iota(jnp.int32, (LANES, LANES), 0)
    cc = lax.broadcasted_iota(jnp.int32, (LANES, LANES), 1)
    same_head = (rr // HEAD_DIM) == (cc // HEAD_DIM)
    strict = same_head & ((cc % CHUNK) < (rr % CHUNK))
    incl = same_head & ((cc % CHUNK) <= (rr % CHUNK))
    eye = (rr == cc).astype(F32)
    head_ones2 = ((lax.broadcasted_iota(jnp.int32, (2 * LANES, LANES), 0) % LANES) // HEAD_DIM
                  == lax.broadcasted_iota(jnp.int32, (2 * LANES, LANES), 1) // HEAD_DIM
                  ).astype(BF16)
    tril = (lax.broadcasted_iota(jnp.int32, (CHUNK, CHUNK), 1)
            <= lax.broadcasted_iota(jnp.int32, (CHUNK, CHUNK), 0)).astype(BF16)
    head0 = lax.broadcasted_iota(jnp.int32, (CHUNK, LANES), 1) < HEAD_DIM
    decay_lane = lax.broadcasted_iota(jnp.int32, (CHUNK, LANES), 1) < DECAY_LORA
    zero_blk = jnp.zeros((LANES, LANES), BF16)
    last = slice(CHUNK - 1, CHUNK)

    def bd(x):
        return jnp.concatenate([jnp.where(head0, x, 0.0), jnp.where(head0, 0.0, x)], axis=0)

    rows = [slice(c * CHUNK, (c + 1) * CHUNK) for c in range(N_CHUNKS)]
    cur = {"half": 0, "slot_p": 0, "slot_m": 1}

    def block_rows(c):
        start = cur["half"] * RWKV_BLOCK + c * CHUNK
        return slice(start, start + CHUNK)

    pv = [dict() for _ in range(N_CHUNKS)]

    def prep_lora_acts(c):
        lora = lora_ref[0, block_rows(c), 0:LORA_USED]
        wa = lora[:, 0:LANES]
        pv[c]["wa_act"] = jnp.where(decay_lane, jnp.tanh(wa), wa).astype(BF16)
        pv[c]["gate_act"] = _sigmoid(lora[:, LANES:LANES + GATE_PAD]).astype(BF16)

    def prep_lora_matmuls(c):
        d = pv[c]
        wa_out = _dot(d["wa_act"], wa2_ref[...].astype(BF16))
        d["lw"] = -(DECAY_SCALE * _sigmoid(w0_ref[...] + wa_out[:, 0:LANES]))
        d["a"] = _sigmoid(a0_ref[...] + wa_out[:, LANES:])
        g = _dot(d["gate_act"], g2_ref[...].astype(BF16))
        z = z_ref[0, block_rows(c), :]
        d["gz"] = g * (z * _sigmoid(z))

    def prep_key_norm(c):
        d = pv[c]
        k = k_ref[0, block_rows(c), :]
        kk = k * kk_ref[...]
        kk = kk * lax.rsqrt(jnp.maximum(_head_sum(kk * kk, head_ones2), KK_NORM_FLOOR ** 2))
        d["kk"] = kk
        d["k2"] = k * (1.0 + (d["a"] - 1.0) * ka_ref[...])
        d["b"] = kk * d["a"]
        parts = _dot(tril, jnp.concatenate(_bf16_pieces(d["lw"], 3), axis=1))
        d["cs"] = parts[:, 0:LANES] + parts[:, LANES:2 * LANES] + parts[:, 2 * LANES:]

    def prep_exp(c):
        d = pv[c]
        d["e_pos"] = jnp.exp(d["cs"])
        d["e_neg"] = jnp.exp(-d["cs"])
        d["e_excl"] = jnp.exp(d["cs"] - d["lw"])
        d["decay_end"] = d["e_pos"][last, :]

    def prep_store_ar(c):
        d = pv[c]
        idx = cur["slot_p"] * N_CHUNKS + c
        pm_a[idx] = bd(-d["kk"] * d["e_excl"]).astype(BF16)
        pm_r[idx] = bd(r_ref[0, block_rows(c), :] * d["e_pos"])

    def prep_store_kbv(c):
        d = pv[c]
        slot_p = cur["slot_p"]
        idx = slot_p * N_CHUNKS + c
        pm_kb[idx] = jnp.concatenate([d["k2"] * d["e_neg"], d["b"] * d["e_neg"]],
                                     axis=0).astype(BF16)
        pm_v[idx] = bd(v_ref[0, block_rows(c), :]).astype(BF16)

    def prep_store_pc(c):
        d = pv[c]
        slot_p = cur["slot_p"]
        pc_r[slot_p, rows[c], :] = r_ref[0, block_rows(c), :]
        pc_k[slot_p, rows[c], :] = d["k2"]
        pc_v[slot_p, rows[c], :] = v_ref[0, block_rows(c), :]
        pc_gz[slot_p, rows[c], :] = d["gz"]

    def prep_store_bk(c):
        d = pv[c]
        idx = cur["slot_p"] * N_CHUNKS + c
        e_end = d["decay_end"] * d["e_neg"]
        bkd_t = jnp.concatenate([bd(d["b"] * e_end), bd(d["k2"] * e_end),
                                 jnp.broadcast_to(d["decay_end"], (LANES, LANES))], axis=0).T
        pm_bk[idx] = bkd_t[:, 0:2 * LANES].astype(BF16)
        pm_d[idx] = bkd_t[:, 2 * LANES:]

    prep_groups = [(prep_lora_acts, 110), (prep_lora_matmuls, 110),
                   (prep_key_norm, 150), (prep_exp, 50), (prep_store_ar, 56),
                   (prep_store_kbv, 50), (prep_store_bk, 75)]

    mm = [dict() for _ in range(N_CHUNKS)]

    def mm_scores(c):
        idx = cur["slot_m"] * N_CHUNKS + c
        kb = pm_kb[idx]
        k_t, b_t = kb[0:CHUNK, :], kb[CHUNK:, :]
        lhs = jnp.concatenate([pm_a[idx], pm_r[idx].astype(BF16)], axis=0)
        x = _dot_nt(lhs, jnp.concatenate([k_t, k_t, b_t, b_t], axis=0))
        d = mm[c]
        d["a_ak"] = jnp.where(strict, x[0:LANES, 0:LANES], 0.0).astype(BF16)
        a_ab = jnp.where(strict, x[0:LANES, LANES:], 0.0)
        d["a_r"] = jnp.concatenate([jnp.where(incl, x[LANES:, 0:LANES], 0.0),
                                    jnp.where(incl, x[LANES:, LANES:], 0.0)],
                                   axis=1).astype(BF16)
        d["q"] = a_ab.astype(BF16)
        d["p"] = eye + a_ab

    def mm_square(c):
        d = mm[c]
        d["q"] = _dot(d["q"], d["q"]).astype(BF16)

    def mm_double(c):
        d = mm[c]
        qp = _dot(d["q"], jnp.concatenate([d["q"], d["p"].astype(BF16)], axis=1))
        d["q"] = qp[:, 0:LANES].astype(BF16)
        d["p"] = d["p"] + qp[:, LANES:]

    def mm_inverse(c):
        d = mm[c]
        d["t"] = (d["p"] + _dot(d["q"], d["p"].astype(BF16))).astype(BF16)

    def mm_akv(c):
        d = mm[c]
        d["ak_v"] = _dot(d["a_ak"], pm_v[cur["slot_m"] * N_CHUNKS + c]).astype(BF16)

    def mm_wu(c):
        d = mm[c]
        rhs = jnp.concatenate([pm_a[cur["slot_m"] * N_CHUNKS + c], d["ak_v"]], axis=1)
        d["wu"] = _dot(d["t"], rhs).astype(BF16)

    def mm_yw(c):
        d = mm[c]
        idx = cur["slot_m"] * N_CHUNKS + c
        wu = d["wu"]
        rhs = jnp.concatenate(
            [jnp.concatenate([pm_v[idx], zero_blk], axis=1),
             jnp.concatenate([wu[:, LANES:], wu[:, 0:LANES]], axis=1)], axis=0)
        yw = _dot(d["a_r"], rhs)
        mc_y0[idx] = yw[:, 0:LANES]
        mc_rw[idx] = (pm_r[idx] + yw[:, LANES:]).astype(BF16)

    def mm_mg(c):
        idx = cur["slot_m"] * N_CHUNKS + c
        rhs = jnp.concatenate([mm[c]["wu"],
                               jnp.concatenate([zero_blk, pm_v[idx]], axis=1)], axis=0)
        mg = _dot(pm_bk[idx], rhs)
        mc_m[idx] = mg[:, 0:LANES].astype(BF16)
        mc_g[idx] = mg[:, LANES:]
        mc_d[idx] = pm_d[idx]

    matmul_stages = [mm_scores, mm_square, mm_double, mm_double, mm_double, mm_double,
                     mm_inverse, mm_akv, mm_wu, mm_yw, mm_mg]

    chain = {"hs": jnp.where(first_chain, 0.0, state[...])}

    cv = [dict() for _ in range(N_CHUNKS)]

    def chain_core(c):
        idx = cur["slot_p"] * N_CHUNKS + c
        hs = chain["hs"]
        hs_b = hs.astype(BF16)
        y = _dot(mc_rw[idx], hs_b) + mc_y0[idx]
        chain["hs"] = hs * mc_d[idx] + _dot(mc_m[idx], hs_b) + mc_g[idx]
        cv[c]["y"] = y[0:CHUNK, :] + y[CHUNK:, :]

    def chain_mean(c):
        slot_p, rws, d = cur["slot_p"], rows[c], cv[c]
        d["yc"] = d["y"] - _head_sum(d["y"], head_ones2) * (1.0 / HEAD_DIM)
        d["bonus"] = _head_sum(pc_r[slot_p, rws, :] * pc_k[slot_p, rws, :] * rk_ref[...],
                               head_ones2) * pc_v[slot_p, rws, :]

    def chain_finish(c):
        slot_p, rws, d = cur["slot_p"], rows[c], cv[c]
        var = _head_sum(d["yc"] * d["yc"], head_ones2) * (1.0 / HEAD_DIM)
        yn = d["yc"] * lax.rsqrt(var + LNX_EPS) * lng_ref[...] + lnb_ref[...]
        o_ref[0, block_rows(c), :] = ((yn + d["bonus"]) * pc_gz[slot_p, rws, :]
                                      ).astype(o_ref.dtype)

    n_rounds = len(matmul_stages)
    assert n_rounds >= N_CHUNKS + 3
    n_slots = (n_rounds - 1) * N_CHUNKS
    total = float(N_CHUNKS * sum(w for _, w in prep_groups))
    prep_slots = [[] for _ in range(n_slots)]
    done = 0.0
    for fn, weight in prep_groups:
        for c in range(N_CHUNKS):
            prep_slots[min(int(done / total * n_slots), n_slots - 1)].append((fn, c))
            done += weight
    for half in (0, 1):
        cur.update(half=half, slot_p=half, slot_m=1 - half)
        for rnd, mm_stage in enumerate(matmul_stages):
            if rnd < N_CHUNKS:
                chain_core(rnd)
            if 1 <= rnd <= N_CHUNKS:
                chain_mean(rnd - 1)
            if 2 <= rnd <= N_CHUNKS + 1:
                chain_finish(rnd - 2)
            for c in range(N_CHUNKS):
                mm_stage(c)
                if rnd == n_rounds - 1:
                    prep_store_pc(c)
                else:
                    for fn, pc in prep_slots[rnd * N_CHUNKS + c]:
                        fn(pc)
    state[...] = chain["hs"]


def _rwkv(proj3, lora3, w0, a0, k_k, k_a, r_k, lnx_g, lnx_b, wa2, g2p):
    batch, seq, _ = proj3.shape
    bps = seq // RWKV_BLOCK
    assert bps % 2 == 0
    wps = bps // 2
    n_total = batch * N_PAIRS * wps
    blk = (1, 2 * RWKV_BLOCK, LANES)

    def split(n):
        return n // (N_PAIRS * wps), n % wps, (n // wps) % N_PAIRS

    def prep_n(s):
        return jnp.minimum(s, n_total - 1)

    def chain_n(s):
        return jnp.maximum(s - 1, 0)

    def col(base):
        def index(s):
            b, i, p = split(prep_n(s))
            return b, i, base // LANES + p
        return index

    def lora_index(s):
        b, i, _ = split(prep_n(s))
        return b, i, 0

    def out_index(s):
        return split(chain_n(s))

    vec_prep = pl.BlockSpec((1, LANES), lambda s: (0, split(prep_n(s))[2]))
    vec_chain = pl.BlockSpec((1, LANES), lambda s: (0, split(chain_n(s))[2]))
    n_slots = 2 * N_CHUNKS
    return pl.pallas_call(
        functools.partial(_rwkv_kernel, blocks_per_stream=bps),
        grid=(n_total + 1,),
        in_specs=[
            pl.BlockSpec(blk, col(COL_R)),
            pl.BlockSpec(blk, col(COL_RK)),
            pl.BlockSpec(blk, col(COL_RV)),
            pl.BlockSpec(blk, col(COL_ZR)),
            pl.BlockSpec((1, 2 * RWKV_BLOCK, LORA_PAD), lora_index),
            vec_prep, vec_prep, vec_prep, vec_prep,
            pl.BlockSpec((LANES, 2 * LANES), lambda s: (0, split(prep_n(s))[2])),
            pl.BlockSpec((GATE_PAD, LANES), lambda s: (0, split(prep_n(s))[2])),
            vec_chain, vec_chain, vec_chain,
        ],
        out_specs=pl.BlockSpec(blk, out_index),
        out_shape=jax.ShapeDtypeStruct((batch, seq, RWKV_WIDTH), BF16),
        scratch_shapes=[
            pltpu.VMEM((LANES, LANES), F32),
            pltpu.VMEM((n_slots, LANES, LANES), BF16),
            pltpu.VMEM((n_slots, LANES, LANES), F32),
            pltpu.VMEM((n_slots, LANES, LANES), BF16),
            pltpu.VMEM((n_slots, LANES, 2 * LANES), BF16),
            pltpu.VMEM((n_slots, LANES, LANES), F32),
            pltpu.VMEM((n_slots, LANES, LANES), BF16),
            pltpu.VMEM((2, RWKV_BLOCK, LANES), F32),
            pltpu.VMEM((2, RWKV_BLOCK, LANES), F32),
            pltpu.VMEM((2, RWKV_BLOCK, LANES), F32),
            pltpu.VMEM((2, RWKV_BLOCK, LANES), F32),
            pltpu.VMEM((n_slots, LANES, LANES), BF16),
            pltpu.VMEM((n_slots, LANES, LANES), F32),
            pltpu.VMEM((n_slots, LANES, LANES), BF16),
            pltpu.VMEM((n_slots, LANES, LANES), F32),
            pltpu.VMEM((n_slots, LANES, LANES), F32),
        ],
        compiler_params=pltpu.CompilerParams(
            dimension_semantics=("arbitrary",),
            vmem_limit_bytes=VMEM_LIMIT),
        name="rwkv7_time_mix",
    )(proj3, proj3, proj3, proj3, lora3, w0, a0, k_k, k_a, wa2, g2p, r_k, lnx_g, lnx_b)


def _out_proj_kernel(att_ref, rwk_ref, wa_ref, wr_ref, x_ref, g_ref, o_ref, *, normalize):
    y = x_ref[...] + _dot(att_ref[...], wa_ref[...]) + _dot(rwk_ref[...], wr_ref[...])
    if normalize:
        ms = jnp.mean(y * y, axis=-1, keepdims=True)
        y = y * lax.rsqrt(ms + NORM_EPS) * g_ref[...]
    o_ref[...] = y


def _out_proj(att2, rwk2, w_out_b, x2, g, normalize):
    tokens = x2.shape[0]
    return pl.pallas_call(
        functools.partial(_out_proj_kernel, normalize=normalize),
        grid=(tokens // OUT_TM,),
        in_specs=[
            pl.BlockSpec((OUT_TM, ATT_WIDTH), lambda i: (i, 0)),
            pl.BlockSpec((OUT_TM, RWKV_WIDTH), lambda i: (i, 0)),
            pl.BlockSpec((ATT_WIDTH, D_MODEL), lambda i: (0, 0)),
            pl.BlockSpec((RWKV_WIDTH, D_MODEL), lambda i: (ATT_WIDTH // RWKV_WIDTH, 0)),
            pl.BlockSpec((OUT_TM, D_MODEL), lambda i: (i, 0)),
            pl.BlockSpec((1, D_MODEL), lambda i: (0, 0)),
        ],
        out_specs=pl.BlockSpec((OUT_TM, D_MODEL), lambda i: (i, 0)),
        out_shape=jax.ShapeDtypeStruct((tokens, D_MODEL), F32),
        compiler_params=pltpu.CompilerParams(
            dimension_semantics=("arbitrary",),
            vmem_limit_bytes=VMEM_LIMIT),
        name="out_proj",
    )(att2, rwk2, w_out_b, w_out_b, x2, g)


def _rope_tables(seq):
    inv_freq = ROPE_THETA ** (-jnp.arange(0, HEAD_DIM, 2, dtype=F32) / HEAD_DIM)
    ang = jnp.arange(seq, dtype=jnp.int32).astype(F32)[:, None] * inv_freq[None, :]
    cos, sin = jnp.cos(ang), jnp.sin(ang)
    reps = LANES // HEAD_DIM
    cos_t = jnp.tile(jnp.concatenate([cos, cos], axis=1), (1, reps))
    sin_t = jnp.tile(jnp.concatenate([-sin, sin], axis=1), (1, reps))
    return cos_t, sin_t


def _layer(x, norm_g, w_in, shift_mu, w0, w2, a0, a2, g2, k_k, k_a, r_k,
           lnx_g, lnx_b, w_out, cos_t, sin_t):
    batch, seq, _ = x.shape
    a_w, r_w = ATT_WIDTH, RWKV_WIDTH
    lo = 4 * a_w + 3 * r_w
    n_lora = DECAY_LORA + AAA_LORA + GATE_LORA
    assert lo == COL_ZR
    w_all = w_in.astype(BF16)
    w_gate = w_all[:, lo + n_lora:]
    w_lora = jnp.pad(w_all[:, lo:lo + n_lora], ((0, 0), (0, LORA_PAD - n_lora)))
    mu_main = shift_mu[None, 0:3 * r_w]
    mu_lora = jnp.pad(shift_mu[None, 3 * r_w:], ((0, 0), (0, LORA_PAD - n_lora)))
    w2p = jnp.pad(w2, ((0, LANES - DECAY_LORA), (0, 0))).reshape(LANES, N_PAIRS, LANES)
    a2p = jnp.pad(a2, ((DECAY_LORA, LANES - DECAY_LORA - AAA_LORA), (0, 0))
                  ).reshape(LANES, N_PAIRS, LANES)
    wa2 = jnp.concatenate([w2p, a2p], axis=2).reshape(LANES, 2 * r_w)
    g2p = jnp.pad(g2, ((0, GATE_PAD - GATE_LORA), (0, 0)))

    x2 = x.reshape(batch * seq, D_MODEL)
    proj, lora = _in_proj(x2, norm_g[None, :], w_all, w_gate, w_lora, cos_t, sin_t,
                          mu_main, mu_lora, seq)
    proj3 = proj.reshape(batch, seq, PROJ_WIDTH)
    lora3 = lora.reshape(batch, seq, LORA_PAD)
    att = _attention(proj3)
    rwk = _rwkv(proj3, lora3, w0[None, :], a0[None, :],
                k_k[None, :], k_a[None, :], r_k.reshape(1, r_w), lnx_g[None, :],
                lnx_b[None, :], wa2, g2p)
    w_out_b = w_out.astype(BF16)
    return att.reshape(batch * seq, a_w), rwk.reshape(batch * seq, r_w), w_out_b, x2


def kernel(x, norm_g, w_in, shift_mu, w0, w2, a0, a2, g2, k_k, k_a, r_k,
           lnx_g, lnx_b, w_out, final_g):
    batch, seq, _ = x.shape
    depth = norm_g.shape[0]
    assert seq % SUPER == 0 and seq % IN_TM == 0 and (batch * seq) % OUT_TM == 0
    cos_t, sin_t = _rope_tables(seq)
    for l in range(depth):
        att2, rwk2, w_out_b, x2 = _layer(
            x, norm_g[l], w_in[l], shift_mu[l], w0[l], w2[l], a0[l], a2[l], g2[l],
            k_k[l], k_a[l], r_k[l], lnx_g[l], lnx_b[l], w_out[l], cos_t, sin_t)
        y2 = _out_proj(att2, rwk2, w_out_b, x2, final_g[None, :],
                       normalize=(l == depth - 1))
        x = y2.reshape(batch, seq, D_MODEL)
    return x
```

```python
import functools

import jax
import jax.numpy as jnp
from jax import lax
from jax.experimental import pallas as pl
from jax.experimental.pallas import tpu as pltpu

F32 = jnp.float32
BF16 = jnp.bfloat16

D_MODEL = 2048
HEAD_DIM = 64
ATT_WIDTH = 1024
RWKV_WIDTH = 1024
DIL_PATTERNS = ((128, 1), (512, 4), (2048, 16))
ROPE_THETA = 10000.0
DECAY_LORA = 64
AAA_LORA = 64
GATE_LORA = 160
NORM_EPS = 1e-5
LNX_EPS = 64e-5
DECAY_SCALE = 0.6065306597126334
KK_NORM_FLOOR = 1e-12
LOG2_E = 1.4426950408889634

LANES = 128
N_PAIRS = ATT_WIDTH // LANES
N_BACK = 128
Q_BLOCK = 128
SUPER = 2048
CHUNK = 64
RWKV_BLOCK = 512
N_CHUNKS = RWKV_BLOCK // CHUNK
LORA_PAD = 512
LORA_USED = 384
GATE_PAD = 256

COL_Q, COL_K, COL_V, COL_ZA = 0, 1024, 2048, 3072
COL_R, COL_RK, COL_RV, COL_ZR = 4096, 5120, 6144, 7168
PROJ_WIDTH = 8192

IN_TM, IN_TN = 1024, 1024
IN_SUB = 512
N_ROPE_STEPS = (COL_V - COL_Q) // IN_TN
N_Q_STEPS = (COL_K - COL_Q) // IN_TN
N_ALIGNED_STEPS = COL_ZR // IN_TN
N_SHIFT_LO = COL_R // IN_TN
OUT_TM = 512
VMEM_LIMIT = 56 * 1024 * 1024


def _dot(a, b):
    return jnp.dot(a, b, preferred_element_type=F32)


def _dot_nt(a, b):
    return lax.dot_general(a, b, (((1,), (1,)), ((), ())), preferred_element_type=F32)


def _sigmoid(x):
    return 0.5 * jnp.tanh(0.5 * x) + 0.5


def _bf16_pieces(x, terms):
    pieces = []
    for _ in range(terms):
        piece = x.astype(BF16)
        pieces.append(piece)
        x = x - piece.astype(F32)
    return pieces


def _head_sum(x, ones2):
    return _dot(jnp.concatenate(_bf16_pieces(x, 2), axis=1), ones2)


def _in_proj_kernel(x_ref, g_ref, w_ref, wz_ref, wl_ref, cos_ref, sin_ref,
                    mu_ref, mul_ref, o_ref, lora_ref, h_ref, carry_ref, carry_lora_ref,
                    *, pos_blocks):
    i = pl.program_id(0)
    j = pl.program_id(1)
    subs = [slice(s * IN_SUB, (s + 1) * IN_SUB) for s in range(IN_TN // IN_SUB)]
    seq_start = (i % pos_blocks) == 0
    last_row = slice(IN_TM - 1, IN_TM)

    def token_shift(u, carry_row, mu):
        row0 = lax.broadcasted_iota(jnp.int32, u.shape, 0) == 0
        prev = jnp.where(row0, jnp.where(seq_start, 0.0, carry_row), pltpu.roll(u, 1, 0))
        return u + (prev - u) * mu

    @pl.when((i == 0) & (j == 0))
    def _():
        carry_ref[...] = jnp.zeros_like(carry_ref)
        carry_lora_ref[...] = jnp.zeros_like(carry_lora_ref)

    @pl.when(j == 0)
    def _():
        x = x_ref[...]
        ms = jnp.mean(x * x, axis=-1, keepdims=True)
        h_ref[...] = (x * lax.rsqrt(ms + NORM_EPS) * g_ref[...]).astype(BF16)
        lora = _dot(h_ref[...], wl_ref[...])
        lora_ref[...] = token_shift(lora, carry_lora_ref[0:1, :], mul_ref[...])
        carry_lora_ref[0:1, :] = lora[last_row, :]

    @pl.when(j < N_ROPE_STEPS)
    def _():
        reps = IN_SUB // LANES
        cos = jnp.concatenate([cos_ref[...]] * reps, axis=1)
        sin = jnp.concatenate([sin_ref[...]] * reps, axis=1)
        lane = lax.broadcasted_iota(jnp.int32, (IN_TM, IN_SUB), 1)
        first_half = (lane % HEAD_DIM) < (HEAD_DIM // 2)
        scale = jnp.where(j < N_Q_STEPS, LOG2_E * HEAD_DIM ** -0.5, 1.0).astype(F32)
        for sub in subs:
            acc = _dot(h_ref[...], w_ref[:, sub])
            partner = jnp.where(first_half,
                                pltpu.roll(acc, IN_SUB - HEAD_DIM // 2, 1),
                                pltpu.roll(acc, HEAD_DIM // 2, 1))
            o_ref[:, sub] = (acc * cos + partner * sin) * scale

    @pl.when((j >= N_ROPE_STEPS) & (j < N_SHIFT_LO))
    def _():
        for sub in subs:
            o_ref[:, sub] = _dot(h_ref[...], w_ref[:, sub])

    @pl.when((j >= N_SHIFT_LO) & (j < N_ALIGNED_STEPS))
    def _():
        slot = j - N_SHIFT_LO
        for sub in subs:
            acc = _dot(h_ref[...], w_ref[:, sub])
            o_ref[:, sub] = token_shift(acc, carry_ref[slot, 0:1, sub], mu_ref[:, sub])
            carry_ref[slot, 0:1, sub] = acc[last_row, :]

    @pl.when(j >= N_ALIGNED_STEPS)
    def _():
        for sub in subs:
            o_ref[:, sub] = _dot(h_ref[...], wz_ref[:, sub])


def _in_proj(x2, g, w_all, w_gate, w_lora, cos_t, sin_t, mu_main, mu_lora, seq):
    tokens = x2.shape[0]
    pos_blocks = seq // IN_TM
    n_shift = N_ALIGNED_STEPS - N_SHIFT_LO

    def mu_index(i, j):
        return 0, jnp.clip(j - N_SHIFT_LO, 0, n_shift - 1)

    return pl.pallas_call(
        functools.partial(_in_proj_kernel, pos_blocks=pos_blocks),
        grid=(tokens // IN_TM, PROJ_WIDTH // IN_TN),
        in_specs=[
            pl.BlockSpec((IN_TM, D_MODEL), lambda i, j: (i, 0)),
            pl.BlockSpec((1, D_MODEL), lambda i, j: (0, 0)),
            pl.BlockSpec((D_MODEL, IN_TN), lambda i, j: (0, jnp.minimum(j, N_ALIGNED_STEPS - 1))),
            pl.BlockSpec((D_MODEL, IN_TN), lambda i, j: (0, 0)),
            pl.BlockSpec((D_MODEL, LORA_PAD), lambda i, j: (0, 0)),
            pl.BlockSpec((IN_TM, LANES), lambda i, j: (i % pos_blocks, 0)),
            pl.BlockSpec((IN_TM, LANES), lambda i, j: (i % pos_blocks, 0)),
            pl.BlockSpec((1, IN_TN), mu_index),
            pl.BlockSpec((1, LORA_PAD), lambda i, j: (0, 0)),
        ],
        out_specs=[
            pl.BlockSpec((IN_TM, IN_TN), lambda i, j: (i, j)),
            pl.BlockSpec((IN_TM, LORA_PAD), lambda i, j: (i, 0)),
        ],
        out_shape=[
            jax.ShapeDtypeStruct((tokens, PROJ_WIDTH), F32),
            jax.ShapeDtypeStruct((tokens, LORA_PAD), F32),
        ],
        scratch_shapes=[
            pltpu.VMEM((IN_TM, D_MODEL), BF16),
            pltpu.VMEM((n_shift, 8, IN_TN), F32),
            pltpu.VMEM((8, LORA_PAD), F32),
        ],
        compiler_params=pltpu.CompilerParams(
            dimension_semantics=("arbitrary", "arbitrary"),
            vmem_limit_bytes=VMEM_LIMIT),
        name="in_proj",
    )(x2, g, w_all, w_gate, w_lora, cos_t, sin_t, mu_main, mu_lora)


def _attention_kernel(q_ref, kc_ref, kp_ref, vc_ref, vp_ref, z_ref, o_ref,
                      k4, v4, acc_s, m_s, den_s):
    i = pl.program_id(2)
    slab = 2 * SUPER // 4
    for b in range(4):
        for dst, prev_ref, cur_ref in ((k4, kp_ref, kc_ref), (v4, vp_ref, vc_ref)):
            dst[b * slab:b * slab + slab // 2, :] = prev_ref[0, pl.ds(b, slab // 2, stride=4), :]
            dst[b * slab + slab // 2:(b + 1) * slab, :] = cur_ref[0, pl.ds(b, slab // 2,
                                                                          stride=4), :]

    qi = lax.broadcasted_iota(jnp.int32, (Q_BLOCK, 2 * Q_BLOCK), 0)
    ki = lax.broadcasted_iota(jnp.int32, (Q_BLOCK, 2 * Q_BLOCK), 1)
    rel = Q_BLOCK + qi - ki
    band = (rel >= 0) & (rel <= N_BACK)
    cur = ki >= Q_BLOCK
    lane = lax.broadcasted_iota(jnp.int32, (Q_BLOCK, LANES), 1)
    head0 = lane < HEAD_DIM

    def rows(start, size, stride):
        return pl.ds(start, size, stride=stride) if stride > 1 else pl.ds(start, size)

    def window(buf4, cur_ref, prev_ref, q_start, dil):
        k_start = SUPER + q_start - Q_BLOCK * dil
        if dil > 1:
            return buf4[rows((k_start % 4) * slab + k_start // 4, 2 * Q_BLOCK, dil // 4), :]
        if k_start >= SUPER:
            return cur_ref[0, pl.ds(k_start - SUPER, 2 * Q_BLOCK), :]
        return jnp.concatenate([prev_ref[0, pl.ds(k_start, Q_BLOCK), :],
                                cur_ref[0, pl.ds(q_start, Q_BLOCK), :]], axis=0)

    def block(pat, dil, q_start):
        rows_q = rows(q_start, Q_BLOCK, dil)
        qs = q_ref[0, rows_q, :]
        kw = window(k4, kc_ref, kp_ref, q_start, dil).astype(BF16)
        vw = window(v4, vc_ref, vp_ref, q_start, dil).astype(BF16)
        if q_start >= Q_BLOCK * dil:
            valid = band
        else:
            valid = band & (cur | (i > 0))
        accs, ms, dens = [], [], []
        for hm in (head0, ~head0):
            qh = jnp.where(hm, qs, 0.0).astype(BF16)
            s = _dot_nt(qh, kw)
            s = jnp.where(valid, s, -jnp.inf)
            m = jnp.max(s, axis=-1, keepdims=True)
            p = jnp.exp2(s - m)
            dens.append(jnp.sum(p, axis=-1, keepdims=True))
            ms.append(m)
            accs.append(_dot(p.astype(BF16), vw))
        acc_s[pat, rows_q, :] = jnp.where(head0, accs[0], accs[1])
        m_s[pat, rows_q, :] = jnp.where(head0, ms[0], ms[1])
        den_s[pat, rows_q, :] = jnp.where(head0, dens[0], dens[1])

    for pat, (window_len, dil) in enumerate(DIL_PATTERNS):
        span = Q_BLOCK * dil
        for s_idx in range(SUPER // span):
            for r in range(dil):
                block(pat, dil, s_idx * span + r)

    m0, m1, m2 = m_s[0], m_s[1], m_s[2]
    mx = jnp.maximum(jnp.maximum(m0, m1), m2)
    e0, e1, e2 = jnp.exp2(m0 - mx), jnp.exp2(m1 - mx), jnp.exp2(m2 - mx)
    num = e0 * acc_s[0] + e1 * acc_s[1] + e2 * acc_s[2]
    den = e0 * den_s[0] + e1 * den_s[1] + e2 * den_s[2]
    z = z_ref[0]
    o_ref[0] = (num / den * (z * _sigmoid(z))).astype(o_ref.dtype)


def _attention(proj3):
    batch, seq, _ = proj3.shape
    blk = (1, SUPER, LANES)

    def col(base):
        return lambda b, p, i: (b, i, base // LANES + p)

    def col_prev(base):
        return lambda b, p, i: (b, jnp.maximum(i - 1, 0), base // LANES + p)

    return pl.pallas_call(
        _attention_kernel,
        grid=(batch, N_PAIRS, seq // SUPER),
        in_specs=[
            pl.BlockSpec(blk, col(COL_Q)),
            pl.BlockSpec(blk, col(COL_K)),
            pl.BlockSpec(blk, col_prev(COL_K)),
            pl.BlockSpec(blk, col(COL_V)),
            pl.BlockSpec(blk, col_prev(COL_V)),
            pl.BlockSpec(blk, col(COL_ZA)),
        ],
        out_specs=pl.BlockSpec(blk, lambda b, p, i: (b, i, p)),
        out_shape=jax.ShapeDtypeStruct((batch, seq, ATT_WIDTH), BF16),
        scratch_shapes=[
            pltpu.VMEM((2 * SUPER, LANES), F32),
            pltpu.VMEM((2 * SUPER, LANES), F32),
            pltpu.VMEM((len(DIL_PATTERNS), SUPER, LANES), F32),
            pltpu.VMEM((len(DIL_PATTERNS), SUPER, LANES), F32),
            pltpu.VMEM((len(DIL_PATTERNS), SUPER, LANES), F32),
        ],
        compiler_params=pltpu.CompilerParams(
            dimension_semantics=("arbitrary", "arbitrary", "arbitrary"),
            vmem_limit_bytes=VMEM_LIMIT),
        name="dilated_attention",
    )(proj3, proj3, proj3, proj3, proj3, proj3)


def _rwkv_kernel(r_ref, k_ref, v_ref, z_ref, lora_ref,
                 w0_ref, a0_ref, kk_ref, ka_ref, wa2_ref, g2_ref,
                 rk_ref, lng_ref, lnb_ref,
                 o_ref,
                 state,
                 pm_a, pm_r, pm_kb, pm_bk, pm_d, pm_v,
                 pc_r, pc_k, pc_v, pc_gz,
                 mc_rw, mc_y0, mc_m, mc_g, mc_d,
                 *, blocks_per_stream):
    t = pl.program_id(0)
    first_chain = ((2 * jnp.maximum(t - 1, 0)) % blocks_per_stream) == 0

    @pl.when(t == 0)
    def _():
        for ref in (state, pm_a, pm_r, pm_kb, pm_bk, pm_d, pm_v,
                    pc_r, pc_k, pc_v, pc_gz, mc_rw, mc_y0, mc_m, mc_g, mc_d):
            ref[...] = jnp.zeros_like(ref)

    rr = lax.broadcasted_iota(jnp.int32, (LANES, LANES), 0)
    cc = lax.broadcasted_iota(jnp.int32, (LANES, LANES), 1)
    same_head = (rr // HEAD_DIM) == (cc // HEAD_DIM)
    strict = same_head & ((cc % CHUNK) < (rr % CHUNK))
    incl = same_head & ((cc % CHUNK) <= (rr % CHUNK))
    eye = (rr == cc).astype(F32)
    head_ones2 = ((lax.broadcasted_iota(jnp.int32, (2 * LANES, LANES), 0) % LANES) // HEAD_DIM
                  == lax.broadcasted_iota(jnp.int32, (2 * LANES, LANES), 1) // HEAD_DIM
                  ).astype(BF16)
    tril = (lax.broadcasted_iota(jnp.int32, (CHUNK, CHUNK), 1)
            <= lax.broadcasted_iota(jnp.int32, (CHUNK, CHUNK), 0)).astype(BF16)
    head0 = lax.broadcasted_iota(jnp.int32, (CHUNK, LANES), 1) < HEAD_DIM
    decay_lane = lax.broadcasted_iota(jnp.int32, (CHUNK, LANES), 1) < DECAY_LORA
    zero_blk = jnp.zeros((LANES, LANES), BF16)
    last = slice(CHUNK - 1, CHUNK)

    def bd(x):
        return jnp.concatenate([jnp.where(head0, x, 0.0), jnp.where(head0, 0.0, x)], axis=0)

    rows = [slice(c * CHUNK, (c + 1) * CHUNK) for c in range(N_CHUNKS)]
    cur = {"half": 0, "slot_p": 0, "slot_m": 1}

    def block_rows(c):
        start = cur["half"] * RWKV_BLOCK + c * CHUNK
        return slice(start, start + CHUNK)

    pv = [dict() for _ in range(N_CHUNKS)]

    def prep_lora_acts(c):
        lora = lora_ref[0, block_rows(c), 0:LORA_USED]
        wa = lora[:, 0:LANES]
        pv[c]["wa_act"] = jnp.where(decay_lane, jnp.tanh(wa), wa).astype(BF16)
        pv[c]["gate_act"] = _sigmoid(lora[:, LANES:LANES + GATE_PAD]).astype(BF16)

    def prep_lora_matmuls(c):
        d = pv[c]
        wa_out = _dot(d["wa_act"], wa2_ref[...].astype(BF16))
        d["lw"] = -(DECAY_SCALE * _sigmoid(w0_ref[...] + wa_out[:, 0:LANES]))
        d["a"] = _sigmoid(a0_ref[...] + wa_out[:, LANES:])
        g = _dot(d["gate_act"], g2_ref[...].astype(BF16))
        z = z_ref[0, block_rows(c), :]
        d["gz"] = g * (z * _sigmoid(z))

    def prep_key_norm(c):
        d = pv[c]
        k = k_ref[0, block_rows(c), :]
        kk = k * kk_ref[...]
        kk = kk * lax.rsqrt(jnp.maximum(_head_sum(kk * kk, head_ones2), KK_NORM_FLOOR ** 2))
        d["kk"] = kk
        d["k2"] = k * (1.0 + (d["a"] - 1.0) * ka_ref[...])
        d["b"] = kk * d["a"]
        parts = _dot(tril, jnp.concatenate(_bf16_pieces(d["lw"], 3), axis=1))
        d["cs"] = parts[:, 0:LANES] + parts[:, LANES:2 * LANES] + parts[:, 2 * LANES:]

    def prep_exp(c):
        d = pv[c]
        d["e_pos"] = jnp.exp(d["cs"])
        d["e_neg"] = jnp.exp(-d["cs"])
        d["e_excl"] = jnp.exp(d["cs"] - d["lw"])
        d["decay_end"] = d["e_pos"][last, :]

    def prep_store_ar(c):
        d = pv[c]
        idx = cur["slot_p"] * N_CHUNKS + c
        pm_a[idx] = bd(-d["kk"] * d["e_excl"]).astype(BF16)
        pm_r[idx] = bd(r_ref[0, block_rows(c), :] * d["e_pos"])

    def prep_store_kbv(c):
        d = pv[c]
        slot_p = cur["slot_p"]
        idx = slot_p * N_CHUNKS + c
        pm_kb[idx] = jnp.concatenate([d["k2"] * d["e_neg"], d["b"] * d["e_neg"]],
                                     axis=0).astype(BF16)
        pm_v[idx] = bd(v_ref[0, block_rows(c), :]).astype(BF16)

    def prep_store_pc(c):
        d = pv[c]
        slot_p = cur["slot_p"]
        pc_r[slot_p, rows[c], :] = r_ref[0, block_rows(c), :]
        pc_k[slot_p, rows[c], :] = d["k2"]
        pc_v[slot_p, rows[c], :] = v_ref[0, block_rows(c), :]
        pc_gz[slot_p, rows[c], :] = d["gz"]

    def prep_store_bk(c):
        d = pv[c]
        idx = cur["slot_p"] * N_CHUNKS + c
        e_end = d["decay_end"] * d["e_neg"]
        bkd_t = jnp.concatenate([bd(d["b"] * e_end), bd(d["k2"] * e_end),
                                 jnp.broadcast_to(d["decay_end"], (LANES, LANES))], axis=0).T
        pm_bk[idx] = bkd_t[:, 0:2 * LANES].astype(BF16)
        pm_d[idx] = bkd_t[:, 2 * LANES:]

    prep_groups = [(prep_lora_acts, 110), (prep_lora_matmuls, 110),
                   (prep_key_norm, 150), (prep_exp, 50), (prep_store_ar, 56),
                   (prep_store_kbv, 50), (prep_store_bk, 75)]

    mm = [dict() for _ in range(N_CHUNKS)]

    def mm_scores(c):
        idx = cur["slot_m"] * N_CHUNKS + c
        kb = pm_kb[idx]
        k_t, b_t = kb[0:CHUNK, :], kb[CHUNK:, :]
        lhs = jnp.concatenate([pm_a[idx], pm_r[idx].astype(BF16)], axis=0)
        x = _dot_nt(lhs, jnp.concatenate([k_t, k_t, b_t, b_t], axis=0))
        d = mm[c]
        d["a_ak"] = jnp.where(strict, x[0:LANES, 0:LANES], 0.0).astype(BF16)
        a_ab = jnp.where(strict, x[0:LANES, LANES:], 0.0)
        d["a_r"] = jnp.concatenate([jnp.where(incl, x[LANES:, 0:LANES], 0.0),
                                    jnp.where(incl, x[LANES:, LANES:], 0.0)],
                                   axis=1).astype(BF16)
        d["q"] = a_ab.astype(BF16)
        d["p"] = eye + a_ab

    def mm_square(c):
        d = mm[c]
        d["q"] = _dot(d["q"], d["q"]).astype(BF16)

    def mm_double(c):
        d = mm[c]
        qp = _dot(d["q"], jnp.concatenate([d["q"], d["p"].astype(BF16)], axis=1))
        d["q"] = qp[:, 0:LANES].astype(BF16)
        d["p"] = d["p"] + qp[:, LANES:]

    def mm_inverse(c):
        d = mm[c]
        d["t"] = (d["p"] + _dot(d["q"], d["p"].astype(BF16))).astype(BF16)

    def mm_akv(c):
        d = mm[c]
        d["ak_v"] = _dot(d["a_ak"], pm_v[cur["slot_m"] * N_CHUNKS + c]).astype(BF16)

    def mm_wu(c):
        d = mm[c]
        rhs = jnp.concatenate([pm_a[cur["slot_m"] * N_CHUNKS + c], d["ak_v"]], axis=1)
        d["wu"] = _dot(d["t"], rhs).astype(BF16)

    def mm_yw(c):
        d = mm[c]
        idx = cur["slot_m"] * N_CHUNKS + c
        wu = d["wu"]
        rhs = jnp.concatenate(
            [jnp.concatenate([pm_v[idx], zero_blk], axis=1),
             jnp.concatenate([wu[:, LANES:], wu[:, 0:LANES]], axis=1)], axis=0)
        yw = _dot(d["a_r"], rhs)
        mc_y0[idx] = yw[:, 0:LANES]
        mc_rw[idx] = (pm_r[idx] + yw[:, LANES:]).astype(BF16)

    def mm_mg(c):
        idx = cur["slot_m"] * N_CHUNKS + c
        rhs = jnp.concatenate([mm[c]["wu"],
                               jnp.concatenate([zero_blk, pm_v[idx]], axis=1)], axis=0)
        mg = _dot(pm_bk[idx], rhs)
        mc_m[idx] = mg[:, 0:LANES].astype(BF16)
        mc_g[idx] = mg[:, LANES:]
        mc_d[idx] = pm_d[idx]

    matmul_stages = [mm_scores, mm_square, mm_double, mm_double, mm_double, mm_double,
                     mm_inverse, mm_akv, mm_wu, mm_yw, mm_mg]

    chain = {"hs": jnp.where(first_chain, 0.0, state[...])}

    cv = [dict() for _ in range(N_CHUNKS)]

    def chain_core(c):
        idx = cur["slot_p"] * N_CHUNKS + c
        hs = chain["hs"]
        hs_b = hs.astype(BF16)
        y = _dot(mc_rw[idx], hs_b) + mc_y0[idx]
        chain["hs"] = hs * mc_d[idx] + _dot(mc_m[idx], hs_b) + mc_g[idx]
        cv[c]["y"] = y[0:CHUNK, :] + y[CHUNK:, :]

    def chain_mean(c):
        slot_p, rws, d = cur["slot_p"], rows[c], cv[c]
        d["yc"] = d["y"] - _head_sum(d["y"], head_ones2) * (1.0 / HEAD_DIM)
        d["bonus"] = _head_sum(pc_r[slot_p, rws, :] * pc_k[slot_p, rws, :] * rk_ref[...],
                               head_ones2) * pc_v[slot_p, rws, :]

    def chain_finish(c):
        slot_p, rws, d = cur["slot_p"], rows[c], cv[c]
        var = _head_sum(d["yc"] * d["yc"], head_ones2) * (1.0 / HEAD_DIM)
        yn = d["yc"] * lax.rsqrt(var + LNX_EPS) * lng_ref[...] + lnb_ref[...]
        o_ref[0, block_rows(c), :] = ((yn + d["bonus"]) * pc_gz[slot_p, rws, :]
                                      ).astype(o_ref.dtype)

    n_rounds = len(matmul_stages)
    assert n_rounds >= N_CHUNKS + 3
    n_slots = (n_rounds - 1) * N_CHUNKS
    total = float(N_CHUNKS * sum(w for _, w in prep_groups))
    prep_slots = [[] for _ in range(n_slots)]
    done = 0.0
    for fn, weight in prep_groups:
        for c in range(N_CHUNKS):
            prep_slots[min(int(done / total * n_slots), n_slots - 1)].append((fn, c))
            done += weight
    for half in (0, 1):
        cur.update(half=half, slot_p=half, slot_m=1 - half)
        for rnd, mm_stage in enumerate(matmul_stages):
            if rnd < N_CHUNKS:
                chain_core(rnd)
            if 1 <= rnd <= N_CHUNKS:
                chain_mean(rnd - 1)
            if 2 <= rnd <= N_CHUNKS + 1:
                chain_finish(rnd - 2)
            for c in range(N_CHUNKS):
                mm_stage(c)
                if rnd == n_rounds - 1:
                    prep_store_pc(c)
                else:
                    for fn, pc in prep_slots[rnd * N_CHUNKS + c]:
                        fn(pc)
    state[...] = chain["hs"]


def _rwkv(proj3, lora3, w0, a0, k_k, k_a, r_k, lnx_g, lnx_b, wa2, g2p):
    batch, seq, _ = proj3.shape
    bps = seq // RWKV_BLOCK
    assert bps % 2 == 0
    wps = bps // 2
    n_total = batch * N_PAIRS * wps
    blk = (1, 2 * RWKV_BLOCK, LANES)

    def split(n):
        return n // (N_PAIRS * wps), n % wps, (n // wps) % N_PAIRS

    def prep_n(s):
        return jnp.minimum(s, n_total - 1)

    def chain_n(s):
        return jnp.maximum(s - 1, 0)

    def col(base):
        def index(s):
            b, i, p = split(prep_n(s))
            return b, i, base // LANES + p
        return index

    def lora_index(s):
        b, i, _ = split(prep_n(s))
        return b, i, 0

    def out_index(s):
        return split(chain_n(s))

    vec_prep = pl.BlockSpec((1, LANES), lambda s: (0, split(prep_n(s))[2]))
    vec_chain = pl.BlockSpec((1, LANES), lambda s: (0, split(chain_n(s))[2]))
    n_slots = 2 * N_CHUNKS
    return pl.pallas_call(
        functools.partial(_rwkv_kernel, blocks_per_stream=bps),
        grid=(n_total + 1,),
        in_specs=[
            pl.BlockSpec(blk, col(COL_R)),
            pl.BlockSpec(blk, col(COL_RK)),
            pl.BlockSpec(blk, col(COL_RV)),
            pl.BlockSpec(blk, col(COL_ZR)),
            pl.BlockSpec((1, 2 * RWKV_BLOCK, LORA_PAD), lora_index),
            vec_prep, vec_prep, vec_prep, vec_prep,
            pl.BlockSpec((LANES, 2 * LANES), lambda s: (0, split(prep_n(s))[2])),
            pl.BlockSpec((GATE_PAD, LANES), lambda s: (0, split(prep_n(s))[2])),
            vec_chain, vec_chain, vec_chain,
        ],
        out_specs=pl.BlockSpec(blk, out_index),
        out_shape=jax.ShapeDtypeStruct((batch, seq, RWKV_WIDTH), BF16),
        scratch_shapes=[
            pltpu.VMEM((LANES, LANES), F32),
            pltpu.VMEM((n_slots, LANES, LANES), BF16),
            pltpu.VMEM((n_slots, LANES, LANES), F32),
            pltpu.VMEM((n_slots, LANES, LANES), BF16),
            pltpu.VMEM((n_slots, LANES, 2 * LANES), BF16),
            pltpu.VMEM((n_slots, LANES, LANES), F32),
            pltpu.VMEM((n_slots, LANES, LANES), BF16),
            pltpu.VMEM((2, RWKV_BLOCK, LANES), F32),
            pltpu.VMEM((2, RWKV_BLOCK, LANES), F32),
            pltpu.VMEM((2, RWKV_BLOCK, LANES), F32),
            pltpu.VMEM((2, RWKV_BLOCK, LANES), F32),
            pltpu.VMEM((n_slots, LANES, LANES), BF16),
            pltpu.VMEM((n_slots, LANES, LANES), F32),
            pltpu.VMEM((n_slots, LANES, LANES), BF16),
            pltpu.VMEM((n_slots, LANES, LANES), F32),
            pltpu.VMEM((n_slots, LANES, LANES), F32),
        ],
        compiler_params=pltpu.CompilerParams(
            dimension_semantics=("arbitrary",),
            vmem_limit_bytes=VMEM_LIMIT),
        name="rwkv7_time_mix",
    )(proj3, proj3, proj3, proj3, lora3, w0, a0, k_k, k_a, wa2, g2p, r_k, lnx_g, lnx_b)


def _out_proj_kernel(att_ref, rwk_ref, wa_ref, wr_ref, x_ref, g_ref, o_ref, *, normalize):
    y = x_ref[...] + _dot(att_ref[...], wa_ref[...]) + _dot(rwk_ref[...], wr_ref[...])
    if normalize:
        ms = jnp.mean(y * y, axis=-1, keepdims=True)
        y = y * lax.rsqrt(ms + NORM_EPS) * g_ref[...]
    o_ref[...] = y


def _out_proj(att2, rwk2, w_out_b, x2, g, normalize):
    tokens = x2.shape[0]
    return pl.pallas_call(
        functools.partial(_out_proj_kernel, normalize=normalize),
        grid=(tokens // OUT_TM,),
        in_specs=[
            pl.BlockSpec((OUT_TM, ATT_WIDTH), lambda i: (i, 0)),
            pl.BlockSpec((OUT_TM, RWKV_WIDTH), lambda i: (i, 0)),
            pl.BlockSpec((ATT_WIDTH, D_MODEL), lambda i: (0, 0)),
            pl.BlockSpec((RWKV_WIDTH, D_MODEL), lambda i: (ATT_WIDTH // RWKV_WIDTH, 0)),
            pl.BlockSpec((OUT_TM, D_MODEL), lambda i: (i, 0)),
            pl.BlockSpec((1, D_MODEL), lambda i: (0, 0)),
        ],
        out_specs=pl.BlockSpec((OUT_TM, D_MODEL), lambda i: (i, 0)),
        out_shape=jax.ShapeDtypeStruct((tokens, D_MODEL), F32),
        compiler_params=pltpu.CompilerParams(
            dimension_semantics=("arbitrary",),
            vmem_limit_bytes=VMEM_LIMIT),
        name="out_proj",
    )(att2, rwk2, w_out_b, w_out_b, x2, g)


def _rope_tables(seq):
    inv_freq = ROPE_THETA ** (-jnp.arange(0, HEAD_DIM, 2, dtype=F32) / HEAD_DIM)
    ang = jnp.arange(seq, dtype=jnp.int32).astype(F32)[:, None] * inv_freq[None, :]
    cos, sin = jnp.cos(ang), jnp.sin(ang)
    reps = LANES // HEAD_DIM
    cos_t = jnp.tile(jnp.concatenate([cos, cos], axis=1), (1, reps))
    sin_t = jnp.tile(jnp.concatenate([-sin, sin], axis=1), (1, reps))
    return cos_t, sin_t


def _layer(x, norm_g, w_in, shift_mu, w0, w2, a0, a2, g2, k_k, k_a, r_k,
           lnx_g, lnx_b, w_out, cos_t, sin_t):
    batch, seq, _ = x.shape
    a_w, r_w = ATT_WIDTH, RWKV_WIDTH
    lo = 4 * a_w + 3 * r_w
    n_lora = DECAY_LORA + AAA_LORA + GATE_LORA
    assert lo == COL_ZR
    w_all = w_in.astype(BF16)
    w_gate = w_all[:, lo + n_lora:]
    w_lora = jnp.pad(w_all[:, lo:lo + n_lora], ((0, 0), (0, LORA_PAD - n_lora)))
    mu_main = shift_mu[None, 0:3 * r_w]
    mu_lora = jnp.pad(shift_mu[None, 3 * r_w:], ((0, 0), (0, LORA_PAD - n_lora)))
    w2p = jnp.pad(w2, ((0, LANES - DECAY_LORA), (0, 0))).reshape(LANES, N_PAIRS, LANES)
    a2p = jnp.pad(a2, ((DECAY_LORA, LANES - DECAY_LORA - AAA_LORA), (0, 0))
                  ).reshape(LANES, N_PAIRS, LANES)
    wa2 = jnp.concatenate([w2p, a2p], axis=2).reshape(LANES, 2 * r_w)
    g2p = jnp.pad(g2, ((0, GATE_PAD - GATE_LORA), (0, 0)))

    x2 = x.reshape(batch * seq, D_MODEL)
    proj, lora = _in_proj(x2, norm_g[None, :], w_all, w_gate, w_lora, cos_t, sin_t,
                          mu_main, mu_lora, seq)
    proj3 = proj.reshape(batch, seq, PROJ_WIDTH)
    lora3 = lora.reshape(batch, seq, LORA_PAD)
    att = _attention(proj3)
    rwk = _rwkv(proj3, lora3, w0[None, :], a0[None, :],
                k_k[None, :], k_a[None, :], r_k.reshape(1, r_w), lnx_g[None, :],
                lnx_b[None, :], wa2, g2p)
    w_out_b = w_out.astype(BF16)
    return att.reshape(batch * seq, a_w), rwk.reshape(batch * seq, r_w), w_out_b, x2


def kernel(x, norm_g, w_in, shift_mu, w0, w2, a0, a2, g2, k_k, k_a, r_k,
           lnx_g, lnx_b, w_out, final_g):
    batch, seq, _ = x.shape
    depth = norm_g.shape[0]
    assert seq % SUPER == 0 and seq % IN_TM == 0 and (batch * seq) % OUT_TM == 0
    cos_t, sin_t = _rope_tables(seq)
    for l in range(depth):
        att2, rwk2, w_out_b, x2 = _layer(
            x, norm_g[l], w_in[l], shift_mu[l], w0[l], w2[l], a0[l], a2[l], g2[l],
            k_k[l], k_a[l], r_k[l], lnx_g[l], lnx_b[l], w_out[l], cos_t, sin_t)
        y2 = _out_proj(att2, rwk2, w_out_b, x2, final_g[None, :],
                       normalize=(l == depth - 1))
        x = y2.reshape(batch, seq, D_MODEL)
    return x
```

```python
import functools

import jax
import jax.numpy as jnp
from jax import lax
from jax.experimental import pallas as pl
from jax.experimental.pallas import tpu as pltpu

F32 = jnp.float32
BF16 = jnp.bfloat16

D_MODEL = 2048
HEAD_DIM = 64
ATT_WIDTH = 1024
RWKV_WIDTH = 1024
DIL_PATTERNS = ((128, 1), (512, 4), (2048, 16))
ROPE_THETA = 10000.0
DECAY_LORA = 64
AAA_LORA = 64
GATE_LORA = 160
NORM_EPS = 1e-5
LNX_EPS = 64e-5
DECAY_SCALE = 0.6065306597126334
KK_NORM_FLOOR = 1e-12
LOG2_E = 1.4426950408889634

LANES = 128
SUBLANES = 8
N_PAIRS = ATT_WIDTH // LANES
N_BACK = 128
Q_BLOCK = 128
SUPER = 2048
CHUNK = 64
RWKV_BLOCK = 512
N_CHUNKS = RWKV_BLOCK // CHUNK
LORA_PAD = 512
LORA_USED = 384
GATE_PAD = 256

COL_Q, COL_K, COL_V, COL_ZA = 0, 1024, 2048, 3072
COL_R, COL_RK, COL_RV, COL_ZR = 4096, 5120, 6144, 7168
PROJ_WIDTH = 8192

IN_TM, IN_TN = 1024, 1024
IN_SUB = 512
N_ROPE_STEPS = (COL_V - COL_Q) // IN_TN
N_Q_STEPS = (COL_K - COL_Q) // IN_TN
N_ALIGNED_STEPS = COL_ZR // IN_TN
N_SHIFT_LO = COL_R // IN_TN
OUT_TM = 512
VMEM_LIMIT = 56 * 1024 * 1024


def _dot(a, b):
    return jnp.dot(a, b, preferred_element_type=F32)


def _dot_nt(a, b):
    return lax.dot_general(a, b, (((1,), (1,)), ((), ())), preferred_element_type=F32)


def _sigmoid(x):
    return 0.5 * jnp.tanh(0.5 * x) + 0.5


def _bf16_pieces(x, terms):
    pieces = []
    for _ in range(terms):
        piece = x.astype(BF16)
        pieces.append(piece)
        x = x - piece.astype(F32)
    return pieces


def _head_sum(x, head0):
    first = jnp.sum(jnp.where(head0, x, 0.0), axis=-1, keepdims=True)
    second = jnp.sum(jnp.where(head0, 0.0, x), axis=-1, keepdims=True)
    return jnp.where(head0, first, second)


def _in_proj_kernel(x_ref, g_ref, w_ref, wz_ref, wl_ref, cos_ref, sin_ref,
                    mu_ref, mul_ref, o_ref, lora_ref, h_ref, carry_ref, carry_lora_ref,
                    *, pos_blocks):
    i = pl.program_id(0)
    j = pl.program_id(1)
    subs = [slice(s * IN_SUB, (s + 1) * IN_SUB) for s in range(IN_TN // IN_SUB)]
    seq_start = (i % pos_blocks) == 0
    last_row = slice(IN_TM - 1, IN_TM)

    def token_shift(u, carry_row, mu):
        row0 = lax.broadcasted_iota(jnp.int32, u.shape, 0) == 0
        prev = jnp.where(row0, jnp.where(seq_start, 0.0, carry_row), pltpu.roll(u, 1, 0))
        return u + (prev - u) * mu

    @pl.when((i == 0) & (j == 0))
    def _():
        carry_ref[...] = jnp.zeros_like(carry_ref)
        carry_lora_ref[...] = jnp.zeros_like(carry_lora_ref)

    @pl.when(j == 0)
    def _():
        x = x_ref[...]
        ms = jnp.mean(x * x, axis=-1, keepdims=True)
        h_ref[...] = (x * lax.rsqrt(ms + NORM_EPS) * g_ref[...]).astype(BF16)
        lora = _dot(h_ref[...], wl_ref[...])
        lora_ref[...] = token_shift(lora, carry_lora_ref[0:1, :], mul_ref[...])
        carry_lora_ref[0:1, :] = lora[last_row, :]

    @pl.when(j < N_ROPE_STEPS)
    def _():
        reps = IN_SUB // LANES
        cos = jnp.concatenate([cos_ref[...]] * reps, axis=1)
        sin = jnp.concatenate([sin_ref[...]] * reps, axis=1)
        lane = lax.broadcasted_iota(jnp.int32, (IN_TM, IN_SUB), 1)
        first_half = (lane % HEAD_DIM) < (HEAD_DIM // 2)
        scale = jnp.where(j < N_Q_STEPS, LOG2_E * HEAD_DIM ** -0.5, 1.0).astype(F32)
        for sub in subs:
            acc = _dot(h_ref[...], w_ref[:, sub])
            partner = jnp.where(first_half,
                                pltpu.roll(acc, IN_SUB - HEAD_DIM // 2, 1),
                                pltpu.roll(acc, HEAD_DIM // 2, 1))
            o_ref[:, sub] = (acc * cos + partner * sin) * scale

    @pl.when((j >= N_ROPE_STEPS) & (j < N_SHIFT_LO))
    def _():
        for sub in subs:
            o_ref[:, sub] = _dot(h_ref[...], w_ref[:, sub])

    @pl.when((j >= N_SHIFT_LO) & (j < N_ALIGNED_STEPS))
    def _():
        slot = j - N_SHIFT_LO
        for sub in subs:
            acc = _dot(h_ref[...], w_ref[:, sub])
            o_ref[:, sub] = token_shift(acc, carry_ref[slot, 0:1, sub], mu_ref[:, sub])
            carry_ref[slot, 0:1, sub] = acc[last_row, :]

    @pl.when(j >= N_ALIGNED_STEPS)
    def _():
        for sub in subs:
            o_ref[:, sub] = _dot(h_ref[...], wz_ref[:, sub])


def _in_proj(x2, g, w_all, w_gate, w_lora, cos_t, sin_t, mu_main, mu_lora, seq):
    tokens = x2.shape[0]
    pos_blocks = seq // IN_TM
    n_shift = N_ALIGNED_STEPS - N_SHIFT_LO

    def mu_index(i, j):
        return 0, jnp.clip(j - N_SHIFT_LO, 0, n_shift - 1)

    return pl.pallas_call(
        functools.partial(_in_proj_kernel, pos_blocks=pos_blocks),
        grid=(tokens // IN_TM, PROJ_WIDTH // IN_TN),
        in_specs=[
            pl.BlockSpec((IN_TM, D_MODEL), lambda i, j: (i, 0)),
            pl.BlockSpec((1, D_MODEL), lambda i, j: (0, 0)),
            pl.BlockSpec((D_MODEL, IN_TN), lambda i, j: (0, jnp.minimum(j, N_ALIGNED_STEPS - 1))),
            pl.BlockSpec((D_MODEL, IN_TN), lambda i, j: (0, 0)),
            pl.BlockSpec((D_MODEL, LORA_PAD), lambda i, j: (0, 0)),
            pl.BlockSpec((IN_TM, LANES), lambda i, j: (i % pos_blocks, 0)),
            pl.BlockSpec((IN_TM, LANES), lambda i, j: (i % pos_blocks, 0)),
            pl.BlockSpec((1, IN_TN), mu_index),
            pl.BlockSpec((1, LORA_PAD), lambda i, j: (0, 0)),
        ],
        out_specs=[
            pl.BlockSpec((IN_TM, IN_TN), lambda i, j: (i, j)),
            pl.BlockSpec((IN_TM, LORA_PAD), lambda i, j: (i, 0)),
        ],
        out_shape=[
            jax.ShapeDtypeStruct((tokens, PROJ_WIDTH), F32),
            jax.ShapeDtypeStruct((tokens, LORA_PAD), F32),
        ],
        scratch_shapes=[
            pltpu.VMEM((IN_TM, D_MODEL), BF16),
            pltpu.VMEM((n_shift, SUBLANES, IN_TN), F32),
            pltpu.VMEM((SUBLANES, LORA_PAD), F32),
        ],
        compiler_params=pltpu.CompilerParams(
            dimension_semantics=("arbitrary", "arbitrary"),
            vmem_limit_bytes=VMEM_LIMIT),
        name="in_proj",
    )(x2, g, w_all, w_gate, w_lora, cos_t, sin_t, mu_main, mu_lora)


def _attention_kernel(q_ref, kc_ref, kp_ref, vc_ref, vp_ref, z_ref, o_ref,
                      k4, v4, acc_s, m_s, den_s):
    i = pl.program_id(2)
    slab = 2 * SUPER // 4
    for b in range(4):
        for dst, prev_ref, cur_ref in ((k4, kp_ref, kc_ref), (v4, vp_ref, vc_ref)):
            dst[b * slab:b * slab + slab // 2, :] = prev_ref[0, pl.ds(b, slab // 2, stride=4), :]
            dst[b * slab + slab // 2:(b + 1) * slab, :] = cur_ref[0, pl.ds(b, slab // 2,
                                                                          stride=4), :]

    qi = lax.broadcasted_iota(jnp.int32, (Q_BLOCK, 2 * Q_BLOCK), 0)
    ki = lax.broadcasted_iota(jnp.int32, (Q_BLOCK, 2 * Q_BLOCK), 1)
    rel = Q_BLOCK + qi - ki
    band = (rel >= 0) & (rel <= N_BACK)
    cur = ki >= Q_BLOCK
    lane = lax.broadcasted_iota(jnp.int32, (Q_BLOCK, LANES), 1)
    head0 = lane < HEAD_DIM

    def rows(start, size, stride):
        return pl.ds(start, size, stride=stride) if stride > 1 else pl.ds(start, size)

    def window(buf4, cur_ref, prev_ref, q_start, dil):
        k_start = SUPER + q_start - Q_BLOCK * dil
        if dil > 1:
            return buf4[rows((k_start % 4) * slab + k_start // 4, 2 * Q_BLOCK, dil // 4), :]
        if k_start >= SUPER:
            return cur_ref[0, pl.ds(k_start - SUPER, 2 * Q_BLOCK), :]
        return jnp.concatenate([prev_ref[0, pl.ds(k_start, Q_BLOCK), :],
                                cur_ref[0, pl.ds(q_start, Q_BLOCK), :]], axis=0)

    def block(pat, dil, q_start):
        rows_q = rows(q_start, Q_BLOCK, dil)
        qs = q_ref[0, rows_q, :]
        kw = window(k4, kc_ref, kp_ref, q_start, dil).astype(BF16)
        vw = window(v4, vc_ref, vp_ref, q_start, dil).astype(BF16)
        if q_start >= Q_BLOCK * dil:
            valid = band
        else:
            valid = band & (cur | (i > 0))
        accs, ms, dens = [], [], []
        for hm in (head0, ~head0):
            qh = jnp.where(hm, qs, 0.0).astype(BF16)
            s = _dot_nt(qh, kw)
            s = jnp.where(valid, s, -jnp.inf)
            m = jnp.max(s, axis=-1, keepdims=True)
            p = jnp.exp2(s - m)
            dens.append(jnp.sum(p, axis=-1, keepdims=True))
            ms.append(m)
            accs.append(_dot(p.astype(BF16), vw))
        acc_s[pat, rows_q, :] = jnp.where(head0, accs[0], accs[1])
        m_s[pat, rows_q, :] = jnp.where(head0, ms[0], ms[1])
        den_s[pat, rows_q, :] = jnp.where(head0, dens[0], dens[1])

    for pat, (window_len, dil) in enumerate(DIL_PATTERNS):
        span = Q_BLOCK * dil
        for s_idx in range(SUPER // span):
            for r in range(dil):
                block(pat, dil, s_idx * span + r)

    m0, m1, m2 = m_s[0], m_s[1], m_s[2]
    mx = jnp.maximum(jnp.maximum(m0, m1), m2)
    e0, e1, e2 = jnp.exp2(m0 - mx), jnp.exp2(m1 - mx), jnp.exp2(m2 - mx)
    num = e0 * acc_s[0] + e1 * acc_s[1] + e2 * acc_s[2]
    den = e0 * den_s[0] + e1 * den_s[1] + e2 * den_s[2]
    z = z_ref[0]
    o_ref[0] = (num / den * (z * _sigmoid(z))).astype(o_ref.dtype)


def _attention(proj3):
    batch, seq, _ = proj3.shape
    blk = (1, SUPER, LANES)

    def col(base):
        return lambda b, p, i: (b, i, base // LANES + p)

    def col_prev(base):
        return lambda b, p, i: (b, jnp.maximum(i - 1, 0), base // LANES + p)

    return pl.pallas_call(
        _attention_kernel,
        grid=(batch, N_PAIRS, seq // SUPER),
        in_specs=[
            pl.BlockSpec(blk, col(COL_Q)),
            pl.BlockSpec(blk, col(COL_K)),
            pl.BlockSpec(blk, col_prev(COL_K)),
            pl.BlockSpec(blk, col(COL_V)),
            pl.BlockSpec(blk, col_prev(COL_V)),
            pl.BlockSpec(blk, col(COL_ZA)),
        ],
        out_specs=pl.BlockSpec(blk, lambda b, p, i: (b, i, p)),
        out_shape=jax.ShapeDtypeStruct((batch, seq, ATT_WIDTH), BF16),
        scratch_shapes=[
            pltpu.VMEM((2 * SUPER, LANES), F32),
            pltpu.VMEM((2 * SUPER, LANES), F32),
            pltpu.VMEM((len(DIL_PATTERNS), SUPER, LANES), F32),
            pltpu.VMEM((len(DIL_PATTERNS), SUPER, LANES), F32),
            pltpu.VMEM((len(DIL_PATTERNS), SUPER, LANES), F32),
        ],
        compiler_params=pltpu.CompilerParams(
            dimension_semantics=("arbitrary", "arbitrary", "arbitrary"),
            vmem_limit_bytes=VMEM_LIMIT),
        name="dilated_attention",
    )(proj3, proj3, proj3, proj3, proj3, proj3)


def _rwkv_kernel(r_ref, k_ref, v_ref, z_ref, lora_ref,
                 w0_ref, a0_ref, kk_ref, ka_ref, wa2_ref, g2_ref,
                 rk_ref, lng_ref, lnb_ref,
                 o_ref,
                 state,
                 pm_a, pm_r, pm_kb, pm_bk, pm_d, pm_v,
                 pc_r, pc_k, pc_v, pc_gz,
                 mc_rw, mc_y0, mc_m, mc_g, mc_d,
                 *, blocks_per_stream):
    t = pl.program_id(0)
    first_chain = ((2 * jnp.maximum(t - 1, 0)) % blocks_per_stream) == 0

    @pl.when(t == 0)
    def _():
        for ref in (state, pm_a, pm_r, pm_kb, pm_bk, pm_d, pm_v,
                    pc_r, pc_k, pc_v, pc_gz, mc_rw, mc_y0, mc_m, mc_g, mc_d):
            ref[...] = jnp.zeros_like(ref)

    rr = lax.broadcasted_iota(jnp.int32, (LANES, LANES), 0)
    cc = lax.broadcasted_iota(jnp.int32, (LANES, LANES), 1)
    same_head = (rr // HEAD_DIM) == (cc // HEAD_DIM)
    strict = same_head & ((cc % CHUNK) < (rr % CHUNK))
    incl = same_head & ((cc % CHUNK) <= (rr % CHUNK))
    eye = (rr == cc).astype(F32)
    tril = (lax.broadcasted_iota(jnp.int32, (CHUNK, CHUNK), 1)
            <= lax.broadcasted_iota(jnp.int32, (CHUNK, CHUNK), 0)).astype(BF16)
    head0 = lax.broadcasted_iota(jnp.int32, (CHUNK, LANES), 1) < HEAD_DIM
    decay_lane = lax.broadcasted_iota(jnp.int32, (CHUNK, LANES), 1) < DECAY_LORA
    zero_blk = jnp.zeros((LANES, LANES), BF16)
    last = slice(CHUNK - 1, CHUNK)

    def bd(x):
        return jnp.concatenate([jnp.where(head0, x, 0.0), jnp.where(head0, 0.0, x)], axis=0)

    rows = [slice(c * CHUNK, (c + 1) * CHUNK) for c in range(N_CHUNKS)]
    cur = {"half": 0, "slot_p": 0, "slot_m": 1}

    def block_rows(c):
        start = cur["half"] * RWKV_BLOCK + c * CHUNK
        return slice(start, start + CHUNK)

    pv = [dict() for _ in range(N_CHUNKS)]

    def prep_lora_acts(c):
        lora = lora_ref[0, block_rows(c), 0:LORA_USED]
        wa = lora[:, 0:LANES]
        pv[c]["wa_act"] = jnp.where(decay_lane, jnp.tanh(wa), wa).astype(BF16)
        pv[c]["gate_act"] = _sigmoid(lora[:, LANES:LANES + GATE_PAD]).astype(BF16)

    def prep_lora_matmuls(c):
        d = pv[c]
        wa_out = _dot(d["wa_act"], wa2_ref[...].astype(BF16))
        d["lw"] = -(DECAY_SCALE * _sigmoid(w0_ref[...] + wa_out[:, 0:LANES]))
        d["a"] = _sigmoid(a0_ref[...] + wa_out[:, LANES:])
        g = _dot(d["gate_act"], g2_ref[...].astype(BF16))
        z = z_ref[0, block_rows(c), :]
        d["gz"] = g * (z * _sigmoid(z))

    def prep_key_norm(c):
        d = pv[c]
        k = k_ref[0, block_rows(c), :]
        kk = k * kk_ref[...]
        kk = kk * lax.rsqrt(jnp.maximum(_head_sum(kk * kk, head0), KK_NORM_FLOOR ** 2))
        d["kk"] = kk
        d["k2"] = k * (1.0 + (d["a"] - 1.0) * ka_ref[...])
        d["b"] = kk * d["a"]
        parts = _dot(tril, jnp.concatenate(_bf16_pieces(d["lw"], 3), axis=1))
        d["cs"] = parts[:, 0:LANES] + parts[:, LANES:2 * LANES] + parts[:, 2 * LANES:]

    def prep_exp(c):
        d = pv[c]
        d["e_pos"] = jnp.exp(d["cs"])
        d["e_neg"] = jnp.exp(-d["cs"])
        d["e_excl"] = jnp.exp(d["cs"] - d["lw"])
        d["decay_end"] = d["e_pos"][last, :]

    def prep_store_ar(c):
        d = pv[c]
        idx = cur["slot_p"] * N_CHUNKS + c
        pm_a[idx] = bd(-d["kk"] * d["e_excl"]).astype(BF16)
        pm_r[idx] = bd(r_ref[0, block_rows(c), :] * d["e_pos"])

    def prep_store_kbv(c):
        d = pv[c]
        slot_p = cur["slot_p"]
        idx = slot_p * N_CHUNKS + c
        pm_kb[idx] = jnp.concatenate([d["k2"] * d["e_neg"], d["b"] * d["e_neg"]],
                                     axis=0).astype(BF16)
        pm_v[idx] = bd(v_ref[0, block_rows(c), :]).astype(BF16)

    def prep_store_pc(c):
        d = pv[c]
        slot_p = cur["slot_p"]
        pc_r[slot_p, rows[c], :] = r_ref[0, block_rows(c), :]
        pc_k[slot_p, rows[c], :] = d["k2"]
        pc_v[slot_p, rows[c], :] = v_ref[0, block_rows(c), :]
        pc_gz[slot_p, rows[c], :] = d["gz"]

    def prep_store_bk(c):
        d = pv[c]
        idx = cur["slot_p"] * N_CHUNKS + c
        e_end = d["decay_end"] * d["e_neg"]
        bkd_t = jnp.concatenate([bd(d["b"] * e_end), bd(d["k2"] * e_end),
                                 jnp.broadcast_to(d["decay_end"], (LANES, LANES))], axis=0).T
        pm_bk[idx] = bkd_t[:, 0:2 * LANES].astype(BF16)
        pm_d[idx] = bkd_t[:, 2 * LANES:]

    prep_groups = [(prep_lora_acts, 110), (prep_lora_matmuls, 110),
                   (prep_key_norm, 150), (prep_exp, 50), (prep_store_ar, 56),
                   (prep_store_kbv, 50), (prep_store_bk, 75)]

    mm = [dict() for _ in range(N_CHUNKS)]

    def mm_scores(c):
        idx = cur["slot_m"] * N_CHUNKS + c
        kb = pm_kb[idx]
        k_t, b_t = kb[0:CHUNK, :], kb[CHUNK:, :]
        lhs = jnp.concatenate([pm_a[idx], pm_r[idx].astype(BF16)], axis=0)
        x = _dot_nt(lhs, jnp.concatenate([k_t, k_t, b_t, b_t], axis=0))
        d = mm[c]
        d["a_ak"] = jnp.where(strict, x[0:LANES, 0:LANES], 0.0).astype(BF16)
        a_ab = jnp.where(strict, x[0:LANES, LANES:], 0.0)
        d["a_r"] = jnp.concatenate([jnp.where(incl, x[LANES:, 0:LANES], 0.0),
                                    jnp.where(incl, x[LANES:, LANES:], 0.0)],
                                   axis=1).astype(BF16)
        d["q"] = a_ab.astype(BF16)
        d["p"] = eye + a_ab

    def mm_square(c):
        d = mm[c]
        d["q"] = _dot(d["q"], d["q"]).astype(BF16)

    def mm_double(c):
        d = mm[c]
        qp = _dot(d["q"], jnp.concatenate([d["q"], d["p"].astype(BF16)], axis=1))
        d["q"] = qp[:, 0:LANES].astype(BF16)
        d["p"] = d["p"] + qp[:, LANES:]

    def mm_inverse(c):
        d = mm[c]
        d["t"] = (d["p"] + _dot(d["q"], d["p"].astype(BF16))).astype(BF16)

    def mm_akv(c):
        d = mm[c]
        d["ak_v"] = _dot(d["a_ak"], pm_v[cur["slot_m"] * N_CHUNKS + c]).astype(BF16)

    def mm_wu(c):
        d = mm[c]
        rhs = jnp.concatenate([pm_a[cur["slot_m"] * N_CHUNKS + c], d["ak_v"]], axis=1)
        d["wu"] = _dot(d["t"], rhs).astype(BF16)

    def mm_yw(c):
        d = mm[c]
        idx = cur["slot_m"] * N_CHUNKS + c
        wu = d["wu"]
        rhs = jnp.concatenate(
            [jnp.concatenate([pm_v[idx], zero_blk], axis=1),
             jnp.concatenate([wu[:, LANES:], wu[:, 0:LANES]], axis=1)], axis=0)
        yw = _dot(d["a_r"], rhs)
        mc_y0[idx] = yw[:, 0:LANES]
        mc_rw[idx] = (pm_r[idx] + yw[:, LANES:]).astype(BF16)

    def mm_mg(c):
        idx = cur["slot_m"] * N_CHUNKS + c
        rhs = jnp.concatenate([mm[c]["wu"],
                               jnp.concatenate([zero_blk, pm_v[idx]], axis=1)], axis=0)
        mg = _dot(pm_bk[idx], rhs)
        mc_m[idx] = mg[:, 0:LANES].astype(BF16)
        mc_g[idx] = mg[:, LANES:]
        mc_d[idx] = pm_d[idx]

    matmul_stages = [mm_scores, mm_square, mm_double, mm_double, mm_double, mm_double,
                     mm_inverse, mm_akv, mm_wu, mm_yw, mm_mg]

    chain = {"hs": jnp.where(first_chain, 0.0, state[...])}

    cv = [dict() for _ in range(N_CHUNKS)]

    def chain_core(c):
        idx = cur["slot_p"] * N_CHUNKS + c
        hs = chain["hs"]
        hs_b = hs.astype(BF16)
        y = _dot(mc_rw[idx], hs_b) + mc_y0[idx]
        chain["hs"] = hs * mc_d[idx] + _dot(mc_m[idx], hs_b) + mc_g[idx]
        cv[c]["y"] = y[0:CHUNK, :] + y[CHUNK:, :]

    def chain_mean(c):
        slot_p, rws, d = cur["slot_p"], rows[c], cv[c]
        d["yc"] = d["y"] - _head_sum(d["y"], head0) * (1.0 / HEAD_DIM)
        d["bonus"] = _head_sum(pc_r[slot_p, rws, :] * pc_k[slot_p, rws, :] * rk_ref[...],
                               head0) * pc_v[slot_p, rws, :]

    def chain_finish(c):
        slot_p, rws, d = cur["slot_p"], rows[c], cv[c]
        var = _head_sum(d["yc"] * d["yc"], head0) * (1.0 / HEAD_DIM)
        yn = d["yc"] * lax.rsqrt(var + LNX_EPS) * lng_ref[...] + lnb_ref[...]
        o_ref[0, block_rows(c), :] = ((yn + d["bonus"]) * pc_gz[slot_p, rws, :]
                                      ).astype(o_ref.dtype)

    n_rounds = len(matmul_stages)
    assert n_rounds >= N_CHUNKS + 3
    n_slots = (n_rounds - 1) * N_CHUNKS
    total = float(N_CHUNKS * sum(w for _, w in prep_groups))
    prep_slots = [[] for _ in range(n_slots)]
    done = 0.0
    for fn, weight in prep_groups:
        for c in range(N_CHUNKS):
            prep_slots[min(int(done / total * n_slots), n_slots - 1)].append((fn, c))
            done += weight
    for half in (0, 1):
        cur.update(half=half, slot_p=half, slot_m=1 - half)
        for rnd, mm_stage in enumerate(matmul_stages):
            if rnd < N_CHUNKS:
                chain_core(rnd)
            if 1 <= rnd <= N_CHUNKS:
                chain_mean(rnd - 1)
            if 2 <= rnd <= N_CHUNKS + 1:
                chain_finish(rnd - 2)
            for c in range(N_CHUNKS):
                mm_stage(c)
                if rnd == n_rounds - 1:
                    prep_store_pc(c)
                else:
                    for fn, pc in prep_slots[rnd * N_CHUNKS + c]:
                        fn(pc)
    state[...] = chain["hs"]


def _rwkv(proj3, lora3, w0, a0, k_k, k_a, r_k, lnx_g, lnx_b, wa2, g2p):
    batch, seq, _ = proj3.shape
    bps = seq // RWKV_BLOCK
    assert bps % 2 == 0
    wps = bps // 2
    n_total = batch * N_PAIRS * wps
    blk = (1, 2 * RWKV_BLOCK, LANES)

    def split(n):
        return n // (N_PAIRS * wps), n % wps, (n // wps) % N_PAIRS

    def prep_n(s):
        return jnp.minimum(s, n_total - 1)

    def chain_n(s):
        return jnp.maximum(s - 1, 0)

    def col(base):
        def index(s):
            b, i, p = split(prep_n(s))
            return b, i, base // LANES + p
        return index

    def lora_index(s):
        b, i, _ = split(prep_n(s))
        return b, i, 0

    def out_index(s):
        return split(chain_n(s))

    vec_prep = pl.BlockSpec((1, LANES), lambda s: (0, split(prep_n(s))[2]))
    vec_chain = pl.BlockSpec((1, LANES), lambda s: (0, split(chain_n(s))[2]))
    n_slots = 2 * N_CHUNKS
    return pl.pallas_call(
        functools.partial(_rwkv_kernel, blocks_per_stream=bps),
        grid=(n_total + 1,),
        in_specs=[
            pl.BlockSpec(blk, col(COL_R)),
            pl.BlockSpec(blk, col(COL_RK)),
            pl.BlockSpec(blk, col(COL_RV)),
            pl.BlockSpec(blk, col(COL_ZR)),
            pl.BlockSpec((1, 2 * RWKV_BLOCK, LORA_PAD), lora_index),
            vec_prep, vec_prep, vec_prep, vec_prep,
            pl.BlockSpec((LANES, 2 * LANES), lambda s: (0, split(prep_n(s))[2])),
            pl.BlockSpec((GATE_PAD, LANES), lambda s: (0, split(prep_n(s))[2])),
            vec_chain, vec_chain, vec_chain,
        ],
        out_specs=pl.BlockSpec(blk, out_index),
        out_shape=jax.ShapeDtypeStruct((batch, seq, RWKV_WIDTH), BF16),
        scratch_shapes=[
            pltpu.VMEM((LANES, LANES), F32),
            pltpu.VMEM((n_slots, LANES, LANES), BF16),
            pltpu.VMEM((n_slots, LANES, LANES), F32),
            pltpu.VMEM((n_slots, LANES, LANES), BF16),
            pltpu.VMEM((n_slots, LANES, 2 * LANES), BF16),
            pltpu.VMEM((n_slots, LANES, LANES), F32),
            pltpu.VMEM((n_slots, LANES, LANES), BF16),
            pltpu.VMEM((2, RWKV_BLOCK, LANES), F32),
            pltpu.VMEM((2, RWKV_BLOCK, LANES), F32),
            pltpu.VMEM((2, RWKV_BLOCK, LANES), F32),
            pltpu.VMEM((2, RWKV_BLOCK, LANES), F32),
            pltpu.VMEM((n_slots, LANES, LANES), BF16),
            pltpu.VMEM((n_slots, LANES, LANES), F32),
            pltpu.VMEM((n_slots, LANES, LANES), BF16),
            pltpu.VMEM((n_slots, LANES, LANES), F32),
            pltpu.VMEM((n_slots, LANES, LANES), F32),
        ],
        compiler_params=pltpu.CompilerParams(
            dimension_semantics=("arbitrary",),
            vmem_limit_bytes=VMEM_LIMIT),
        name="rwkv7_time_mix",
    )(proj3, proj3, proj3, proj3, lora3, w0, a0, k_k, k_a, wa2, g2p, r_k, lnx_g, lnx_b)


def _out_proj_kernel(att_ref, rwk_ref, wa_ref, wr_ref, x_ref, g_ref, o_ref, *, normalize):
    y = x_ref[...] + _dot(att_ref[...], wa_ref[...]) + _dot(rwk_ref[...], wr_ref[...])
    if normalize:
        ms = jnp.mean(y * y, axis=-1, keepdims=True)
        y = y * lax.rsqrt(ms + NORM_EPS) * g_ref[...]
    o_ref[...] = y


def _out_proj(att2, rwk2, w_out_b, x2, g, normalize):
    tokens = x2.shape[0]
    return pl.pallas_call(
        functools.partial(_out_proj_kernel, normalize=normalize),
        grid=(tokens // OUT_TM,),
        in_specs=[
            pl.BlockSpec((OUT_TM, ATT_WIDTH), lambda i: (i, 0)),
            pl.BlockSpec((OUT_TM, RWKV_WIDTH), lambda i: (i, 0)),
            pl.BlockSpec((ATT_WIDTH, D_MODEL), lambda i: (0, 0)),
            pl.BlockSpec((RWKV_WIDTH, D_MODEL), lambda i: (ATT_WIDTH // RWKV_WIDTH, 0)),
            pl.BlockSpec((OUT_TM, D_MODEL), lambda i: (i, 0)),
            pl.BlockSpec((1, D_MODEL), lambda i: (0, 0)),
        ],
        out_specs=pl.BlockSpec((OUT_TM, D_MODEL), lambda i: (i, 0)),
        out_shape=jax.ShapeDtypeStruct((tokens, D_MODEL), F32),
        compiler_params=pltpu.CompilerParams(
            dimension_semantics=("arbitrary",),
            vmem_limit_bytes=VMEM_LIMIT),
        name="out_proj",
    )(att2, rwk2, w_out_b, w_out_b, x2, g)


def _rope_tables(seq):
    inv_freq = ROPE_THETA ** (-jnp.arange(0, HEAD_DIM, 2, dtype=F32) / HEAD_DIM)
    ang = jnp.arange(seq, dtype=jnp.int32).astype(F32)[:, None] * inv_freq[None, :]
    cos, sin = jnp.cos(ang), jnp.sin(ang)
    reps = LANES // HEAD_DIM
    cos_t = jnp.tile(jnp.concatenate([cos, cos], axis=1), (1, reps))
    sin_t = jnp.tile(jnp.concatenate([-sin, sin], axis=1), (1, reps))
    return cos_t, sin_t


def _layer(x, norm_g, w_in, shift_mu, w0, w2, a0, a2, g2, k_k, k_a, r_k,
           lnx_g, lnx_b, w_out, cos_t, sin_t):
    batch, seq, _ = x.shape
    a_w, r_w = ATT_WIDTH, RWKV_WIDTH
    lo = 4 * a_w + 3 * r_w
    n_lora = DECAY_LORA + AAA_LORA + GATE_LORA
    assert lo == COL_ZR
    w_all = w_in.astype(BF16)
    w_gate = w_all[:, lo + n_lora:]
    w_lora = jnp.pad(w_all[:, lo:lo + n_lora], ((0, 0), (0, LORA_PAD - n_lora)))
    mu_main = shift_mu[None, 0:3 * r_w]
    mu_lora = jnp.pad(shift_mu[None, 3 * r_w:], ((0, 0), (0, LORA_PAD - n_lora)))
    w2p = jnp.pad(w2, ((0, LANES - DECAY_LORA), (0, 0))).reshape(LANES, N_PAIRS, LANES)
    a2p = jnp.pad(a2, ((DECAY_LORA, LANES - DECAY_LORA - AAA_LORA), (0, 0))
                  ).reshape(LANES, N_PAIRS, LANES)
    wa2 = jnp.concatenate([w2p, a2p], axis=2).reshape(LANES, 2 * r_w)
    g2p = jnp.pad(g2, ((0, GATE_PAD - GATE_LORA), (0, 0)))

    x2 = x.reshape(batch * seq, D_MODEL)
    proj, lora = _in_proj(x2, norm_g[None, :], w_all, w_gate, w_lora, cos_t, sin_t,
                          mu_main, mu_lora, seq)
    proj3 = proj.reshape(batch, seq, PROJ_WIDTH)
    lora3 = lora.reshape(batch, seq, LORA_PAD)
    att = _attention(proj3)
    rwk = _rwkv(proj3, lora3, w0[None, :], a0[None, :],
                k_k[None, :], k_a[None, :], r_k.reshape(1, r_w), lnx_g[None, :],
                lnx_b[None, :], wa2, g2p)
    w_out_b = w_out.astype(BF16)
    return att.reshape(batch * seq, a_w), rwk.reshape(batch * seq, r_w), w_out_b, x2


def kernel(x, norm_g, w_in, shift_mu, w0, w2, a0, a2, g2, k_k, k_a, r_k,
           lnx_g, lnx_b, w_out, final_g):
    batch, seq, _ = x.shape
    depth = norm_g.shape[0]
    assert seq % SUPER == 0 and seq % IN_TM == 0 and (batch * seq) % OUT_TM == 0
    cos_t, sin_t = _rope_tables(seq)
    for l in range(depth):
        att2, rwk2, w_out_b, x2 = _layer(
            x, norm_g[l], w_in[l], shift_mu[l], w0[l], w2[l], a0[l], a2[l], g2[l],
            k_k[l], k_a[l], r_k[l], lnx_g[l], lnx_b[l], w_out[l], cos_t, sin_t)
        y2 = _out_proj(att2, rwk2, w_out_b, x2, final_g[None, :],
                       normalize=(l == depth - 1))
        x = y2.reshape(batch, seq, D_MODEL)
    return x
```

```python
import functools

import jax
import jax.numpy as jnp
from jax import lax
from jax.experimental import pallas as pl
from jax.experimental.pallas import tpu as pltpu

F32 = jnp.float32
BF16 = jnp.bfloat16

D_MODEL = 2048
HEAD_DIM = 64
ATT_WIDTH = 1024
RWKV_WIDTH = 1024
DIL_PATTERNS = ((128, 1), (512, 4), (2048, 16))
ROPE_THETA = 10000.0
DECAY_LORA = 64
AAA_LORA = 64
GATE_LORA = 160
NORM_EPS = 1e-5
LNX_EPS = 64e-5
DECAY_SCALE = 0.6065306597126334
KK_NORM_FLOOR = 1e-12
LOG2_E = 1.4426950408889634

LANES = 128
SUBLANES = 8
N_PAIRS = ATT_WIDTH // LANES
N_BACK = 128
Q_BLOCK = 128
SUPER = 2048
CHUNK = 64
RWKV_BLOCK = 1024
N_CHUNKS = RWKV_BLOCK // CHUNK
LORA_PAD = 512
LORA_USED = 384
GATE_PAD = 256

COL_Q, COL_K, COL_V, COL_ZA = 0, 1024, 2048, 3072
COL_R, COL_RK, COL_RV, COL_ZR = 4096, 5120, 6144, 7168
PROJ_WIDTH = 8192

IN_TM, IN_TN = 1024, 1024
IN_SUB = 512
N_ROPE_STEPS = (COL_V - COL_Q) // IN_TN
N_Q_STEPS = (COL_K - COL_Q) // IN_TN
N_ALIGNED_STEPS = COL_ZR // IN_TN
N_SHIFT_LO = COL_R // IN_TN
OUT_TM = 512
VMEM_LIMIT = 56 * 1024 * 1024


def _dot(a, b):
    return jnp.dot(a, b, preferred_element_type=F32)


def _dot_nt(a, b):
    return lax.dot_general(a, b, (((1,), (1,)), ((), ())), preferred_element_type=F32)


def _sigmoid(x):
    return 0.5 * jnp.tanh(0.5 * x) + 0.5


def _bf16_pieces(x, terms):
    pieces = []
    for _ in range(terms):
        piece = x.astype(BF16)
        pieces.append(piece)
        x = x - piece.astype(F32)
    return pieces


def _head_sum(x, head0):
    first = jnp.sum(jnp.where(head0, x, 0.0), axis=-1, keepdims=True)
    second = jnp.sum(jnp.where(head0, 0.0, x), axis=-1, keepdims=True)
    return jnp.where(head0, first, second)


def _in_proj_kernel(x_ref, g_ref, w_ref, wz_ref, wl_ref, cos_ref, sin_ref,
                    mu_ref, mul_ref, o_ref, lora_ref, h_ref, carry_ref, carry_lora_ref,
                    *, pos_blocks):
    i = pl.program_id(0)
    j = pl.program_id(1)
    subs = [slice(s * IN_SUB, (s + 1) * IN_SUB) for s in range(IN_TN // IN_SUB)]
    seq_start = (i % pos_blocks) == 0
    last_row = slice(IN_TM - 1, IN_TM)

    def token_shift(u, carry_row, mu):
        row0 = lax.broadcasted_iota(jnp.int32, u.shape, 0) == 0
        prev = jnp.where(row0, jnp.where(seq_start, 0.0, carry_row), pltpu.roll(u, 1, 0))
        return u + (prev - u) * mu

    @pl.when((i == 0) & (j == 0))
    def _():
        carry_ref[...] = jnp.zeros_like(carry_ref)
        carry_lora_ref[...] = jnp.zeros_like(carry_lora_ref)

    @pl.when(j == 0)
    def _():
        x = x_ref[...]
        ms = jnp.mean(x * x, axis=-1, keepdims=True)
        h_ref[...] = (x * lax.rsqrt(ms + NORM_EPS) * g_ref[...]).astype(BF16)
        lora = _dot(h_ref[...], wl_ref[...])
        lora_ref[...] = token_shift(lora, carry_lora_ref[0:1, :], mul_ref[...])
        carry_lora_ref[0:1, :] = lora[last_row, :]

    @pl.when(j < N_ROPE_STEPS)
    def _():
        reps = IN_SUB // LANES
        cos = jnp.concatenate([cos_ref[...]] * reps, axis=1)
        sin = jnp.concatenate([sin_ref[...]] * reps, axis=1)
        lane = lax.broadcasted_iota(jnp.int32, (IN_TM, IN_SUB), 1)
        first_half = (lane % HEAD_DIM) < (HEAD_DIM // 2)
        scale = jnp.where(j < N_Q_STEPS, LOG2_E * HEAD_DIM ** -0.5, 1.0).astype(F32)
        for sub in subs:
            acc = _dot(h_ref[...], w_ref[:, sub])
            partner = jnp.where(first_half,
                                pltpu.roll(acc, IN_SUB - HEAD_DIM // 2, 1),
                                pltpu.roll(acc, HEAD_DIM // 2, 1))
            o_ref[:, sub] = (acc * cos + partner * sin) * scale

    @pl.when((j >= N_ROPE_STEPS) & (j < N_SHIFT_LO))
    def _():
        for sub in subs:
            o_ref[:, sub] = _dot(h_ref[...], w_ref[:, sub])

    @pl.when((j >= N_SHIFT_LO) & (j < N_ALIGNED_STEPS))
    def _():
        slot = j - N_SHIFT_LO
        for sub in subs:
            acc = _dot(h_ref[...], w_ref[:, sub])
            o_ref[:, sub] = token_shift(acc, carry_ref[slot, 0:1, sub], mu_ref[:, sub])
            carry_ref[slot, 0:1, sub] = acc[last_row, :]

    @pl.when(j >= N_ALIGNED_STEPS)
    def _():
        for sub in subs:
            o_ref[:, sub] = _dot(h_ref[...], wz_ref[:, sub])


def _in_proj(x2, g, w_all, w_gate, w_lora, cos_t, sin_t, mu_main, mu_lora, seq):
    tokens = x2.shape[0]
    pos_blocks = seq // IN_TM
    n_shift = N_ALIGNED_STEPS - N_SHIFT_LO

    def mu_index(i, j):
        return 0, jnp.clip(j - N_SHIFT_LO, 0, n_shift - 1)

    return pl.pallas_call(
        functools.partial(_in_proj_kernel, pos_blocks=pos_blocks),
        grid=(tokens // IN_TM, PROJ_WIDTH // IN_TN),
        in_specs=[
            pl.BlockSpec((IN_TM, D_MODEL), lambda i, j: (i, 0)),
            pl.BlockSpec((1, D_MODEL), lambda i, j: (0, 0)),
            pl.BlockSpec((D_MODEL, IN_TN), lambda i, j: (0, jnp.minimum(j, N_ALIGNED_STEPS - 1))),
            pl.BlockSpec((D_MODEL, IN_TN), lambda i, j: (0, 0)),
            pl.BlockSpec((D_MODEL, LORA_PAD), lambda i, j: (0, 0)),
            pl.BlockSpec((IN_TM, LANES), lambda i, j: (i % pos_blocks, 0)),
            pl.BlockSpec((IN_TM, LANES), lambda i, j: (i % pos_blocks, 0)),
            pl.BlockSpec((1, IN_TN), mu_index),
            pl.BlockSpec((1, LORA_PAD), lambda i, j: (0, 0)),
        ],
        out_specs=[
            pl.BlockSpec((IN_TM, IN_TN), lambda i, j: (i, j)),
            pl.BlockSpec((IN_TM, LORA_PAD), lambda i, j: (i, 0)),
        ],
        out_shape=[
            jax.ShapeDtypeStruct((tokens, PROJ_WIDTH), F32),
            jax.ShapeDtypeStruct((tokens, LORA_PAD), F32),
        ],
        scratch_shapes=[
            pltpu.VMEM((IN_TM, D_MODEL), BF16),
            pltpu.VMEM((n_shift, SUBLANES, IN_TN), F32),
            pltpu.VMEM((SUBLANES, LORA_PAD), F32),
        ],
        compiler_params=pltpu.CompilerParams(
            dimension_semantics=("arbitrary", "arbitrary"),
            vmem_limit_bytes=VMEM_LIMIT),
        name="in_proj",
    )(x2, g, w_all, w_gate, w_lora, cos_t, sin_t, mu_main, mu_lora)


def _attention_kernel(q_ref, kc_ref, kp_ref, vc_ref, vp_ref, z_ref, o_ref,
                      k4, v4, acc_s, m_s, den_s):
    i = pl.program_id(2)
    slab = 2 * SUPER // 4
    for b in range(4):
        for dst, prev_ref, cur_ref in ((k4, kp_ref, kc_ref), (v4, vp_ref, vc_ref)):
            dst[b * slab:b * slab + slab // 2, :] = prev_ref[0, pl.ds(b, slab // 2, stride=4), :]
            dst[b * slab + slab // 2:(b + 1) * slab, :] = cur_ref[0, pl.ds(b, slab // 2,
                                                                          stride=4), :]

    qi = lax.broadcasted_iota(jnp.int32, (Q_BLOCK, 2 * Q_BLOCK), 0)
    ki = lax.broadcasted_iota(jnp.int32, (Q_BLOCK, 2 * Q_BLOCK), 1)
    rel = Q_BLOCK + qi - ki
    band = (rel >= 0) & (rel <= N_BACK)
    cur = ki >= Q_BLOCK
    lane = lax.broadcasted_iota(jnp.int32, (Q_BLOCK, LANES), 1)
    head0 = lane < HEAD_DIM

    def rows(start, size, stride):
        return pl.ds(start, size, stride=stride) if stride > 1 else pl.ds(start, size)

    def window(buf4, cur_ref, prev_ref, q_start, dil):
        k_start = SUPER + q_start - Q_BLOCK * dil
        if dil > 1:
            return buf4[rows((k_start % 4) * slab + k_start // 4, 2 * Q_BLOCK, dil // 4), :]
        if k_start >= SUPER:
            return cur_ref[0, pl.ds(k_start - SUPER, 2 * Q_BLOCK), :]
        return jnp.concatenate([prev_ref[0, pl.ds(k_start, Q_BLOCK), :],
                                cur_ref[0, pl.ds(q_start, Q_BLOCK), :]], axis=0)

    def block(pat, dil, q_start):
        rows_q = rows(q_start, Q_BLOCK, dil)
        qs = q_ref[0, rows_q, :]
        kw = window(k4, kc_ref, kp_ref, q_start, dil).astype(BF16)
        vw = window(v4, vc_ref, vp_ref, q_start, dil).astype(BF16)
        if q_start >= Q_BLOCK * dil:
            valid = band
        else:
            valid = band & (cur | (i > 0))
        accs, ms, dens = [], [], []
        for hm in (head0, ~head0):
            qh = jnp.where(hm, qs, 0.0).astype(BF16)
            s = _dot_nt(qh, kw)
            s = jnp.where(valid, s, -jnp.inf)
            m = jnp.max(s, axis=-1, keepdims=True)
            p = jnp.exp2(s - m)
            dens.append(jnp.sum(p, axis=-1, keepdims=True))
            ms.append(m)
            accs.append(_dot(p.astype(BF16), vw))
        acc_s[pat, rows_q, :] = jnp.where(head0, accs[0], accs[1])
        m_s[pat, rows_q, :] = jnp.where(head0, ms[0], ms[1])
        den_s[pat, rows_q, :] = jnp.where(head0, dens[0], dens[1])

    for pat, (window_len, dil) in enumerate(DIL_PATTERNS):
        span = Q_BLOCK * dil
        for s_idx in range(SUPER // span):
            for r in range(dil):
                block(pat, dil, s_idx * span + r)

    m0, m1, m2 = m_s[0], m_s[1], m_s[2]
    mx = jnp.maximum(jnp.maximum(m0, m1), m2)
    e0, e1, e2 = jnp.exp2(m0 - mx), jnp.exp2(m1 - mx), jnp.exp2(m2 - mx)
    num = e0 * acc_s[0] + e1 * acc_s[1] + e2 * acc_s[2]
    den = e0 * den_s[0] + e1 * den_s[1] + e2 * den_s[2]
    z = z_ref[0]
    o_ref[0] = (num / den * (z * _sigmoid(z))).astype(o_ref.dtype)


def _attention(proj3):
    batch, seq, _ = proj3.shape
    blk = (1, SUPER, LANES)

    def col(base):
        return lambda b, p, i: (b, i, base // LANES + p)

    def col_prev(base):
        return lambda b, p, i: (b, jnp.maximum(i - 1, 0), base // LANES + p)

    return pl.pallas_call(
        _attention_kernel,
        grid=(batch, N_PAIRS, seq // SUPER),
        in_specs=[
            pl.BlockSpec(blk, col(COL_Q)),
            pl.BlockSpec(blk, col(COL_K)),
            pl.BlockSpec(blk, col_prev(COL_K)),
            pl.BlockSpec(blk, col(COL_V)),
            pl.BlockSpec(blk, col_prev(COL_V)),
            pl.BlockSpec(blk, col(COL_ZA)),
        ],
        out_specs=pl.BlockSpec(blk, lambda b, p, i: (b, i, p)),
        out_shape=jax.ShapeDtypeStruct((batch, seq, ATT_WIDTH), BF16),
        scratch_shapes=[
            pltpu.VMEM((2 * SUPER, LANES), F32),
            pltpu.VMEM((2 * SUPER, LANES), F32),
            pltpu.VMEM((len(DIL_PATTERNS), SUPER, LANES), F32),
            pltpu.VMEM((len(DIL_PATTERNS), SUPER, LANES), F32),
            pltpu.VMEM((len(DIL_PATTERNS), SUPER, LANES), F32),
        ],
        compiler_params=pltpu.CompilerParams(
            dimension_semantics=("arbitrary", "arbitrary", "arbitrary"),
            vmem_limit_bytes=VMEM_LIMIT),
        name="dilated_attention",
    )(proj3, proj3, proj3, proj3, proj3, proj3)


def _rwkv_kernel(r_ref, k_ref, v_ref, z_ref, lora_ref,
                 w0_ref, a0_ref, kk_ref, ka_ref, wa2_ref, g2_ref,
                 rk_ref, lng_ref, lnb_ref,
                 o_ref,
                 state,
                 pm_a, pm_r, pm_kb, pm_bk, pm_d, pm_v,
                 pc_r, pc_k, pc_v, pc_gz,
                 mc_rw, mc_y0, mc_m, mc_g, mc_d,
                 *, blocks_per_stream):
    t = pl.program_id(0)
    first_chain = ((2 * jnp.maximum(t - 1, 0)) % blocks_per_stream) == 0

    @pl.when(t == 0)
    def _():
        for ref in (state, pm_a, pm_r, pm_kb, pm_bk, pm_d, pm_v,
                    pc_r, pc_k, pc_v, pc_gz, mc_rw, mc_y0, mc_m, mc_g, mc_d):
            ref[...] = jnp.zeros_like(ref)

    rr = lax.broadcasted_iota(jnp.int32, (LANES, LANES), 0)
    cc = lax.broadcasted_iota(jnp.int32, (LANES, LANES), 1)
    same_head = (rr // HEAD_DIM) == (cc // HEAD_DIM)
    strict = same_head & ((cc % CHUNK) < (rr % CHUNK))
    incl = same_head & ((cc % CHUNK) <= (rr % CHUNK))
    eye = (rr == cc).astype(F32)
    tril = (lax.broadcasted_iota(jnp.int32, (CHUNK, CHUNK), 1)
            <= lax.broadcasted_iota(jnp.int32, (CHUNK, CHUNK), 0)).astype(BF16)
    head0 = lax.broadcasted_iota(jnp.int32, (CHUNK, LANES), 1) < HEAD_DIM
    decay_lane = lax.broadcasted_iota(jnp.int32, (CHUNK, LANES), 1) < DECAY_LORA
    zero_blk = jnp.zeros((LANES, LANES), BF16)
    last = slice(CHUNK - 1, CHUNK)

    def bd(x):
        return jnp.concatenate([jnp.where(head0, x, 0.0), jnp.where(head0, 0.0, x)], axis=0)

    rows = [slice(c * CHUNK, (c + 1) * CHUNK) for c in range(N_CHUNKS)]
    cur = {"half": 0, "slot_p": 0, "slot_m": 1}

    def block_rows(c):
        start = cur["half"] * RWKV_BLOCK + c * CHUNK
        return slice(start, start + CHUNK)

    pv = [dict() for _ in range(N_CHUNKS)]

    def prep_lora_acts(c):
        lora = lora_ref[0, block_rows(c), 0:LORA_USED]
        wa = lora[:, 0:LANES]
        pv[c]["wa_act"] = jnp.where(decay_lane, jnp.tanh(wa), wa).astype(BF16)
        pv[c]["gate_act"] = _sigmoid(lora[:, LANES:LANES + GATE_PAD]).astype(BF16)

    def prep_lora_matmuls(c):
        d = pv[c]
        wa_out = _dot(d["wa_act"], wa2_ref[...].astype(BF16))
        d["lw"] = -(DECAY_SCALE * _sigmoid(w0_ref[...] + wa_out[:, 0:LANES]))
        d["a"] = _sigmoid(a0_ref[...] + wa_out[:, LANES:])
        g = _dot(d["gate_act"], g2_ref[...].astype(BF16))
        z = z_ref[0, block_rows(c), :]
        d["gz"] = g * (z * _sigmoid(z))

    def prep_key_norm(c):
        d = pv[c]
        k = k_ref[0, block_rows(c), :]
        kk = k * kk_ref[...]
        kk = kk * lax.rsqrt(jnp.maximum(_head_sum(kk * kk, head0), KK_NORM_FLOOR ** 2))
        d["kk"] = kk
        d["k2"] = k * (1.0 + (d["a"] - 1.0) * ka_ref[...])
        d["b"] = kk * d["a"]
        parts = _dot(tril, jnp.concatenate(_bf16_pieces(d["lw"], 3), axis=1))
        d["cs"] = parts[:, 0:LANES] + parts[:, LANES:2 * LANES] + parts[:, 2 * LANES:]

    def prep_exp(c):
        d = pv[c]
        d["e_pos"] = jnp.exp(d["cs"])
        d["e_neg"] = jnp.exp(-d["cs"])
        d["e_excl"] = jnp.exp(d["cs"] - d["lw"])
        d["decay_end"] = d["e_pos"][last, :]

    def prep_store_ar(c):
        d = pv[c]
        idx = cur["slot_p"] * N_CHUNKS + c
        pm_a[idx] = bd(-d["kk"] * d["e_excl"]).astype(BF16)
        pm_r[idx] = bd(r_ref[0, block_rows(c), :] * d["e_pos"])

    def prep_store_kbv(c):
        d = pv[c]
        slot_p = cur["slot_p"]
        idx = slot_p * N_CHUNKS + c
        pm_kb[idx] = jnp.concatenate([d["k2"] * d["e_neg"], d["b"] * d["e_neg"]],
                                     axis=0).astype(BF16)
        pm_v[idx] = bd(v_ref[0, block_rows(c), :]).astype(BF16)

    def prep_store_pc(c):
        d = pv[c]
        slot_p = cur["slot_p"]
        pc_r[slot_p, rows[c], :] = r_ref[0, block_rows(c), :]
        pc_k[slot_p, rows[c], :] = d["k2"]
        pc_v[slot_p, rows[c], :] = v_ref[0, block_rows(c), :]
        pc_gz[slot_p, rows[c], :] = d["gz"]

    def prep_store_bk(c):
        d = pv[c]
        idx = cur["slot_p"] * N_CHUNKS + c
        e_end = d["decay_end"] * d["e_neg"]
        bkd_t = jnp.concatenate([bd(d["b"] * e_end), bd(d["k2"] * e_end),
                                 jnp.broadcast_to(d["decay_end"], (LANES, LANES))], axis=0).T
        pm_bk[idx] = bkd_t[:, 0:2 * LANES].astype(BF16)
        pm_d[idx] = bkd_t[:, 2 * LANES:]

    prep_groups = [(prep_lora_acts, 110), (prep_lora_matmuls, 110),
                   (prep_key_norm, 150), (prep_exp, 50), (prep_store_ar, 56),
                   (prep_store_kbv, 50), (prep_store_bk, 75)]

    mm = [dict() for _ in range(N_CHUNKS)]

    def mm_scores(c):
        idx = cur["slot_m"] * N_CHUNKS + c
        kb = pm_kb[idx]
        k_t, b_t = kb[0:CHUNK, :], kb[CHUNK:, :]
        lhs = jnp.concatenate([pm_a[idx], pm_r[idx].astype(BF16)], axis=0)
        x = _dot_nt(lhs, jnp.concatenate([k_t, k_t, b_t, b_t], axis=0))
        d = mm[c]
        d["a_ak"] = jnp.where(strict, x[0:LANES, 0:LANES], 0.0).astype(BF16)
        a_ab = jnp.where(strict, x[0:LANES, LANES:], 0.0)
        d["a_r"] = jnp.concatenate([jnp.where(incl, x[LANES:, 0:LANES], 0.0),
                                    jnp.where(incl, x[LANES:, LANES:], 0.0)],
                                   axis=1).astype(BF16)
        d["q"] = a_ab.astype(BF16)
        d["p"] = eye + a_ab

    def mm_square(c):
        d = mm[c]
        d["q"] = _dot(d["q"], d["q"]).astype(BF16)

    def mm_double(c):
        d = mm[c]
        qp = _dot(d["q"], jnp.concatenate([d["q"], d["p"].astype(BF16)], axis=1))
        d["q"] = qp[:, 0:LANES].astype(BF16)
        d["p"] = d["p"] + qp[:, LANES:]

    def mm_inverse(c):
        d = mm[c]
        d["t"] = (d["p"] + _dot(d["q"], d["p"].astype(BF16))).astype(BF16)

    def mm_akv(c):
        d = mm[c]
        d["ak_v"] = _dot(d["a_ak"], pm_v[cur["slot_m"] * N_CHUNKS + c]).astype(BF16)

    def mm_wu(c):
        d = mm[c]
        rhs = jnp.concatenate([pm_a[cur["slot_m"] * N_CHUNKS + c], d["ak_v"]], axis=1)
        d["wu"] = _dot(d["t"], rhs).astype(BF16)

    def mm_yw(c):
        d = mm[c]
        idx = cur["slot_m"] * N_CHUNKS + c
        wu = d["wu"]
        rhs = jnp.concatenate(
            [jnp.concatenate([pm_v[idx], zero_blk], axis=1),
             jnp.concatenate([wu[:, LANES:], wu[:, 0:LANES]], axis=1)], axis=0)
        yw = _dot(d["a_r"], rhs)
        mc_y0[idx] = yw[:, 0:LANES]
        mc_rw[idx] = (pm_r[idx] + yw[:, LANES:]).astype(BF16)

    def mm_mg(c):
        idx = cur["slot_m"] * N_CHUNKS + c
        rhs = jnp.concatenate([mm[c]["wu"],
                               jnp.concatenate([zero_blk, pm_v[idx]], axis=1)], axis=0)
        mg = _dot(pm_bk[idx], rhs)
        mc_m[idx] = mg[:, 0:LANES].astype(BF16)
        mc_g[idx] = mg[:, LANES:]
        mc_d[idx] = pm_d[idx]

    matmul_stages = [mm_scores, mm_square, mm_double, mm_double, mm_double, mm_double,
                     mm_inverse, mm_akv, mm_wu, mm_yw, mm_mg]

    chain = {"hs": jnp.where(first_chain, 0.0, state[...])}

    cv = [dict() for _ in range(N_CHUNKS)]

    def chain_core(c):
        idx = cur["slot_p"] * N_CHUNKS + c
        hs = chain["hs"]
        hs_b = hs.astype(BF16)
        y = _dot(mc_rw[idx], hs_b) + mc_y0[idx]
        chain["hs"] = hs * mc_d[idx] + _dot(mc_m[idx], hs_b) + mc_g[idx]
        cv[c]["y"] = y[0:CHUNK, :] + y[CHUNK:, :]

    def chain_mean(c):
        slot_p, rws, d = cur["slot_p"], rows[c], cv[c]
        d["yc"] = d["y"] - _head_sum(d["y"], head0) * (1.0 / HEAD_DIM)
        d["bonus"] = _head_sum(pc_r[slot_p, rws, :] * pc_k[slot_p, rws, :] * rk_ref[...],
                               head0) * pc_v[slot_p, rws, :]

    def chain_finish(c):
        slot_p, rws, d = cur["slot_p"], rows[c], cv[c]
        var = _head_sum(d["yc"] * d["yc"], head0) * (1.0 / HEAD_DIM)
        yn = d["yc"] * lax.rsqrt(var + LNX_EPS) * lng_ref[...] + lnb_ref[...]
        o_ref[0, block_rows(c), :] = ((yn + d["bonus"]) * pc_gz[slot_p, rws, :]
                                      ).astype(o_ref.dtype)

    n_rounds = len(matmul_stages)
    per_round = -(-N_CHUNKS // (n_rounds - 3))
    core_gap = N_CHUNKS // per_round

    def chain_cores_of(rnd):
        return range(min(rnd * per_round, N_CHUNKS), min((rnd + 1) * per_round, N_CHUNKS))

    n_slots = (n_rounds - 1) * N_CHUNKS
    total = float(N_CHUNKS * sum(w for _, w in prep_groups))
    prep_slots = [[] for _ in range(n_slots)]
    done = 0.0
    for fn, weight in prep_groups:
        for c in range(N_CHUNKS):
            prep_slots[min(int(done / total * n_slots), n_slots - 1)].append((fn, c))
            done += weight
    for half in (0, 1):
        cur.update(half=half, slot_p=half, slot_m=1 - half)
        for rnd, mm_stage in enumerate(matmul_stages):
            if rnd >= 1:
                for chunk in chain_cores_of(rnd - 1):
                    chain_mean(chunk)
            if rnd >= 2:
                for chunk in chain_cores_of(rnd - 2):
                    chain_finish(chunk)
            cores = list(chain_cores_of(rnd))
            for c in range(N_CHUNKS):
                if c % core_gap == 0 and c // core_gap < len(cores):
                    chain_core(cores[c // core_gap])
                mm_stage(c)
                if rnd == n_rounds - 1:
                    prep_store_pc(c)
                else:
                    for fn, pc in prep_slots[rnd * N_CHUNKS + c]:
                        fn(pc)
    state[...] = chain["hs"]


def _rwkv(proj3, lora3, w0, a0, k_k, k_a, r_k, lnx_g, lnx_b, wa2, g2p):
    batch, seq, _ = proj3.shape
    bps = seq // RWKV_BLOCK
    assert bps % 2 == 0
    wps = bps // 2
    n_total = batch * N_PAIRS * wps
    blk = (1, 2 * RWKV_BLOCK, LANES)

    def split(n):
        return n // (N_PAIRS * wps), n % wps, (n // wps) % N_PAIRS

    def prep_n(s):
        return jnp.minimum(s, n_total - 1)

    def chain_n(s):
        return jnp.maximum(s - 1, 0)

    def col(base):
        def index(s):
            b, i, p = split(prep_n(s))
            return b, i, base // LANES + p
        return index

    def lora_index(s):
        b, i, _ = split(prep_n(s))
        return b, i, 0

    def out_index(s):
        return split(chain_n(s))

    vec_prep = pl.BlockSpec((1, LANES), lambda s: (0, split(prep_n(s))[2]))
    vec_chain = pl.BlockSpec((1, LANES), lambda s: (0, split(chain_n(s))[2]))
    n_slots = 2 * N_CHUNKS
    return pl.pallas_call(
        functools.partial(_rwkv_kernel, blocks_per_stream=bps),
        grid=(n_total + 1,),
        in_specs=[
            pl.BlockSpec(blk, col(COL_R)),
            pl.BlockSpec(blk, col(COL_RK)),
            pl.BlockSpec(blk, col(COL_RV)),
            pl.BlockSpec(blk, col(COL_ZR)),
            pl.BlockSpec((1, 2 * RWKV_BLOCK, LORA_PAD), lora_index),
            vec_prep, vec_prep, vec_prep, vec_prep,
            pl.BlockSpec((LANES, 2 * LANES), lambda s: (0, split(prep_n(s))[2])),
            pl.BlockSpec((GATE_PAD, LANES), lambda s: (0, split(prep_n(s))[2])),
            vec_chain, vec_chain, vec_chain,
        ],
        out_specs=pl.BlockSpec(blk, out_index),
        out_shape=jax.ShapeDtypeStruct((batch, seq, RWKV_WIDTH), BF16),
        scratch_shapes=[
            pltpu.VMEM((LANES, LANES), F32),
            pltpu.VMEM((n_slots, LANES, LANES), BF16),
            pltpu.VMEM((n_slots, LANES, LANES), F32),
            pltpu.VMEM((n_slots, LANES, LANES), BF16),
            pltpu.VMEM((n_slots, LANES, 2 * LANES), BF16),
            pltpu.VMEM((n_slots, LANES, LANES), F32),
            pltpu.VMEM((n_slots, LANES, LANES), BF16),
            pltpu.VMEM((2, RWKV_BLOCK, LANES), F32),
            pltpu.VMEM((2, RWKV_BLOCK, LANES), F32),
            pltpu.VMEM((2, RWKV_BLOCK, LANES), F32),
            pltpu.VMEM((2, RWKV_BLOCK, LANES), F32),
            pltpu.VMEM((n_slots, LANES, LANES), BF16),
            pltpu.VMEM((n_slots, LANES, LANES), F32),
            pltpu.VMEM((n_slots, LANES, LANES), BF16),
            pltpu.VMEM((n_slots, LANES, LANES), F32),
            pltpu.VMEM((n_slots, LANES, LANES), F32),
        ],
        compiler_params=pltpu.CompilerParams(
            dimension_semantics=("arbitrary",),
            vmem_limit_bytes=VMEM_LIMIT),
        name="rwkv7_time_mix",
    )(proj3, proj3, proj3, proj3, lora3, w0, a0, k_k, k_a, wa2, g2p, r_k, lnx_g, lnx_b)


def _out_proj_kernel(att_ref, rwk_ref, wa_ref, wr_ref, x_ref, g_ref, o_ref, *, normalize):
    y = x_ref[...] + _dot(att_ref[...], wa_ref[...]) + _dot(rwk_ref[...], wr_ref[...])
    if normalize:
        ms = jnp.mean(y * y, axis=-1, keepdims=True)
        y = y * lax.rsqrt(ms + NORM_EPS) * g_ref[...]
    o_ref[...] = y


def _out_proj(att2, rwk2, w_out_b, x2, g, normalize):
    tokens = x2.shape[0]
    return pl.pallas_call(
        functools.partial(_out_proj_kernel, normalize=normalize),
        grid=(tokens // OUT_TM,),
        in_specs=[
            pl.BlockSpec((OUT_TM, ATT_WIDTH), lambda i: (i, 0)),
            pl.BlockSpec((OUT_TM, RWKV_WIDTH), lambda i: (i, 0)),
            pl.BlockSpec((ATT_WIDTH, D_MODEL), lambda i: (0, 0)),
            pl.BlockSpec((RWKV_WIDTH, D_MODEL), lambda i: (ATT_WIDTH // RWKV_WIDTH, 0)),
            pl.BlockSpec((OUT_TM, D_MODEL), lambda i: (i, 0)),
            pl.BlockSpec((1, D_MODEL), lambda i: (0, 0)),
        ],
        out_specs=pl.BlockSpec((OUT_TM, D_MODEL), lambda i: (i, 0)),
        out_shape=jax.ShapeDtypeStruct((tokens, D_MODEL), F32),
        compiler_params=pltpu.CompilerParams(
            dimension_semantics=("arbitrary",),
            vmem_limit_bytes=VMEM_LIMIT),
        name="out_proj",
    )(att2, rwk2, w_out_b, w_out_b, x2, g)


def _rope_tables(seq):
    inv_freq = ROPE_THETA ** (-jnp.arange(0, HEAD_DIM, 2, dtype=F32) / HEAD_DIM)
    ang = jnp.arange(seq, dtype=jnp.int32).astype(F32)[:, None] * inv_freq[None, :]
    cos, sin = jnp.cos(ang), jnp.sin(ang)
    reps = LANES // HEAD_DIM
    cos_t = jnp.tile(jnp.concatenate([cos, cos], axis=1), (1, reps))
    sin_t = jnp.tile(jnp.concatenate([-sin, sin], axis=1), (1, reps))
    return cos_t, sin_t


def _layer(x, norm_g, w_in, shift_mu, w0, w2, a0, a2, g2, k_k, k_a, r_k,
           lnx_g, lnx_b, w_out, cos_t, sin_t):
    batch, seq, _ = x.shape
    a_w, r_w = ATT_WIDTH, RWKV_WIDTH
    lo = 4 * a_w + 3 * r_w
    n_lora = DECAY_LORA + AAA_LORA + GATE_LORA
    assert lo == COL_ZR
    w_all = w_in.astype(BF16)
    w_gate = w_all[:, lo + n_lora:]
    w_lora = jnp.pad(w_all[:, lo:lo + n_lora], ((0, 0), (0, LORA_PAD - n_lora)))
    mu_main = shift_mu[None, 0:3 * r_w]
    mu_lora = jnp.pad(shift_mu[None, 3 * r_w:], ((0, 0), (0, LORA_PAD - n_lora)))
    w2p = jnp.pad(w2, ((0, LANES - DECAY_LORA), (0, 0))).reshape(LANES, N_PAIRS, LANES)
    a2p = jnp.pad(a2, ((DECAY_LORA, LANES - DECAY_LORA - AAA_LORA), (0, 0))
                  ).reshape(LANES, N_PAIRS, LANES)
    wa2 = jnp.concatenate([w2p, a2p], axis=2).reshape(LANES, 2 * r_w)
    g2p = jnp.pad(g2, ((0, GATE_PAD - GATE_LORA), (0, 0)))

    x2 = x.reshape(batch * seq, D_MODEL)
    proj, lora = _in_proj(x2, norm_g[None, :], w_all, w_gate, w_lora, cos_t, sin_t,
                          mu_main, mu_lora, seq)
    proj3 = proj.reshape(batch, seq, PROJ_WIDTH)
    lora3 = lora.reshape(batch, seq, LORA_PAD)
    att = _attention(proj3)
    rwk = _rwkv(proj3, lora3, w0[None, :], a0[None, :],
                k_k[None, :], k_a[None, :], r_k.reshape(1, r_w), lnx_g[None, :],
                lnx_b[None, :], wa2, g2p)
    w_out_b = w_out.astype(BF16)
    return att.reshape(batch * seq, a_w), rwk.reshape(batch * seq, r_w), w_out_b, x2


def kernel(x, norm_g, w_in, shift_mu, w0, w2, a0, a2, g2, k_k, k_a, r_k,
           lnx_g, lnx_b, w_out, final_g):
    batch, seq, _ = x.shape
    depth = norm_g.shape[0]
    assert seq % SUPER == 0 and seq % IN_TM == 0 and (batch * seq) % OUT_TM == 0
    cos_t, sin_t = _rope_tables(seq)
    for l in range(depth):
        att2, rwk2, w_out_b, x2 = _layer(
            x, norm_g[l], w_in[l], shift_mu[l], w0[l], w2[l], a0[l], a2[l], g2[l],
            k_k[l], k_a[l], r_k[l], lnx_g[l], lnx_b[l], w_out[l], cos_t, sin_t)
        y2 = _out_proj(att2, rwk2, w_out_b, x2, final_g[None, :],
                       normalize=(l == depth - 1))
        x = y2.reshape(batch, seq, D_MODEL)
    return x
```

```python
import functools

import jax
import jax.numpy as jnp
from jax import lax
from jax.experimental import pallas as pl
from jax.experimental.pallas import tpu as pltpu

F32 = jnp.float32
BF16 = jnp.bfloat16

D_MODEL = 2048
HEAD_DIM = 64
ATT_WIDTH = 1024
RWKV_WIDTH = 1024
DIL_PATTERNS = ((128, 1), (512, 4), (2048, 16))
ROPE_THETA = 10000.0
DECAY_LORA = 64
AAA_LORA = 64
GATE_LORA = 160
NORM_EPS = 1e-5
LNX_EPS = 64e-5
DECAY_SCALE = 0.6065306597126334
KK_NORM_FLOOR = 1e-12
LOG2_E = 1.4426950408889634

LANES = 128
SUBLANES = 8
N_PAIRS = ATT_WIDTH // LANES
N_BACK = 128
Q_BLOCK = 128
SUPER = 2048
CHUNK = 64
RWKV_BLOCK = 1024
N_CHUNKS = RWKV_BLOCK // CHUNK
LORA_PAD = 512
LORA_USED = 384
GATE_PAD = 256

COL_Q, COL_K, COL_V, COL_ZA = 0, 1024, 2048, 3072
COL_R, COL_RK, COL_RV, COL_ZR = 4096, 5120, 6144, 7168
PROJ_WIDTH = 8192

IN_TM, IN_TN = 1024, 1024
IN_SUB = 512
N_ROPE_STEPS = (COL_V - COL_Q) // IN_TN
N_Q_STEPS = (COL_K - COL_Q) // IN_TN
N_ALIGNED_STEPS = COL_ZR // IN_TN
N_SHIFT_LO = COL_R // IN_TN
OUT_TM = 512
VMEM_LIMIT = 56 * 1024 * 1024


def _dot(a, b):
    return jnp.dot(a, b, preferred_element_type=F32)


def _dot_nt(a, b):
    return lax.dot_general(a, b, (((1,), (1,)), ((), ())), preferred_element_type=F32)


def _sigmoid(x):
    return 0.5 * jnp.tanh(0.5 * x) + 0.5


def _bf16_pieces(x, terms):
    pieces = []
    for _ in range(terms):
        piece = x.astype(BF16)
        pieces.append(piece)
        x = x - piece.astype(F32)
    return pieces


def _head_sum(x, head0):
    first = jnp.sum(jnp.where(head0, x, 0.0), axis=-1, keepdims=True)
    second = jnp.sum(jnp.where(head0, 0.0, x), axis=-1, keepdims=True)
    return jnp.where(head0, first, second)


def _in_proj_kernel(x_ref, g_ref, w_ref, wz_ref, wl_ref, cos_ref, sin_ref,
                    mu_ref, mul_ref, o_ref, lora_ref, h_ref, carry_ref, carry_lora_ref,
                    *, pos_blocks):
    i = pl.program_id(0)
    j = pl.program_id(1)
    subs = [slice(s * IN_SUB, (s + 1) * IN_SUB) for s in range(IN_TN // IN_SUB)]
    seq_start = (i % pos_blocks) == 0
    last_row = slice(IN_TM - 1, IN_TM)

    def token_shift(u, carry_row, mu):
        row0 = lax.broadcasted_iota(jnp.int32, u.shape, 0) == 0
        prev = jnp.where(row0, jnp.where(seq_start, 0.0, carry_row), pltpu.roll(u, 1, 0))
        return u + (prev - u) * mu

    @pl.when((i == 0) & (j == 0))
    def _():
        carry_ref[...] = jnp.zeros_like(carry_ref)
        carry_lora_ref[...] = jnp.zeros_like(carry_lora_ref)

    @pl.when(j == 0)
    def _():
        x = x_ref[...]
        ms = jnp.mean(x * x, axis=-1, keepdims=True)
        h_ref[...] = (x * lax.rsqrt(ms + NORM_EPS) * g_ref[...]).astype(BF16)
        lora = _dot(h_ref[...], wl_ref[...])
        lora_ref[...] = token_shift(lora, carry_lora_ref[0:1, :], mul_ref[...])
        carry_lora_ref[0:1, :] = lora[last_row, :]

    @pl.when(j < N_ROPE_STEPS)
    def _():
        reps = IN_SUB // LANES
        cos = jnp.concatenate([cos_ref[...]] * reps, axis=1)
        sin = jnp.concatenate([sin_ref[...]] * reps, axis=1)
        lane = lax.broadcasted_iota(jnp.int32, (IN_TM, IN_SUB), 1)
        first_half = (lane % HEAD_DIM) < (HEAD_DIM // 2)
        scale = jnp.where(j < N_Q_STEPS, LOG2_E * HEAD_DIM ** -0.5, 1.0).astype(F32)
        for sub in subs:
            acc = _dot_nt(h_ref[...], w_ref[sub, :])
            partner = jnp.where(first_half,
                                pltpu.roll(acc, IN_SUB - HEAD_DIM // 2, 1),
                                pltpu.roll(acc, HEAD_DIM // 2, 1))
            o_ref[:, sub] = (acc * cos + partner * sin) * scale

    @pl.when((j >= N_ROPE_STEPS) & (j < N_SHIFT_LO))
    def _():
        for sub in subs:
            o_ref[:, sub] = _dot_nt(h_ref[...], w_ref[sub, :])

    @pl.when((j >= N_SHIFT_LO) & (j < N_ALIGNED_STEPS))
    def _():
        slot = j - N_SHIFT_LO
        for sub in subs:
            acc = _dot_nt(h_ref[...], w_ref[sub, :])
            o_ref[:, sub] = token_shift(acc, carry_ref[slot, 0:1, sub], mu_ref[:, sub])
            carry_ref[slot, 0:1, sub] = acc[last_row, :]

    @pl.when(j >= N_ALIGNED_STEPS)
    def _():
        for sub in subs:
            o_ref[:, sub] = _dot_nt(h_ref[...], wz_ref[sub, :])


def _in_proj(x2, g, w_all, w_gate, w_lora, cos_t, sin_t, mu_main, mu_lora, seq):
    tokens = x2.shape[0]
    pos_blocks = seq // IN_TM
    n_shift = N_ALIGNED_STEPS - N_SHIFT_LO

    def mu_index(i, j):
        return 0, jnp.clip(j - N_SHIFT_LO, 0, n_shift - 1)

    return pl.pallas_call(
        functools.partial(_in_proj_kernel, pos_blocks=pos_blocks),
        grid=(tokens // IN_TM, PROJ_WIDTH // IN_TN),
        in_specs=[
            pl.BlockSpec((IN_TM, D_MODEL), lambda i, j: (i, 0)),
            pl.BlockSpec((1, D_MODEL), lambda i, j: (0, 0)),
            pl.BlockSpec((IN_TN, D_MODEL), lambda i, j: (jnp.minimum(j, N_ALIGNED_STEPS - 1), 0)),
            pl.BlockSpec((IN_TN, D_MODEL), lambda i, j: (0, 0)),
            pl.BlockSpec((D_MODEL, LORA_PAD), lambda i, j: (0, 0)),
            pl.BlockSpec((IN_TM, LANES), lambda i, j: (i % pos_blocks, 0)),
            pl.BlockSpec((IN_TM, LANES), lambda i, j: (i % pos_blocks, 0)),
            pl.BlockSpec((1, IN_TN), mu_index),
            pl.BlockSpec((1, LORA_PAD), lambda i, j: (0, 0)),
        ],
        out_specs=[
            pl.BlockSpec((IN_TM, IN_TN), lambda i, j: (i, j)),
            pl.BlockSpec((IN_TM, LORA_PAD), lambda i, j: (i, 0)),
        ],
        out_shape=[
            jax.ShapeDtypeStruct((tokens, PROJ_WIDTH), F32),
            jax.ShapeDtypeStruct((tokens, LORA_PAD), F32),
        ],
        scratch_shapes=[
            pltpu.VMEM((IN_TM, D_MODEL), BF16),
            pltpu.VMEM((n_shift, SUBLANES, IN_TN), F32),
            pltpu.VMEM((SUBLANES, LORA_PAD), F32),
        ],
        compiler_params=pltpu.CompilerParams(
            dimension_semantics=("arbitrary", "arbitrary"),
            vmem_limit_bytes=VMEM_LIMIT),
        name="in_proj",
    )(x2, g, w_all, w_gate, w_lora, cos_t, sin_t, mu_main, mu_lora)


def _attention_kernel(q_ref, kc_ref, kp_ref, vc_ref, vp_ref, z_ref, o_ref,
                      k4, v4, acc_s, m_s, den_s):
    i = pl.program_id(2)
    slab = 2 * SUPER // 4
    for b in range(4):
        for dst, prev_ref, cur_ref in ((k4, kp_ref, kc_ref), (v4, vp_ref, vc_ref)):
            dst[b * slab:b * slab + slab // 2, :] = prev_ref[0, pl.ds(b, slab // 2, stride=4), :]
            dst[b * slab + slab // 2:(b + 1) * slab, :] = cur_ref[0, pl.ds(b, slab // 2,
                                                                          stride=4), :]

    qi = lax.broadcasted_iota(jnp.int32, (Q_BLOCK, 2 * Q_BLOCK), 0)
    ki = lax.broadcasted_iota(jnp.int32, (Q_BLOCK, 2 * Q_BLOCK), 1)
    rel = Q_BLOCK + qi - ki
    band = (rel >= 0) & (rel <= N_BACK)
    cur = ki >= Q_BLOCK
    lane = lax.broadcasted_iota(jnp.int32, (Q_BLOCK, LANES), 1)
    head0 = lane < HEAD_DIM

    def rows(start, size, stride):
        return pl.ds(start, size, stride=stride) if stride > 1 else pl.ds(start, size)

    def window(buf4, cur_ref, prev_ref, q_start, dil):
        k_start = SUPER + q_start - Q_BLOCK * dil
        if dil > 1:
            return buf4[rows((k_start % 4) * slab + k_start // 4, 2 * Q_BLOCK, dil // 4), :]
        if k_start >= SUPER:
            return cur_ref[0, pl.ds(k_start - SUPER, 2 * Q_BLOCK), :]
        return jnp.concatenate([prev_ref[0, pl.ds(k_start, Q_BLOCK), :],
                                cur_ref[0, pl.ds(q_start, Q_BLOCK), :]], axis=0)

    def block(pat, dil, q_start):
        rows_q = rows(q_start, Q_BLOCK, dil)
        qs = q_ref[0, rows_q, :]
        kw = window(k4, kc_ref, kp_ref, q_start, dil).astype(BF16)
        vw = window(v4, vc_ref, vp_ref, q_start, dil).astype(BF16)
        if q_start >= Q_BLOCK * dil:
            valid = band
        else:
            valid = band & (cur | (i > 0))
        accs, ms, dens = [], [], []
        for hm in (head0, ~head0):
            qh = jnp.where(hm, qs, 0.0).astype(BF16)
            s = _dot_nt(qh, kw)
            s = jnp.where(valid, s, -jnp.inf)
            m = jnp.max(s, axis=-1, keepdims=True)
            p = jnp.exp2(s - m)
            dens.append(jnp.sum(p, axis=-1, keepdims=True))
            ms.append(m)
            accs.append(_dot(p.astype(BF16), vw))
        acc_s[pat, rows_q, :] = jnp.where(head0, accs[0], accs[1])
        m_s[pat, rows_q, :] = jnp.where(head0, ms[0], ms[1])
        den_s[pat, rows_q, :] = jnp.where(head0, dens[0], dens[1])

    for pat, (window_len, dil) in enumerate(DIL_PATTERNS):
        span = Q_BLOCK * dil
        for s_idx in range(SUPER // span):
            for r in range(dil):
                block(pat, dil, s_idx * span + r)

    m0, m1, m2 = m_s[0], m_s[1], m_s[2]
    mx = jnp.maximum(jnp.maximum(m0, m1), m2)
    e0, e1, e2 = jnp.exp2(m0 - mx), jnp.exp2(m1 - mx), jnp.exp2(m2 - mx)
    num = e0 * acc_s[0] + e1 * acc_s[1] + e2 * acc_s[2]
    den = e0 * den_s[0] + e1 * den_s[1] + e2 * den_s[2]
    z = z_ref[0]
    o_ref[0] = (num / den * (z * _sigmoid(z))).astype(o_ref.dtype)


def _attention(proj3):
    batch, seq, _ = proj3.shape
    blk = (1, SUPER, LANES)

    def col(base):
        return lambda b, p, i: (b, i, base // LANES + p)

    def col_prev(base):
        return lambda b, p, i: (b, jnp.maximum(i - 1, 0), base // LANES + p)

    return pl.pallas_call(
        _attention_kernel,
        grid=(batch, N_PAIRS, seq // SUPER),
        in_specs=[
            pl.BlockSpec(blk, col(COL_Q)),
            pl.BlockSpec(blk, col(COL_K)),
            pl.BlockSpec(blk, col_prev(COL_K)),
            pl.BlockSpec(blk, col(COL_V)),
            pl.BlockSpec(blk, col_prev(COL_V)),
            pl.BlockSpec(blk, col(COL_ZA)),
        ],
        out_specs=pl.BlockSpec(blk, lambda b, p, i: (b, i, p)),
        out_shape=jax.ShapeDtypeStruct((batch, seq, ATT_WIDTH), BF16),
        scratch_shapes=[
            pltpu.VMEM((2 * SUPER, LANES), F32),
            pltpu.VMEM((2 * SUPER, LANES), F32),
            pltpu.VMEM((len(DIL_PATTERNS), SUPER, LANES), F32),
            pltpu.VMEM((len(DIL_PATTERNS), SUPER, LANES), F32),
            pltpu.VMEM((len(DIL_PATTERNS), SUPER, LANES), F32),
        ],
        compiler_params=pltpu.CompilerParams(
            dimension_semantics=("arbitrary", "arbitrary", "arbitrary"),
            vmem_limit_bytes=VMEM_LIMIT),
        name="dilated_attention",
    )(proj3, proj3, proj3, proj3, proj3, proj3)


def _rwkv_kernel(r_ref, k_ref, v_ref, z_ref, lora_ref,
                 w0_ref, a0_ref, kk_ref, ka_ref, wa2_ref, g2_ref,
                 rk_ref, lng_ref, lnb_ref,
                 o_ref,
                 state,
                 pm_a, pm_r, pm_kb, pm_bk, pm_d, pm_v,
                 pc_r, pc_k, pc_v, pc_gz,
                 mc_rw, mc_y0, mc_m, mc_g, mc_d,
                 *, blocks_per_stream):
    t = pl.program_id(0)
    first_chain = ((2 * jnp.maximum(t - 1, 0)) % blocks_per_stream) == 0

    @pl.when(t == 0)
    def _():
        for ref in (state, pm_a, pm_r, pm_kb, pm_bk, pm_d, pm_v,
                    pc_r, pc_k, pc_v, pc_gz, mc_rw, mc_y0, mc_m, mc_g, mc_d):
            ref[...] = jnp.zeros_like(ref)

    rr = lax.broadcasted_iota(jnp.int32, (LANES, LANES), 0)
    cc = lax.broadcasted_iota(jnp.int32, (LANES, LANES), 1)
    same_head = (rr // HEAD_DIM) == (cc // HEAD_DIM)
    strict = same_head & ((cc % CHUNK) < (rr % CHUNK))
    incl = same_head & ((cc % CHUNK) <= (rr % CHUNK))
    eye = (rr == cc).astype(F32)
    tril = (lax.broadcasted_iota(jnp.int32, (CHUNK, CHUNK), 1)
            <= lax.broadcasted_iota(jnp.int32, (CHUNK, CHUNK), 0)).astype(BF16)
    head0 = lax.broadcasted_iota(jnp.int32, (CHUNK, LANES), 1) < HEAD_DIM
    decay_lane = lax.broadcasted_iota(jnp.int32, (CHUNK, LANES), 1) < DECAY_LORA
    zero_blk = jnp.zeros((LANES, LANES), BF16)
    last = slice(CHUNK - 1, CHUNK)

    def bd(x):
        return jnp.concatenate([jnp.where(head0, x, 0.0), jnp.where(head0, 0.0, x)], axis=0)

    rows = [slice(c * CHUNK, (c + 1) * CHUNK) for c in range(N_CHUNKS)]
    cur = {"half": 0, "slot_p": 0, "slot_m": 1}

    def block_rows(c):
        start = cur["half"] * RWKV_BLOCK + c * CHUNK
        return slice(start, start + CHUNK)

    pv = [dict() for _ in range(N_CHUNKS)]

    def prep_lora_acts(c):
        lora = lora_ref[0, block_rows(c), 0:LORA_USED]
        wa = lora[:, 0:LANES]
        pv[c]["wa_act"] = jnp.where(decay_lane, jnp.tanh(wa), wa).astype(BF16)
        pv[c]["gate_act"] = _sigmoid(lora[:, LANES:LANES + GATE_PAD]).astype(BF16)

    def prep_lora_matmuls(c):
        d = pv[c]
        wa_out = _dot(d["wa_act"], wa2_ref[...].astype(BF16))
        d["lw"] = -(DECAY_SCALE * _sigmoid(w0_ref[...] + wa_out[:, 0:LANES]))
        d["a"] = _sigmoid(a0_ref[...] + wa_out[:, LANES:])
        g = _dot(d["gate_act"], g2_ref[...].astype(BF16))
        z = z_ref[0, block_rows(c), :]
        d["gz"] = g * (z * _sigmoid(z))

    def prep_key_norm(c):
        d = pv[c]
        k = k_ref[0, block_rows(c), :]
        kk = k * kk_ref[...]
        kk = kk * lax.rsqrt(jnp.maximum(_head_sum(kk * kk, head0), KK_NORM_FLOOR ** 2))
        d["kk"] = kk
        d["k2"] = k * (1.0 + (d["a"] - 1.0) * ka_ref[...])
        d["b"] = kk * d["a"]
        parts = _dot(tril, jnp.concatenate(_bf16_pieces(d["lw"], 3), axis=1))
        d["cs"] = parts[:, 0:LANES] + parts[:, LANES:2 * LANES] + parts[:, 2 * LANES:]

    def prep_exp(c):
        d = pv[c]
        d["e_pos"] = jnp.exp(d["cs"])
        d["e_neg"] = jnp.exp(-d["cs"])
        d["e_excl"] = jnp.exp(d["cs"] - d["lw"])
        d["decay_end"] = d["e_pos"][last, :]

    def prep_store_ar(c):
        d = pv[c]
        idx = cur["slot_p"] * N_CHUNKS + c
        pm_a[idx] = bd(-d["kk"] * d["e_excl"]).astype(BF16)
        pm_r[idx] = bd(r_ref[0, block_rows(c), :] * d["e_pos"])

    def prep_store_kbv(c):
        d = pv[c]
        slot_p = cur["slot_p"]
        idx = slot_p * N_CHUNKS + c
        pm_kb[idx] = jnp.concatenate([d["k2"] * d["e_neg"], d["b"] * d["e_neg"]],
                                     axis=0).astype(BF16)
        pm_v[idx] = bd(v_ref[0, block_rows(c), :]).astype(BF16)

    def prep_store_pc(c):
        d = pv[c]
        slot_p = cur["slot_p"]
        pc_r[slot_p, rows[c], :] = r_ref[0, block_rows(c), :]
        pc_k[slot_p, rows[c], :] = d["k2"]
        pc_v[slot_p, rows[c], :] = v_ref[0, block_rows(c), :]
        pc_gz[slot_p, rows[c], :] = d["gz"]

    def prep_store_bk(c):
        d = pv[c]
        idx = cur["slot_p"] * N_CHUNKS + c
        e_end = d["decay_end"] * d["e_neg"]
        bkd_t = jnp.concatenate([bd(d["b"] * e_end), bd(d["k2"] * e_end),
                                 jnp.broadcast_to(d["decay_end"], (LANES, LANES))], axis=0).T
        pm_bk[idx] = bkd_t[:, 0:2 * LANES].astype(BF16)
        pm_d[idx] = bkd_t[:, 2 * LANES:]

    prep_groups = [(prep_lora_acts, 110), (prep_lora_matmuls, 110),
                   (prep_key_norm, 150), (prep_exp, 50), (prep_store_ar, 56),
                   (prep_store_kbv, 50), (prep_store_bk, 75)]

    mm = [dict() for _ in range(N_CHUNKS)]

    def mm_scores(c):
        idx = cur["slot_m"] * N_CHUNKS + c
        kb = pm_kb[idx]
        k_t, b_t = kb[0:CHUNK, :], kb[CHUNK:, :]
        lhs = jnp.concatenate([pm_a[idx], pm_r[idx].astype(BF16)], axis=0)
        x = _dot_nt(lhs, jnp.concatenate([k_t, k_t, b_t, b_t], axis=0))
        d = mm[c]
        d["a_ak"] = jnp.where(strict, x[0:LANES, 0:LANES], 0.0).astype(BF16)
        a_ab = jnp.where(strict, x[0:LANES, LANES:], 0.0)
        d["a_r"] = jnp.concatenate([jnp.where(incl, x[LANES:, 0:LANES], 0.0),
                                    jnp.where(incl, x[LANES:, LANES:], 0.0)],
                                   axis=1).astype(BF16)
        d["q"] = a_ab.astype(BF16)
        d["p"] = eye + a_ab

    def mm_square(c):
        d = mm[c]
        d["q"] = _dot(d["q"], d["q"]).astype(BF16)

    def mm_double(c):
        d = mm[c]
        qp = _dot(d["q"], jnp.concatenate([d["q"], d["p"].astype(BF16)], axis=1))
        d["q"] = qp[:, 0:LANES].astype(BF16)
        d["p"] = d["p"] + qp[:, LANES:]

    def mm_inverse(c):
        d = mm[c]
        d["t"] = (d["p"] + _dot(d["q"], d["p"].astype(BF16))).astype(BF16)

    def mm_akv(c):
        d = mm[c]
        d["ak_v"] = _dot(d["a_ak"], pm_v[cur["slot_m"] * N_CHUNKS + c]).astype(BF16)

    def mm_wu(c):
        d = mm[c]
        rhs = jnp.concatenate([pm_a[cur["slot_m"] * N_CHUNKS + c], d["ak_v"]], axis=1)
        d["wu"] = _dot(d["t"], rhs).astype(BF16)

    def mm_yw(c):
        d = mm[c]
        idx = cur["slot_m"] * N_CHUNKS + c
        wu = d["wu"]
        rhs = jnp.concatenate(
            [jnp.concatenate([pm_v[idx], zero_blk], axis=1),
             jnp.concatenate([wu[:, LANES:], wu[:, 0:LANES]], axis=1)], axis=0)
        yw = _dot(d["a_r"], rhs)
        mc_y0[idx] = yw[:, 0:LANES]
        mc_rw[idx] = (pm_r[idx] + yw[:, LANES:]).astype(BF16)

    def mm_mg(c):
        idx = cur["slot_m"] * N_CHUNKS + c
        rhs = jnp.concatenate([mm[c]["wu"],
                               jnp.concatenate([zero_blk, pm_v[idx]], axis=1)], axis=0)
        mg = _dot(pm_bk[idx], rhs)
        mc_m[idx] = mg[:, 0:LANES].astype(BF16)
        mc_g[idx] = mg[:, LANES:]
        mc_d[idx] = pm_d[idx]

    matmul_stages = [mm_scores, mm_square, mm_double, mm_double, mm_double, mm_double,
                     mm_inverse, mm_akv, mm_wu, mm_yw, mm_mg]

    chain = {"hs": jnp.where(first_chain, 0.0, state[...])}

    cv = [dict() for _ in range(N_CHUNKS)]

    def chain_core(c):
        idx = cur["slot_p"] * N_CHUNKS + c
        hs = chain["hs"]
        hs_b = hs.astype(BF16)
        y = _dot(mc_rw[idx], hs_b) + mc_y0[idx]
        chain["hs"] = hs * mc_d[idx] + _dot(mc_m[idx], hs_b) + mc_g[idx]
        cv[c]["y"] = y[0:CHUNK, :] + y[CHUNK:, :]

    def chain_mean(c):
        slot_p, rws, d = cur["slot_p"], rows[c], cv[c]
        d["yc"] = d["y"] - _head_sum(d["y"], head0) * (1.0 / HEAD_DIM)
        d["bonus"] = _head_sum(pc_r[slot_p, rws, :] * pc_k[slot_p, rws, :] * rk_ref[...],
                               head0) * pc_v[slot_p, rws, :]

    def chain_finish(c):
        slot_p, rws, d = cur["slot_p"], rows[c], cv[c]
        var = _head_sum(d["yc"] * d["yc"], head0) * (1.0 / HEAD_DIM)
        yn = d["yc"] * lax.rsqrt(var + LNX_EPS) * lng_ref[...] + lnb_ref[...]
        o_ref[0, block_rows(c), :] = ((yn + d["bonus"]) * pc_gz[slot_p, rws, :]
                                      ).astype(o_ref.dtype)

    n_rounds = len(matmul_stages)
    per_round = -(-N_CHUNKS // (n_rounds - 3))
    core_gap = N_CHUNKS // per_round

    def chain_cores_of(rnd):
        return range(min(rnd * per_round, N_CHUNKS), min((rnd + 1) * per_round, N_CHUNKS))

    n_slots = (n_rounds - 1) * N_CHUNKS
    total = float(N_CHUNKS * sum(w for _, w in prep_groups))
    prep_slots = [[] for _ in range(n_slots)]
    done = 0.0
    for fn, weight in prep_groups:
        for c in range(N_CHUNKS):
            prep_slots[min(int(done / total * n_slots), n_slots - 1)].append((fn, c))
            done += weight
    for half in (0, 1):
        cur.update(half=half, slot_p=half, slot_m=1 - half)
        for rnd, mm_stage in enumerate(matmul_stages):
            if rnd >= 1:
                for chunk in chain_cores_of(rnd - 1):
                    chain_mean(chunk)
            if rnd >= 2:
                for chunk in chain_cores_of(rnd - 2):
                    chain_finish(chunk)
            cores = list(chain_cores_of(rnd))
            for c in range(N_CHUNKS):
                if c % core_gap == 0 and c // core_gap < len(cores):
                    chain_core(cores[c // core_gap])
                mm_stage(c)
                if rnd == n_rounds - 1:
                    prep_store_pc(c)
                else:
                    for fn, pc in prep_slots[rnd * N_CHUNKS + c]:
                        fn(pc)
    state[...] = chain["hs"]


def _rwkv(proj3, lora3, w0, a0, k_k, k_a, r_k, lnx_g, lnx_b, wa2, g2p):
    batch, seq, _ = proj3.shape
    bps = seq // RWKV_BLOCK
    assert bps % 2 == 0
    wps = bps // 2
    n_total = batch * N_PAIRS * wps
    blk = (1, 2 * RWKV_BLOCK, LANES)

    def split(n):
        return n // (N_PAIRS * wps), n % wps, (n // wps) % N_PAIRS

    def prep_n(s):
        return jnp.minimum(s, n_total - 1)

    def chain_n(s):
        return jnp.maximum(s - 1, 0)

    def col(base):
        def index(s):
            b, i, p = split(prep_n(s))
            return b, i, base // LANES + p
        return index

    def lora_index(s):
        b, i, _ = split(prep_n(s))
        return b, i, 0

    def out_index(s):
        return split(chain_n(s))

    vec_prep = pl.BlockSpec((1, LANES), lambda s: (0, split(prep_n(s))[2]))
    vec_chain = pl.BlockSpec((1, LANES), lambda s: (0, split(chain_n(s))[2]))
    n_slots = 2 * N_CHUNKS
    return pl.pallas_call(
        functools.partial(_rwkv_kernel, blocks_per_stream=bps),
        grid=(n_total + 1,),
        in_specs=[
            pl.BlockSpec(blk, col(COL_R)),
            pl.BlockSpec(blk, col(COL_RK)),
            pl.BlockSpec(blk, col(COL_RV)),
            pl.BlockSpec(blk, col(COL_ZR)),
            pl.BlockSpec((1, 2 * RWKV_BLOCK, LORA_PAD), lora_index),
            vec_prep, vec_prep, vec_prep, vec_prep,
            pl.BlockSpec((LANES, 2 * LANES), lambda s: (0, split(prep_n(s))[2])),
            pl.BlockSpec((GATE_PAD, LANES), lambda s: (0, split(prep_n(s))[2])),
            vec_chain, vec_chain, vec_chain,
        ],
        out_specs=pl.BlockSpec(blk, out_index),
        out_shape=jax.ShapeDtypeStruct((batch, seq, RWKV_WIDTH), BF16),
        scratch_shapes=[
            pltpu.VMEM((LANES, LANES), F32),
            pltpu.VMEM((n_slots, LANES, LANES), BF16),
            pltpu.VMEM((n_slots, LANES, LANES), F32),
            pltpu.VMEM((n_slots, LANES, LANES), BF16),
            pltpu.VMEM((n_slots, LANES, 2 * LANES), BF16),
            pltpu.VMEM((n_slots, LANES, LANES), F32),
            pltpu.VMEM((n_slots, LANES, LANES), BF16),
            pltpu.VMEM((2, RWKV_BLOCK, LANES), F32),
            pltpu.VMEM((2, RWKV_BLOCK, LANES), F32),
            pltpu.VMEM((2, RWKV_BLOCK, LANES), F32),
            pltpu.VMEM((2, RWKV_BLOCK, LANES), F32),
            pltpu.VMEM((n_slots, LANES, LANES), BF16),
            pltpu.VMEM((n_slots, LANES, LANES), F32),
            pltpu.VMEM((n_slots, LANES, LANES), BF16),
            pltpu.VMEM((n_slots, LANES, LANES), F32),
            pltpu.VMEM((n_slots, LANES, LANES), F32),
        ],
        compiler_params=pltpu.CompilerParams(
            dimension_semantics=("arbitrary",),
            vmem_limit_bytes=VMEM_LIMIT),
        name="rwkv7_time_mix",
    )(proj3, proj3, proj3, proj3, lora3, w0, a0, k_k, k_a, wa2, g2p, r_k, lnx_g, lnx_b)


def _out_proj_kernel(att_ref, rwk_ref, wa_ref, wr_ref, x_ref, g_ref, o_ref, *, normalize):
    y = x_ref[...] + _dot(att_ref[...], wa_ref[...]) + _dot(rwk_ref[...], wr_ref[...])
    if normalize:
        ms = jnp.mean(y * y, axis=-1, keepdims=True)
        y = y * lax.rsqrt(ms + NORM_EPS) * g_ref[...]
    o_ref[...] = y


def _out_proj(att2, rwk2, w_out_b, x2, g, normalize):
    tokens = x2.shape[0]
    return pl.pallas_call(
        functools.partial(_out_proj_kernel, normalize=normalize),
        grid=(tokens // OUT_TM,),
        in_specs=[
            pl.BlockSpec((OUT_TM, ATT_WIDTH), lambda i: (i, 0)),
            pl.BlockSpec((OUT_TM, RWKV_WIDTH), lambda i: (i, 0)),
            pl.BlockSpec((ATT_WIDTH, D_MODEL), lambda i: (0, 0)),
            pl.BlockSpec((RWKV_WIDTH, D_MODEL), lambda i: (ATT_WIDTH // RWKV_WIDTH, 0)),
            pl.BlockSpec((OUT_TM, D_MODEL), lambda i: (i, 0)),
            pl.BlockSpec((1, D_MODEL), lambda i: (0, 0)),
        ],
        out_specs=pl.BlockSpec((OUT_TM, D_MODEL), lambda i: (i, 0)),
        out_shape=jax.ShapeDtypeStruct((tokens, D_MODEL), F32),
        compiler_params=pltpu.CompilerParams(
            dimension_semantics=("arbitrary",),
            vmem_limit_bytes=VMEM_LIMIT),
        name="out_proj",
    )(att2, rwk2, w_out_b, w_out_b, x2, g)


def _rope_tables(seq):
    inv_freq = ROPE_THETA ** (-jnp.arange(0, HEAD_DIM, 2, dtype=F32) / HEAD_DIM)
    ang = jnp.arange(seq, dtype=jnp.int32).astype(F32)[:, None] * inv_freq[None, :]
    cos, sin = jnp.cos(ang), jnp.sin(ang)
    reps = LANES // HEAD_DIM
    cos_t = jnp.tile(jnp.concatenate([cos, cos], axis=1), (1, reps))
    sin_t = jnp.tile(jnp.concatenate([-sin, sin], axis=1), (1, reps))
    return cos_t, sin_t


def _layer(x, norm_g, w_in, shift_mu, w0, w2, a0, a2, g2, k_k, k_a, r_k,
           lnx_g, lnx_b, w_out, cos_t, sin_t):
    batch, seq, _ = x.shape
    a_w, r_w = ATT_WIDTH, RWKV_WIDTH
    lo = 4 * a_w + 3 * r_w
    n_lora = DECAY_LORA + AAA_LORA + GATE_LORA
    assert lo == COL_ZR
    w_all = jnp.transpose(w_in).astype(BF16)
    w_gate = w_all[lo + n_lora:, :]
    w_lora = jnp.pad(w_in[:, lo:lo + n_lora], ((0, 0), (0, LORA_PAD - n_lora))).astype(BF16)
    mu_main = shift_mu[None, 0:3 * r_w]
    mu_lora = jnp.pad(shift_mu[None, 3 * r_w:], ((0, 0), (0, LORA_PAD - n_lora)))
    w2p = jnp.pad(w2, ((0, LANES - DECAY_LORA), (0, 0))).reshape(LANES, N_PAIRS, LANES)
    a2p = jnp.pad(a2, ((DECAY_LORA, LANES - DECAY_LORA - AAA_LORA), (0, 0))
                  ).reshape(LANES, N_PAIRS, LANES)
    wa2 = jnp.concatenate([w2p, a2p], axis=2).reshape(LANES, 2 * r_w)
    g2p = jnp.pad(g2, ((0, GATE_PAD - GATE_LORA), (0, 0)))

    x2 = x.reshape(batch * seq, D_MODEL)
    proj, lora = _in_proj(x2, norm_g[None, :], w_all, w_gate, w_lora, cos_t, sin_t,
                          mu_main, mu_lora, seq)
    proj3 = proj.reshape(batch, seq, PROJ_WIDTH)
    lora3 = lora.reshape(batch, seq, LORA_PAD)
    att = _attention(proj3)
    rwk = _rwkv(proj3, lora3, w0[None, :], a0[None, :],
                k_k[None, :], k_a[None, :], r_k.reshape(1, r_w), lnx_g[None, :],
                lnx_b[None, :], wa2, g2p)
    w_out_b = w_out.astype(BF16)
    return att.reshape(batch * seq, a_w), rwk.reshape(batch * seq, r_w), w_out_b, x2


def kernel(x, norm_g, w_in, shift_mu, w0, w2, a0, a2, g2, k_k, k_a, r_k,
           lnx_g, lnx_b, w_out, final_g):
    batch, seq, _ = x.shape
    depth = norm_g.shape[0]
    assert seq % SUPER == 0 and seq % IN_TM == 0 and (batch * seq) % OUT_TM == 0
    cos_t, sin_t = _rope_tables(seq)
    for l in range(depth):
        att2, rwk2, w_out_b, x2 = _layer(
            x, norm_g[l], w_in[l], shift_mu[l], w0[l], w2[l], a0[l], a2[l], g2[l],
            k_k[l], k_a[l], r_k[l], lnx_g[l], lnx_b[l], w_out[l], cos_t, sin_t)
        y2 = _out_proj(att2, rwk2, w_out_b, x2, final_g[None, :],
                       normalize=(l == depth - 1))
        x = y2.reshape(batch, seq, D_MODEL)
    return x
```

```python
import functools

import jax
import jax.numpy as jnp
from jax import lax
from jax.experimental import pallas as pl
from jax.experimental.pallas import tpu as pltpu

F32 = jnp.float32
BF16 = jnp.bfloat16

D_MODEL = 2048
HEAD_DIM = 64
ATT_WIDTH = 1024
RWKV_WIDTH = 1024
DIL_PATTERNS = ((128, 1), (512, 4), (2048, 16))
ROPE_THETA = 10000.0
DECAY_LORA = 64
AAA_LORA = 64
GATE_LORA = 160
NORM_EPS = 1e-5
LNX_EPS = 64e-5
DECAY_SCALE = 0.6065306597126334
KK_NORM_FLOOR = 1e-12
LOG2_E = 1.4426950408889634

LANES = 128
SUBLANES = 8
N_PAIRS = ATT_WIDTH // LANES
N_BACK = 128
Q_BLOCK = 128
SUPER = 2048
CHUNK = 64
RWKV_BLOCK = 1024
N_CHUNKS = RWKV_BLOCK // CHUNK
LORA_PAD = 512
LORA_USED = 384
GATE_PAD = 256

COL_Q, COL_K, COL_V, COL_ZA = 0, 1024, 2048, 3072
COL_R, COL_RK, COL_RV, COL_ZR = 4096, 5120, 6144, 7168
PROJ_WIDTH = 8192

IN_TM, IN_TN = 1024, 1024
IN_SUB = 512
N_ROPE_STEPS = (COL_V - COL_Q) // IN_TN
N_Q_STEPS = (COL_K - COL_Q) // IN_TN
N_ALIGNED_STEPS = COL_ZR // IN_TN
N_SHIFT_LO = COL_R // IN_TN
OUT_TM = 512
VMEM_LIMIT = 56 * 1024 * 1024


def _dot(a, b):
    return jnp.dot(a, b, preferred_element_type=F32)


def _dot_nt(a, b):
    return lax.dot_general(a, b, (((1,), (1,)), ((), ())), preferred_element_type=F32)


def _sigmoid(x):
    return 0.5 * jnp.tanh(0.5 * x) + 0.5


def _bf16_pieces(x, terms):
    pieces = []
    for _ in range(terms):
        piece = x.astype(BF16)
        pieces.append(piece)
        x = x - piece.astype(F32)
    return pieces


def _head_sum(x, head0):
    first = jnp.sum(jnp.where(head0, x, 0.0), axis=-1, keepdims=True)
    second = jnp.sum(jnp.where(head0, 0.0, x), axis=-1, keepdims=True)
    return jnp.where(head0, first, second)


def _in_proj_kernel(x_ref, g_ref, w_ref, wz_ref, wl_ref, cos_ref, sin_ref,
                    mu_ref, mul_ref, o_ref, lora_ref, h_ref, carry_ref, carry_lora_ref,
                    *, pos_blocks):
    i = pl.program_id(0)
    j = pl.program_id(1)
    subs = [slice(s * IN_SUB, (s + 1) * IN_SUB) for s in range(IN_TN // IN_SUB)]
    seq_start = (i % pos_blocks) == 0
    last_row = slice(IN_TM - 1, IN_TM)

    def token_shift(u, carry_row, mu):
        row0 = lax.broadcasted_iota(jnp.int32, u.shape, 0) == 0
        prev = jnp.where(row0, jnp.where(seq_start, 0.0, carry_row), pltpu.roll(u, 1, 0))
        return u + (prev - u) * mu

    @pl.when((i == 0) & (j == 0))
    def _():
        carry_ref[...] = jnp.zeros_like(carry_ref)
        carry_lora_ref[...] = jnp.zeros_like(carry_lora_ref)

    @pl.when(j == 0)
    def _():
        x = x_ref[...]
        ms = jnp.mean(x * x, axis=-1, keepdims=True)
        h_ref[...] = (x * lax.rsqrt(ms + NORM_EPS) * g_ref[...]).astype(BF16)
        lora = _dot(h_ref[...], wl_ref[...])
        lora_ref[...] = token_shift(lora, carry_lora_ref[0:1, :], mul_ref[...])
        carry_lora_ref[0:1, :] = lora[last_row, :]

    @pl.when(j < N_ROPE_STEPS)
    def _():
        reps = IN_SUB // LANES
        cos = jnp.concatenate([cos_ref[...]] * reps, axis=1)
        sin = jnp.concatenate([sin_ref[...]] * reps, axis=1)
        lane = lax.broadcasted_iota(jnp.int32, (IN_TM, IN_SUB), 1)
        first_half = (lane % HEAD_DIM) < (HEAD_DIM // 2)
        scale = jnp.where(j < N_Q_STEPS, LOG2_E * HEAD_DIM ** -0.5, 1.0).astype(F32)
        for sub in subs:
            acc = _dot(h_ref[...], w_ref[:, sub])
            partner = jnp.where(first_half,
                                pltpu.roll(acc, IN_SUB - HEAD_DIM // 2, 1),
                                pltpu.roll(acc, HEAD_DIM // 2, 1))
            o_ref[:, sub] = (acc * cos + partner * sin) * scale

    @pl.when((j >= N_ROPE_STEPS) & (j < N_SHIFT_LO))
    def _():
        for sub in subs:
            o_ref[:, sub] = _dot(h_ref[...], w_ref[:, sub])

    @pl.when((j >= N_SHIFT_LO) & (j < N_ALIGNED_STEPS))
    def _():
        slot = j - N_SHIFT_LO
        for sub in subs:
            acc = _dot(h_ref[...], w_ref[:, sub])
            o_ref[:, sub] = token_shift(acc, carry_ref[slot, 0:1, sub], mu_ref[:, sub])
            carry_ref[slot, 0:1, sub] = acc[last_row, :]

    @pl.when(j >= N_ALIGNED_STEPS)
    def _():
        for sub in subs:
            o_ref[:, sub] = _dot(h_ref[...], wz_ref[:, sub])


def _in_proj(x2, g, w_all, w_gate, w_lora, cos_t, sin_t, mu_main, mu_lora, seq):
    tokens = x2.shape[0]
    pos_blocks = seq // IN_TM
    n_shift = N_ALIGNED_STEPS - N_SHIFT_LO

    def mu_index(i, j):
        return 0, jnp.clip(j - N_SHIFT_LO, 0, n_shift - 1)

    return pl.pallas_call(
        functools.partial(_in_proj_kernel, pos_blocks=pos_blocks),
        grid=(tokens // IN_TM, PROJ_WIDTH // IN_TN),
        in_specs=[
            pl.BlockSpec((IN_TM, D_MODEL), lambda i, j: (i, 0)),
            pl.BlockSpec((1, D_MODEL), lambda i, j: (0, 0)),
            pl.BlockSpec((D_MODEL, IN_TN), lambda i, j: (0, jnp.minimum(j, N_ALIGNED_STEPS - 1))),
            pl.BlockSpec((D_MODEL, IN_TN), lambda i, j: (0, 0)),
            pl.BlockSpec((D_MODEL, LORA_PAD), lambda i, j: (0, 0)),
            pl.BlockSpec((IN_TM, LANES), lambda i, j: (i % pos_blocks, 0)),
            pl.BlockSpec((IN_TM, LANES), lambda i, j: (i % pos_blocks, 0)),
            pl.BlockSpec((1, IN_TN), mu_index),
            pl.BlockSpec((1, LORA_PAD), lambda i, j: (0, 0)),
        ],
        out_specs=[
            pl.BlockSpec((IN_TM, IN_TN), lambda i, j: (i, j)),
            pl.BlockSpec((IN_TM, LORA_PAD), lambda i, j: (i, 0)),
        ],
        out_shape=[
            jax.ShapeDtypeStruct((tokens, PROJ_WIDTH), F32),
            jax.ShapeDtypeStruct((tokens, LORA_PAD), F32),
        ],
        scratch_shapes=[
            pltpu.VMEM((IN_TM, D_MODEL), BF16),
            pltpu.VMEM((n_shift, SUBLANES, IN_TN), F32),
            pltpu.VMEM((SUBLANES, LORA_PAD), F32),
        ],
        compiler_params=pltpu.CompilerParams(
            dimension_semantics=("arbitrary", "arbitrary"),
            vmem_limit_bytes=VMEM_LIMIT),
        name="in_proj",
    )(x2, g, w_all, w_gate, w_lora, cos_t, sin_t, mu_main, mu_lora)


def _attention_kernel(q_ref, kc_ref, kp_ref, vc_ref, vp_ref, z_ref, o_ref,
                      k4, v4, q4, acc_s, m_s, den_s, res):
    i = pl.program_id(2)
    slab = 2 * SUPER // 4
    qslab = SUPER // 4
    for b in range(4):
        q4[b * qslab:(b + 1) * qslab, :] = q_ref[0, pl.ds(b, qslab, stride=4), :]
        for dst, prev_ref, cur_ref in ((k4, kp_ref, kc_ref), (v4, vp_ref, vc_ref)):
            dst[b * slab:b * slab + slab // 2, :] = prev_ref[0, pl.ds(b, slab // 2, stride=4), :]
            dst[b * slab + slab // 2:(b + 1) * slab, :] = cur_ref[0, pl.ds(b, slab // 2,
                                                                          stride=4), :]

    qi = lax.broadcasted_iota(jnp.int32, (Q_BLOCK, 2 * Q_BLOCK), 0)
    ki = lax.broadcasted_iota(jnp.int32, (Q_BLOCK, 2 * Q_BLOCK), 1)
    rel = Q_BLOCK + qi - ki
    band = (rel >= 0) & (rel <= N_BACK)
    cur = ki >= Q_BLOCK
    lane = lax.broadcasted_iota(jnp.int32, (Q_BLOCK, LANES), 1)
    head0 = lane < HEAD_DIM

    def rows(start, size, stride):
        return pl.ds(start, size, stride=stride) if stride > 1 else pl.ds(start, size)

    def window(buf4, cur_ref, prev_ref, q_start, dil):
        k_start = SUPER + q_start - Q_BLOCK * dil
        if dil > 1:
            return buf4[rows((k_start % 4) * slab + k_start // 4, 2 * Q_BLOCK, dil // 4), :]
        if k_start >= SUPER:
            return cur_ref[0, pl.ds(k_start - SUPER, 2 * Q_BLOCK), :]
        return jnp.concatenate([prev_ref[0, pl.ds(k_start, Q_BLOCK), :],
                                cur_ref[0, pl.ds(q_start, Q_BLOCK), :]], axis=0)

    def block(pat, dil, q_start):
        if dil > 1:
            rows_q = rows((q_start % 4) * qslab + q_start // 4, Q_BLOCK, dil // 4)
            qs = q4[rows_q, :]
        else:
            rows_q = pl.ds(q_start, Q_BLOCK)
            qs = q_ref[0, rows_q, :]
        kw = window(k4, kc_ref, kp_ref, q_start, dil).astype(BF16)
        vw = window(v4, vc_ref, vp_ref, q_start, dil).astype(BF16)
        if q_start >= Q_BLOCK * dil:
            valid = band
        else:
            valid = band & (cur | (i > 0))
        accs, ms, dens = [], [], []
        for hm in (head0, ~head0):
            qh = jnp.where(hm, qs, 0.0).astype(BF16)
            s = _dot_nt(qh, kw)
            s = jnp.where(valid, s, -jnp.inf)
            m = jnp.max(s, axis=-1, keepdims=True)
            p = jnp.exp2(s - m)
            dens.append(jnp.sum(p, axis=-1, keepdims=True))
            ms.append(m)
            accs.append(_dot(p.astype(BF16), vw))
        acc_s[pat, rows_q, :] = jnp.where(head0, accs[0], accs[1])
        m_s[pat, rows_q, :] = jnp.where(head0, ms[0], ms[1])
        den_s[pat, rows_q, :] = jnp.where(head0, dens[0], dens[1])

    for pat, (window_len, dil) in enumerate(DIL_PATTERNS):
        span = Q_BLOCK * dil
        for s_idx in range(SUPER // span):
            for r in range(dil):
                block(pat, dil, s_idx * span + r)

    for b in range(4):
        natural = pl.ds(b, qslab, stride=4)
        grouped = pl.ds(b * qslab, qslab)
        m0, m1, m2 = m_s[0, natural, :], m_s[1, grouped, :], m_s[2, grouped, :]
        mx = jnp.maximum(jnp.maximum(m0, m1), m2)
        e0, e1, e2 = jnp.exp2(m0 - mx), jnp.exp2(m1 - mx), jnp.exp2(m2 - mx)
        num = e0 * acc_s[0, natural, :] + e1 * acc_s[1, grouped, :] + e2 * acc_s[2, grouped, :]
        den = e0 * den_s[0, natural, :] + e1 * den_s[1, grouped, :] + e2 * den_s[2, grouped, :]
        z = z_ref[0, natural, :]
        res[natural, :] = num / den * (z * _sigmoid(z))
    o_ref[0] = res[...].astype(o_ref.dtype)


def _attention(proj3):
    batch, seq, _ = proj3.shape
    blk = (1, SUPER, LANES)

    def col(base):
        return lambda b, p, i: (b, i, base // LANES + p)

    def col_prev(base):
        return lambda b, p, i: (b, jnp.maximum(i - 1, 0), base // LANES + p)

    return pl.pallas_call(
        _attention_kernel,
        grid=(batch, N_PAIRS, seq // SUPER),
        in_specs=[
            pl.BlockSpec(blk, col(COL_Q)),
            pl.BlockSpec(blk, col(COL_K)),
            pl.BlockSpec(blk, col_prev(COL_K)),
            pl.BlockSpec(blk, col(COL_V)),
            pl.BlockSpec(blk, col_prev(COL_V)),
            pl.BlockSpec(blk, col(COL_ZA)),
        ],
        out_specs=pl.BlockSpec(blk, lambda b, p, i: (b, i, p)),
        out_shape=jax.ShapeDtypeStruct((batch, seq, ATT_WIDTH), BF16),
        scratch_shapes=[
            pltpu.VMEM((2 * SUPER, LANES), F32),
            pltpu.VMEM((2 * SUPER, LANES), F32),
            pltpu.VMEM((SUPER, LANES), F32),
            pltpu.VMEM((len(DIL_PATTERNS), SUPER, LANES), F32),
            pltpu.VMEM((len(DIL_PATTERNS), SUPER, LANES), F32),
            pltpu.VMEM((len(DIL_PATTERNS), SUPER, LANES), F32),
            pltpu.VMEM((SUPER, LANES), F32),
        ],
        compiler_params=pltpu.CompilerParams(
            dimension_semantics=("arbitrary", "arbitrary", "arbitrary"),
            vmem_limit_bytes=VMEM_LIMIT),
        name="dilated_attention",
    )(proj3, proj3, proj3, proj3, proj3, proj3)


def _rwkv_kernel(r_ref, k_ref, v_ref, z_ref, lora_ref,
                 w0_ref, a0_ref, kk_ref, ka_ref, wa2_ref, g2_ref,
                 rk_ref, lng_ref, lnb_ref,
                 o_ref,
                 state,
                 pm_a, pm_r, pm_kb, pm_bk, pm_d, pm_v,
                 pc_r, pc_k, pc_v, pc_gz,
                 mc_rw, mc_y0, mc_m, mc_g, mc_d,
                 *, blocks_per_stream):
    t = pl.program_id(0)
    first_chain = ((2 * jnp.maximum(t - 1, 0)) % blocks_per_stream) == 0

    @pl.when(t == 0)
    def _():
        for ref in (state, pm_a, pm_r, pm_kb, pm_bk, pm_d, pm_v,
                    pc_r, pc_k, pc_v, pc_gz, mc_rw, mc_y0, mc_m, mc_g, mc_d):
            ref[...] = jnp.zeros_like(ref)

    rr = lax.broadcasted_iota(jnp.int32, (LANES, LANES), 0)
    cc = lax.broadcasted_iota(jnp.int32, (LANES, LANES), 1)
    same_head = (rr // HEAD_DIM) == (cc // HEAD_DIM)
    strict = same_head & ((cc % CHUNK) < (rr % CHUNK))
    incl = same_head & ((cc % CHUNK) <= (rr % CHUNK))
    eye = (rr == cc).astype(F32)
    tril = (lax.broadcasted_iota(jnp.int32, (CHUNK, CHUNK), 1)
            <= lax.broadcasted_iota(jnp.int32, (CHUNK, CHUNK), 0)).astype(BF16)
    head0 = lax.broadcasted_iota(jnp.int32, (CHUNK, LANES), 1) < HEAD_DIM
    decay_lane = lax.broadcasted_iota(jnp.int32, (CHUNK, LANES), 1) < DECAY_LORA
    zero_blk = jnp.zeros((LANES, LANES), BF16)
    last = slice(CHUNK - 1, CHUNK)

    def bd(x):
        return jnp.concatenate([jnp.where(head0, x, 0.0), jnp.where(head0, 0.0, x)], axis=0)

    rows = [slice(c * CHUNK, (c + 1) * CHUNK) for c in range(N_CHUNKS)]
    cur = {"half": 0, "slot_p": 0, "slot_m": 1}

    def block_rows(c):
        start = cur["half"] * RWKV_BLOCK + c * CHUNK
        return slice(start, start + CHUNK)

    pv = [dict() for _ in range(N_CHUNKS)]

    def prep_lora_acts(c):
        lora = lora_ref[0, block_rows(c), 0:LORA_USED]
        wa = lora[:, 0:LANES]
        pv[c]["wa_act"] = jnp.where(decay_lane, jnp.tanh(wa), wa).astype(BF16)
        pv[c]["gate_act"] = _sigmoid(lora[:, LANES:LANES + GATE_PAD]).astype(BF16)

    def prep_lora_matmuls(c):
        d = pv[c]
        wa_out = _dot(d["wa_act"], wa2_ref[...].astype(BF16))
        d["lw"] = -(DECAY_SCALE * _sigmoid(w0_ref[...] + wa_out[:, 0:LANES]))
        d["a"] = _sigmoid(a0_ref[...] + wa_out[:, LANES:])
        g = _dot(d["gate_act"], g2_ref[...].astype(BF16))
        z = z_ref[0, block_rows(c), :]
        d["gz"] = g * (z * _sigmoid(z))

    def prep_key_norm(c):
        d = pv[c]
        k = k_ref[0, block_rows(c), :]
        kk = k * kk_ref[...]
        kk = kk * lax.rsqrt(jnp.maximum(_head_sum(kk * kk, head0), KK_NORM_FLOOR ** 2))
        d["kk"] = kk
        d["k2"] = k * (1.0 + (d["a"] - 1.0) * ka_ref[...])
        d["b"] = kk * d["a"]
        parts = _dot(tril, jnp.concatenate(_bf16_pieces(d["lw"], 3), axis=1))
        d["cs"] = parts[:, 0:LANES] + parts[:, LANES:2 * LANES] + parts[:, 2 * LANES:]

    def prep_exp(c):
        d = pv[c]
        d["e_pos"] = jnp.exp(d["cs"])
        d["e_neg"] = jnp.exp(-d["cs"])
        d["e_excl"] = jnp.exp(d["cs"] - d["lw"])
        d["decay_end"] = d["e_pos"][last, :]

    def prep_store_ar(c):
        d = pv[c]
        idx = cur["slot_p"] * N_CHUNKS + c
        pm_a[idx] = bd(-d["kk"] * d["e_excl"]).astype(BF16)
        pm_r[idx] = bd(r_ref[0, block_rows(c), :] * d["e_pos"])

    def prep_store_kbv(c):
        d = pv[c]
        slot_p = cur["slot_p"]
        idx = slot_p * N_CHUNKS + c
        pm_kb[idx] = jnp.concatenate([d["k2"] * d["e_neg"], d["b"] * d["e_neg"]],
                                     axis=0).astype(BF16)
        pm_v[idx] = bd(v_ref[0, block_rows(c), :]).astype(BF16)

    def prep_store_pc(c):
        d = pv[c]
        slot_p = cur["slot_p"]
        pc_r[slot_p, rows[c], :] = r_ref[0, block_rows(c), :]
        pc_k[slot_p, rows[c], :] = d["k2"]
        pc_v[slot_p, rows[c], :] = v_ref[0, block_rows(c), :]
        pc_gz[slot_p, rows[c], :] = d["gz"]

    def prep_store_bk(c):
        d = pv[c]
        idx = cur["slot_p"] * N_CHUNKS + c
        e_end = d["decay_end"] * d["e_neg"]
        bkd_t = jnp.concatenate([bd(d["b"] * e_end), bd(d["k2"] * e_end),
                                 jnp.broadcast_to(d["decay_end"], (LANES, LANES))], axis=0).T
        pm_bk[idx] = bkd_t[:, 0:2 * LANES].astype(BF16)
        pm_d[idx] = bkd_t[:, 2 * LANES:]

    prep_groups = [(prep_lora_acts, 110), (prep_lora_matmuls, 110),
                   (prep_key_norm, 150), (prep_exp, 50), (prep_store_ar, 56),
                   (prep_store_kbv, 50), (prep_store_bk, 75)]

    mm = [dict() for _ in range(N_CHUNKS)]

    def mm_scores(c):
        idx = cur["slot_m"] * N_CHUNKS + c
        kb = pm_kb[idx]
        k_t, b_t = kb[0:CHUNK, :], kb[CHUNK:, :]
        lhs = jnp.concatenate([pm_a[idx], pm_r[idx].astype(BF16)], axis=0)
        x = _dot_nt(lhs, jnp.concatenate([k_t, k_t, b_t, b_t], axis=0))
        d = mm[c]
        d["a_ak"] = jnp.where(strict, x[0:LANES, 0:LANES], 0.0).astype(BF16)
        a_ab = jnp.where(strict, x[0:LANES, LANES:], 0.0)
        d["a_r"] = jnp.concatenate([jnp.where(incl, x[LANES:, 0:LANES], 0.0),
                                    jnp.where(incl, x[LANES:, LANES:], 0.0)],
                                   axis=1).astype(BF16)
        d["q"] = a_ab.astype(BF16)
        d["p"] = eye + a_ab

    def mm_square(c):
        d = mm[c]
        d["q"] = _dot(d["q"], d["q"]).astype(BF16)

    def mm_double(c):
        d = mm[c]
        qp = _dot(d["q"], jnp.concatenate([d["q"], d["p"].astype(BF16)], axis=1))
        d["q"] = qp[:, 0:LANES].astype(BF16)
        d["p"] = d["p"] + qp[:, LANES:]

    def mm_inverse(c):
        d = mm[c]
        d["t"] = (d["p"] + _dot(d["q"], d["p"].astype(BF16))).astype(BF16)

    def mm_akv(c):
        d = mm[c]
        d["ak_v"] = _dot(d["a_ak"], pm_v[cur["slot_m"] * N_CHUNKS + c]).astype(BF16)

    def mm_wu(c):
        d = mm[c]
        rhs = jnp.concatenate([pm_a[cur["slot_m"] * N_CHUNKS + c], d["ak_v"]], axis=1)
        d["wu"] = _dot(d["t"], rhs).astype(BF16)

    def mm_yw(c):
        d = mm[c]
        idx = cur["slot_m"] * N_CHUNKS + c
        wu = d["wu"]
        rhs = jnp.concatenate(
            [jnp.concatenate([pm_v[idx], zero_blk], axis=1),
             jnp.concatenate([wu[:, LANES:], wu[:, 0:LANES]], axis=1)], axis=0)
        yw = _dot(d["a_r"], rhs)
        mc_y0[idx] = yw[:, 0:LANES]
        mc_rw[idx] = (pm_r[idx] + yw[:, LANES:]).astype(BF16)

    def mm_mg(c):
        idx = cur["slot_m"] * N_CHUNKS + c
        rhs = jnp.concatenate([mm[c]["wu"],
                               jnp.concatenate([zero_blk, pm_v[idx]], axis=1)], axis=0)
        mg = _dot(pm_bk[idx], rhs)
        mc_m[idx] = mg[:, 0:LANES].astype(BF16)
        mc_g[idx] = mg[:, LANES:]
        mc_d[idx] = pm_d[idx]

    matmul_stages = [mm_scores, mm_square, mm_double, mm_double, mm_double, mm_double,
                     mm_inverse, mm_akv, mm_wu, mm_yw, mm_mg]

    chain = {"hs": jnp.where(first_chain, 0.0, state[...])}

    cv = [dict() for _ in range(N_CHUNKS)]

    def chain_core(c):
        idx = cur["slot_p"] * N_CHUNKS + c
        hs = chain["hs"]
        hs_b = hs.astype(BF16)
        y = _dot(mc_rw[idx], hs_b) + mc_y0[idx]
        chain["hs"] = hs * mc_d[idx] + _dot(mc_m[idx], hs_b) + mc_g[idx]
        cv[c]["y"] = y[0:CHUNK, :] + y[CHUNK:, :]

    def chain_mean(c):
        slot_p, rws, d = cur["slot_p"], rows[c], cv[c]
        d["yc"] = d["y"] - _head_sum(d["y"], head0) * (1.0 / HEAD_DIM)
        d["bonus"] = _head_sum(pc_r[slot_p, rws, :] * pc_k[slot_p, rws, :] * rk_ref[...],
                               head0) * pc_v[slot_p, rws, :]

    def chain_finish(c):
        slot_p, rws, d = cur["slot_p"], rows[c], cv[c]
        var = _head_sum(d["yc"] * d["yc"], head0) * (1.0 / HEAD_DIM)
        yn = d["yc"] * lax.rsqrt(var + LNX_EPS) * lng_ref[...] + lnb_ref[...]
        o_ref[0, block_rows(c), :] = ((yn + d["bonus"]) * pc_gz[slot_p, rws, :]
                                      ).astype(o_ref.dtype)

    n_rounds = len(matmul_stages)
    per_round = -(-N_CHUNKS // (n_rounds - 3))
    core_gap = N_CHUNKS // per_round

    def chain_cores_of(rnd):
        return range(min(rnd * per_round, N_CHUNKS), min((rnd + 1) * per_round, N_CHUNKS))

    n_slots = (n_rounds - 1) * N_CHUNKS
    total = float(N_CHUNKS * sum(w for _, w in prep_groups))
    prep_slots = [[] for _ in range(n_slots)]
    done = 0.0
    for fn, weight in prep_groups:
        for c in range(N_CHUNKS):
            prep_slots[min(int(done / total * n_slots), n_slots - 1)].append((fn, c))
            done += weight
    for half in (0, 1):
        cur.update(half=half, slot_p=half, slot_m=1 - half)
        for rnd, mm_stage in enumerate(matmul_stages):
            if rnd >= 1:
                for chunk in chain_cores_of(rnd - 1):
                    chain_mean(chunk)
            if rnd >= 2:
                for chunk in chain_cores_of(rnd - 2):
                    chain_finish(chunk)
            cores = list(chain_cores_of(rnd))
            for c in range(N_CHUNKS):
                if c % core_gap == 0 and c // core_gap < len(cores):
                    chain_core(cores[c // core_gap])
                mm_stage(c)
                if rnd == n_rounds - 1:
                    prep_store_pc(c)
                else:
                    for fn, pc in prep_slots[rnd * N_CHUNKS + c]:
                        fn(pc)
    state[...] = chain["hs"]


def _rwkv(proj3, lora3, w0, a0, k_k, k_a, r_k, lnx_g, lnx_b, wa2, g2p):
    batch, seq, _ = proj3.shape
    bps = seq // RWKV_BLOCK
    assert bps % 2 == 0
    wps = bps // 2
    n_total = batch * N_PAIRS * wps
    blk = (1, 2 * RWKV_BLOCK, LANES)

    def split(n):
        return n // (N_PAIRS * wps), n % wps, (n // wps) % N_PAIRS

    def prep_n(s):
        return jnp.minimum(s, n_total - 1)

    def chain_n(s):
        return jnp.maximum(s - 1, 0)

    def col(base):
        def index(s):
            b, i, p = split(prep_n(s))
            return b, i, base // LANES + p
        return index

    def lora_index(s):
        b, i, _ = split(prep_n(s))
        return b, i, 0

    def out_index(s):
        return split(chain_n(s))

    vec_prep = pl.BlockSpec((1, LANES), lambda s: (0, split(prep_n(s))[2]))
    vec_chain = pl.BlockSpec((1, LANES), lambda s: (0, split(chain_n(s))[2]))
    n_slots = 2 * N_CHUNKS
    return pl.pallas_call(
        functools.partial(_rwkv_kernel, blocks_per_stream=bps),
        grid=(n_total + 1,),
        in_specs=[
            pl.BlockSpec(blk, col(COL_R)),
            pl.BlockSpec(blk, col(COL_RK)),
            pl.BlockSpec(blk, col(COL_RV)),
            pl.BlockSpec(blk, col(COL_ZR)),
            pl.BlockSpec((1, 2 * RWKV_BLOCK, LORA_PAD), lora_index),
            vec_prep, vec_prep, vec_prep, vec_prep,
            pl.BlockSpec((LANES, 2 * LANES), lambda s: (0, split(prep_n(s))[2])),
            pl.BlockSpec((GATE_PAD, LANES), lambda s: (0, split(prep_n(s))[2])),
            vec_chain, vec_chain, vec_chain,
        ],
        out_specs=pl.BlockSpec(blk, out_index),
        out_shape=jax.ShapeDtypeStruct((batch, seq, RWKV_WIDTH), BF16),
        scratch_shapes=[
            pltpu.VMEM((LANES, LANES), F32),
            pltpu.VMEM((n_slots, LANES, LANES), BF16),
            pltpu.VMEM((n_slots, LANES, LANES), F32),
            pltpu.VMEM((n_slots, LANES, LANES), BF16),
            pltpu.VMEM((n_slots, LANES, 2 * LANES), BF16),
            pltpu.VMEM((n_slots, LANES, LANES), F32),
            pltpu.VMEM((n_slots, LANES, LANES), BF16),
            pltpu.VMEM((2, RWKV_BLOCK, LANES), F32),
            pltpu.VMEM((2, RWKV_BLOCK, LANES), F32),
            pltpu.VMEM((2, RWKV_BLOCK, LANES), F32),
            pltpu.VMEM((2, RWKV_BLOCK, LANES), F32),
            pltpu.VMEM((n_slots, LANES, LANES), BF16),
            pltpu.VMEM((n_slots, LANES, LANES), F32),
            pltpu.VMEM((n_slots, LANES, LANES), BF16),
            pltpu.VMEM((n_slots, LANES, LANES), F32),
            pltpu.VMEM((n_slots, LANES, LANES), F32),
        ],
        compiler_params=pltpu.CompilerParams(
            dimension_semantics=("arbitrary",),
            vmem_limit_bytes=VMEM_LIMIT),
        name="rwkv7_time_mix",
    )(proj3, proj3, proj3, proj3, lora3, w0, a0, k_k, k_a, wa2, g2p, r_k, lnx_g, lnx_b)


def _out_proj_kernel(att_ref, rwk_ref, wa_ref, wr_ref, x_ref, g_ref, o_ref, *, normalize):
    y = x_ref[...] + _dot(att_ref[...], wa_ref[...]) + _dot(rwk_ref[...], wr_ref[...])
    if normalize:
        ms = jnp.mean(y * y, axis=-1, keepdims=True)
        y = y * lax.rsqrt(ms + NORM_EPS) * g_ref[...]
    o_ref[...] = y


def _out_proj(att2, rwk2, w_out_b, x2, g, normalize):
    tokens = x2.shape[0]
    return pl.pallas_call(
        functools.partial(_out_proj_kernel, normalize=normalize),
        grid=(tokens // OUT_TM,),
        in_specs=[
            pl.BlockSpec((OUT_TM, ATT_WIDTH), lambda i: (i, 0)),
            pl.BlockSpec((OUT_TM, RWKV_WIDTH), lambda i: (i, 0)),
            pl.BlockSpec((ATT_WIDTH, D_MODEL), lambda i: (0, 0)),
            pl.BlockSpec((RWKV_WIDTH, D_MODEL), lambda i: (ATT_WIDTH // RWKV_WIDTH, 0)),
            pl.BlockSpec((OUT_TM, D_MODEL), lambda i: (i, 0)),
            pl.BlockSpec((1, D_MODEL), lambda i: (0, 0)),
        ],
        out_specs=pl.BlockSpec((OUT_TM, D_MODEL), lambda i: (i, 0)),
        out_shape=jax.ShapeDtypeStruct((tokens, D_MODEL), F32),
        compiler_params=pltpu.CompilerParams(
            dimension_semantics=("arbitrary",),
            vmem_limit_bytes=VMEM_LIMIT),
        name="out_proj",
    )(att2, rwk2, w_out_b, w_out_b, x2, g)


def _rope_tables(seq):
    inv_freq = ROPE_THETA ** (-jnp.arange(0, HEAD_DIM, 2, dtype=F32) / HEAD_DIM)
    ang = jnp.arange(seq, dtype=jnp.int32).astype(F32)[:, None] * inv_freq[None, :]
    cos, sin = jnp.cos(ang), jnp.sin(ang)
    reps = LANES // HEAD_DIM
    cos_t = jnp.tile(jnp.concatenate([cos, cos], axis=1), (1, reps))
    sin_t = jnp.tile(jnp.concatenate([-sin, sin], axis=1), (1, reps))
    return cos_t, sin_t


def _layer(x, norm_g, w_in, shift_mu, w0, w2, a0, a2, g2, k_k, k_a, r_k,
           lnx_g, lnx_b, w_out, cos_t, sin_t):
    batch, seq, _ = x.shape
    a_w, r_w = ATT_WIDTH, RWKV_WIDTH
    lo = 4 * a_w + 3 * r_w
    n_lora = DECAY_LORA + AAA_LORA + GATE_LORA
    assert lo == COL_ZR
    w_all = w_in.astype(BF16)
    w_gate = w_all[:, lo + n_lora:]
    w_lora = jnp.pad(w_all[:, lo:lo + n_lora], ((0, 0), (0, LORA_PAD - n_lora)))
    mu_main = shift_mu[None, 0:3 * r_w]
    mu_lora = jnp.pad(shift_mu[None, 3 * r_w:], ((0, 0), (0, LORA_PAD - n_lora)))
    w2p = jnp.pad(w2, ((0, LANES - DECAY_LORA), (0, 0))).reshape(LANES, N_PAIRS, LANES)
    a2p = jnp.pad(a2, ((DECAY_LORA, LANES - DECAY_LORA - AAA_LORA), (0, 0))
                  ).reshape(LANES, N_PAIRS, LANES)
    wa2 = jnp.concatenate([w2p, a2p], axis=2).reshape(LANES, 2 * r_w)
    g2p = jnp.pad(g2, ((0, GATE_PAD - GATE_LORA), (0, 0)))

    x2 = x.reshape(batch * seq, D_MODEL)
    proj, lora = _in_proj(x2, norm_g[None, :], w_all, w_gate, w_lora, cos_t, sin_t,
                          mu_main, mu_lora, seq)
    proj3 = proj.reshape(batch, seq, PROJ_WIDTH)
    lora3 = lora.reshape(batch, seq, LORA_PAD)
    att = _attention(proj3)
    rwk = _rwkv(proj3, lora3, w0[None, :], a0[None, :],
                k_k[None, :], k_a[None, :], r_k.reshape(1, r_w), lnx_g[None, :],
                lnx_b[None, :], wa2, g2p)
    w_out_b = w_out.astype(BF16)
    return att.reshape(batch * seq, a_w), rwk.reshape(batch * seq, r_w), w_out_b, x2


def kernel(x, norm_g, w_in, shift_mu, w0, w2, a0, a2, g2, k_k, k_a, r_k,
           lnx_g, lnx_b, w_out, final_g):
    batch, seq, _ = x.shape
    depth = norm_g.shape[0]
    assert seq % SUPER == 0 and seq % IN_TM == 0 and (batch * seq) % OUT_TM == 0
    cos_t, sin_t = _rope_tables(seq)
    for l in range(depth):
        att2, rwk2, w_out_b, x2 = _layer(
            x, norm_g[l], w_in[l], shift_mu[l], w0[l], w2[l], a0[l], a2[l], g2[l],
            k_k[l], k_a[l], r_k[l], lnx_g[l], lnx_b[l], w_out[l], cos_t, sin_t)
        y2 = _out_proj(att2, rwk2, w_out_b, x2, final_g[None, :],
                       normalize=(l == depth - 1))
        x = y2.reshape(batch, seq, D_MODEL)
    return x
```

```python
import functools

import jax
import jax.numpy as jnp
from jax import lax
from jax.experimental import pallas as pl
from jax.experimental.pallas import tpu as pltpu

F32 = jnp.float32
BF16 = jnp.bfloat16

D_MODEL = 2048
HEAD_DIM = 64
ATT_WIDTH = 1024
RWKV_WIDTH = 1024
DIL_PATTERNS = ((128, 1), (512, 4), (2048, 16))
ROPE_THETA = 10000.0
DECAY_LORA = 64
AAA_LORA = 64
GATE_LORA = 160
NORM_EPS = 1e-5
LNX_EPS = 64e-5
DECAY_SCALE = 0.6065306597126334
KK_NORM_FLOOR = 1e-12
LOG2_E = 1.4426950408889634

LANES = 128
SUBLANES = 8
N_PAIRS = ATT_WIDTH // LANES
N_BACK = 128
Q_BLOCK = 128
SUPER = 2048
CHUNK = 64
RWKV_BLOCK = 1024
N_CHUNKS = RWKV_BLOCK // CHUNK
LORA_PAD = 512
GATE_PAD = 256

COL_Q, COL_K, COL_V, COL_ZA = 0, 1024, 2048, 3072
COL_R, COL_RK, COL_RV, COL_ZR = 4096, 5120, 6144, 7168
PROJ_WIDTH = 8192

IN_TM, IN_TN = 1024, 1024
IN_SUB = 512
N_ROPE_STEPS = (COL_V - COL_Q) // IN_TN
N_Q_STEPS = (COL_K - COL_Q) // IN_TN
N_ALIGNED_STEPS = COL_ZR // IN_TN
N_SHIFT_LO = COL_R // IN_TN
OUT_TM = 512
VMEM_LIMIT = 56 * 1024 * 1024


def _dot(a, b):
    return jnp.dot(a, b, preferred_element_type=F32)


def _dot_nt(a, b):
    return lax.dot_general(a, b, (((1,), (1,)), ((), ())), preferred_element_type=F32)


def _sigmoid(x):
    return 0.5 * jnp.tanh(0.5 * x) + 0.5


def _bf16_pieces(x, terms):
    pieces = []
    for _ in range(terms):
        piece = x.astype(BF16)
        pieces.append(piece)
        x = x - piece.astype(F32)
    return pieces


def _head_sum(x, head0):
    first = jnp.sum(jnp.where(head0, x, 0.0), axis=-1, keepdims=True)
    second = jnp.sum(jnp.where(head0, 0.0, x), axis=-1, keepdims=True)
    return jnp.where(head0, first, second)


def _in_proj_kernel(x_ref, g_ref, w_ref, wz_ref, wl_ref, cos_ref, sin_ref,
                    mu_ref, mul_ref, o_ref, lora_ref, h_ref, carry_ref, carry_lora_ref,
                    *, pos_blocks):
    i = pl.program_id(0)
    j = pl.program_id(1)
    subs = [slice(s * IN_SUB, (s + 1) * IN_SUB) for s in range(IN_TN // IN_SUB)]
    seq_start = (i % pos_blocks) == 0
    last_row = slice(IN_TM - 1, IN_TM)

    def token_shift(u, carry_row, mu):
        row0 = lax.broadcasted_iota(jnp.int32, u.shape, 0) == 0
        prev = jnp.where(row0, jnp.where(seq_start, 0.0, carry_row), pltpu.roll(u, 1, 0))
        return u + (prev - u) * mu

    @pl.when((i == 0) & (j == 0))
    def _():
        carry_ref[...] = jnp.zeros_like(carry_ref)
        carry_lora_ref[...] = jnp.zeros_like(carry_lora_ref)

    @pl.when(j == 0)
    def _():
        x = x_ref[...]
        ms = jnp.mean(x * x, axis=-1, keepdims=True)
        h_ref[...] = (x * lax.rsqrt(ms + NORM_EPS) * g_ref[...]).astype(BF16)
        lora = _dot(h_ref[...], wl_ref[...])
        shifted = token_shift(lora, carry_lora_ref[0:1, :], mul_ref[...])
        carry_lora_ref[0:1, :] = lora[last_row, :]
        lane = lax.broadcasted_iota(jnp.int32, (IN_TM, LANES), 1)
        wa = shifted[:, 0:LANES]
        lora_ref[:, 0:LANES] = jnp.where(lane < DECAY_LORA, jnp.tanh(wa), wa).astype(BF16)
        lora_ref[:, LANES:LANES + GATE_PAD] = _sigmoid(
            shifted[:, LANES:LANES + GATE_PAD]).astype(BF16)
        lora_ref[:, LANES + GATE_PAD:] = jnp.zeros((IN_TM, LORA_PAD - LANES - GATE_PAD), BF16)

    @pl.when(j < N_ROPE_STEPS)
    def _():
        reps = IN_SUB // LANES
        cos = jnp.concatenate([cos_ref[...]] * reps, axis=1)
        sin = jnp.concatenate([sin_ref[...]] * reps, axis=1)
        lane = lax.broadcasted_iota(jnp.int32, (IN_TM, IN_SUB), 1)
        first_half = (lane % HEAD_DIM) < (HEAD_DIM // 2)
        scale = jnp.where(j < N_Q_STEPS, LOG2_E * HEAD_DIM ** -0.5, 1.0).astype(F32)
        for sub in subs:
            acc = _dot(h_ref[...], w_ref[:, sub])
            partner = jnp.where(first_half,
                                pltpu.roll(acc, IN_SUB - HEAD_DIM // 2, 1),
                                pltpu.roll(acc, HEAD_DIM // 2, 1))
            o_ref[:, sub] = (acc * cos + partner * sin) * scale

    @pl.when((j >= N_ROPE_STEPS) & (j < N_SHIFT_LO))
    def _():
        for sub in subs:
            o_ref[:, sub] = _dot(h_ref[...], w_ref[:, sub])

    @pl.when((j >= N_SHIFT_LO) & (j < N_ALIGNED_STEPS))
    def _():
        slot = j - N_SHIFT_LO
        for sub in subs:
            acc = _dot(h_ref[...], w_ref[:, sub])
            o_ref[:, sub] = token_shift(acc, carry_ref[slot, 0:1, sub], mu_ref[:, sub])
            carry_ref[slot, 0:1, sub] = acc[last_row, :]

    @pl.when(j >= N_ALIGNED_STEPS)
    def _():
        for sub in subs:
            o_ref[:, sub] = _dot(h_ref[...], wz_ref[:, sub])


def _in_proj(x2, g, w_all, w_gate, w_lora, cos_t, sin_t, mu_main, mu_lora, seq):
    tokens = x2.shape[0]
    pos_blocks = seq // IN_TM
    n_shift = N_ALIGNED_STEPS - N_SHIFT_LO

    def mu_index(i, j):
        return 0, jnp.clip(j - N_SHIFT_LO, 0, n_shift - 1)

    return pl.pallas_call(
        functools.partial(_in_proj_kernel, pos_blocks=pos_blocks),
        grid=(tokens // IN_TM, PROJ_WIDTH // IN_TN),
        in_specs=[
            pl.BlockSpec((IN_TM, D_MODEL), lambda i, j: (i, 0)),
            pl.BlockSpec((1, D_MODEL), lambda i, j: (0, 0)),
            pl.BlockSpec((D_MODEL, IN_TN), lambda i, j: (0, jnp.minimum(j, N_ALIGNED_STEPS - 1))),
            pl.BlockSpec((D_MODEL, IN_TN), lambda i, j: (0, 0)),
            pl.BlockSpec((D_MODEL, LORA_PAD), lambda i, j: (0, 0)),
            pl.BlockSpec((IN_TM, LANES), lambda i, j: (i % pos_blocks, 0)),
            pl.BlockSpec((IN_TM, LANES), lambda i, j: (i % pos_blocks, 0)),
            pl.BlockSpec((1, IN_TN), mu_index),
            pl.BlockSpec((1, LORA_PAD), lambda i, j: (0, 0)),
        ],
        out_specs=[
            pl.BlockSpec((IN_TM, IN_TN), lambda i, j: (i, j)),
            pl.BlockSpec((IN_TM, LORA_PAD), lambda i, j: (i, 0)),
        ],
        out_shape=[
            jax.ShapeDtypeStruct((tokens, PROJ_WIDTH), F32),
            jax.ShapeDtypeStruct((tokens, LORA_PAD), BF16),
        ],
        scratch_shapes=[
            pltpu.VMEM((IN_TM, D_MODEL), BF16),
            pltpu.VMEM((n_shift, SUBLANES, IN_TN), F32),
            pltpu.VMEM((SUBLANES, LORA_PAD), F32),
        ],
        compiler_params=pltpu.CompilerParams(
            dimension_semantics=("arbitrary", "arbitrary"),
            vmem_limit_bytes=VMEM_LIMIT),
        name="in_proj",
    )(x2, g, w_all, w_gate, w_lora, cos_t, sin_t, mu_main, mu_lora)


def _attention_kernel(q_ref, kc_ref, kp_ref, vc_ref, vp_ref, z_ref, o_ref,
                      k4, v4, q4, acc_s, m_s, den_s, res):
    i = pl.program_id(2)
    slab = 2 * SUPER // 4
    qslab = SUPER // 4
    for b in range(4):
        q4[b * qslab:(b + 1) * qslab, :] = q_ref[0, pl.ds(b, qslab, stride=4), :]
        for dst, prev_ref, cur_ref in ((k4, kp_ref, kc_ref), (v4, vp_ref, vc_ref)):
            dst[b * slab:b * slab + slab // 2, :] = prev_ref[0, pl.ds(b, slab // 2, stride=4), :]
            dst[b * slab + slab // 2:(b + 1) * slab, :] = cur_ref[0, pl.ds(b, slab // 2,
                                                                          stride=4), :]

    qi = lax.broadcasted_iota(jnp.int32, (Q_BLOCK, 2 * Q_BLOCK), 0)
    ki = lax.broadcasted_iota(jnp.int32, (Q_BLOCK, 2 * Q_BLOCK), 1)
    rel = Q_BLOCK + qi - ki
    band = (rel >= 0) & (rel <= N_BACK)
    cur = ki >= Q_BLOCK
    lane = lax.broadcasted_iota(jnp.int32, (Q_BLOCK, LANES), 1)
    head0 = lane < HEAD_DIM

    def rows(start, size, stride):
        return pl.ds(start, size, stride=stride) if stride > 1 else pl.ds(start, size)

    def window(buf4, cur_ref, prev_ref, q_start, dil):
        k_start = SUPER + q_start - Q_BLOCK * dil
        if dil > 1:
            return buf4[rows((k_start % 4) * slab + k_start // 4, 2 * Q_BLOCK, dil // 4), :]
        if k_start >= SUPER:
            return cur_ref[0, pl.ds(k_start - SUPER, 2 * Q_BLOCK), :]
        return jnp.concatenate([prev_ref[0, pl.ds(k_start, Q_BLOCK), :],
                                cur_ref[0, pl.ds(q_start, Q_BLOCK), :]], axis=0)

    def block(pat, dil, q_start):
        if dil > 1:
            rows_q = rows((q_start % 4) * qslab + q_start // 4, Q_BLOCK, dil // 4)
            qs = q4[rows_q, :]
        else:
            rows_q = pl.ds(q_start, Q_BLOCK)
            qs = q_ref[0, rows_q, :]
        kw = window(k4, kc_ref, kp_ref, q_start, dil).astype(BF16)
        vw = window(v4, vc_ref, vp_ref, q_start, dil).astype(BF16)
        if q_start >= Q_BLOCK * dil:
            valid = band
        else:
            valid = band & (cur | (i > 0))
        accs, ms, dens = [], [], []
        for hm in (head0, ~head0):
            qh = jnp.where(hm, qs, 0.0).astype(BF16)
            s = _dot_nt(qh, kw)
            s = jnp.where(valid, s, -jnp.inf)
            m = jnp.max(s, axis=-1, keepdims=True)
            p = jnp.exp2(s - m)
            dens.append(jnp.sum(p, axis=-1, keepdims=True))
            ms.append(m)
            accs.append(_dot(p.astype(BF16), vw))
        acc_s[pat, rows_q, :] = jnp.where(head0, accs[0], accs[1])
        m_s[pat, rows_q, :] = jnp.where(head0, ms[0], ms[1])
        den_s[pat, rows_q, :] = jnp.where(head0, dens[0], dens[1])

    for pat, (window_len, dil) in enumerate(DIL_PATTERNS):
        span = Q_BLOCK * dil
        for s_idx in range(SUPER // span):
            for r in range(dil):
                block(pat, dil, s_idx * span + r)

    for b in range(4):
        natural = pl.ds(b, qslab, stride=4)
        grouped = pl.ds(b * qslab, qslab)
        m0, m1, m2 = m_s[0, natural, :], m_s[1, grouped, :], m_s[2, grouped, :]
        mx = jnp.maximum(jnp.maximum(m0, m1), m2)
        e0, e1, e2 = jnp.exp2(m0 - mx), jnp.exp2(m1 - mx), jnp.exp2(m2 - mx)
        num = e0 * acc_s[0, natural, :] + e1 * acc_s[1, grouped, :] + e2 * acc_s[2, grouped, :]
        den = e0 * den_s[0, natural, :] + e1 * den_s[1, grouped, :] + e2 * den_s[2, grouped, :]
        z = z_ref[0, natural, :]
        res[natural, :] = num / den * (z * _sigmoid(z))
    o_ref[0] = res[...].astype(o_ref.dtype)


def _attention(proj3):
    batch, seq, _ = proj3.shape
    blk = (1, SUPER, LANES)

    def col(base):
        return lambda b, p, i: (b, i, base // LANES + p)

    def col_prev(base):
        return lambda b, p, i: (b, jnp.maximum(i - 1, 0), base // LANES + p)

    return pl.pallas_call(
        _attention_kernel,
        grid=(batch, N_PAIRS, seq // SUPER),
        in_specs=[
            pl.BlockSpec(blk, col(COL_Q)),
            pl.BlockSpec(blk, col(COL_K)),
            pl.BlockSpec(blk, col_prev(COL_K)),
            pl.BlockSpec(blk, col(COL_V)),
            pl.BlockSpec(blk, col_prev(COL_V)),
            pl.BlockSpec(blk, col(COL_ZA)),
        ],
        out_specs=pl.BlockSpec(blk, lambda b, p, i: (b, i, p)),
        out_shape=jax.ShapeDtypeStruct((batch, seq, ATT_WIDTH), BF16),
        scratch_shapes=[
            pltpu.VMEM((2 * SUPER, LANES), F32),
            pltpu.VMEM((2 * SUPER, LANES), F32),
            pltpu.VMEM((SUPER, LANES), F32),
            pltpu.VMEM((len(DIL_PATTERNS), SUPER, LANES), F32),
            pltpu.VMEM((len(DIL_PATTERNS), SUPER, LANES), F32),
            pltpu.VMEM((len(DIL_PATTERNS), SUPER, LANES), F32),
            pltpu.VMEM((SUPER, LANES), F32),
        ],
        compiler_params=pltpu.CompilerParams(
            dimension_semantics=("arbitrary", "arbitrary", "arbitrary"),
            vmem_limit_bytes=VMEM_LIMIT),
        name="dilated_attention",
    )(proj3, proj3, proj3, proj3, proj3, proj3)


def _rwkv_kernel(r_ref, k_ref, v_ref, z_ref, lora_ref,
                 w0_ref, a0_ref, kk_ref, ka_ref, wa2_ref, g2_ref,
                 rk_ref, lng_ref, lnb_ref,
                 o_ref,
                 state,
                 pm_a, pm_r, pm_kb, pm_bk, pm_d, pm_v,
                 pc_r, pc_k, pc_v, pc_gz,
                 mc_rw, mc_y0, mc_m, mc_g, mc_d,
                 *, blocks_per_stream):
    t = pl.program_id(0)
    first_chain = ((2 * jnp.maximum(t - 1, 0)) % blocks_per_stream) == 0

    @pl.when(t == 0)
    def _():
        for ref in (state, pm_a, pm_r, pm_kb, pm_bk, pm_d, pm_v,
                    pc_r, pc_k, pc_v, pc_gz, mc_rw, mc_y0, mc_m, mc_g, mc_d):
            ref[...] = jnp.zeros_like(ref)

    rr = lax.broadcasted_iota(jnp.int32, (LANES, LANES), 0)
    cc = lax.broadcasted_iota(jnp.int32, (LANES, LANES), 1)
    same_head = (rr // HEAD_DIM) == (cc // HEAD_DIM)
    strict = same_head & ((cc % CHUNK) < (rr % CHUNK))
    incl = same_head & ((cc % CHUNK) <= (rr % CHUNK))
    eye = (rr == cc).astype(F32)
    tril = (lax.broadcasted_iota(jnp.int32, (CHUNK, CHUNK), 1)
            <= lax.broadcasted_iota(jnp.int32, (CHUNK, CHUNK), 0)).astype(BF16)
    head0 = lax.broadcasted_iota(jnp.int32, (CHUNK, LANES), 1) < HEAD_DIM
    zero_blk = jnp.zeros((LANES, LANES), BF16)
    last = slice(CHUNK - 1, CHUNK)

    def bd(x):
        return jnp.concatenate([jnp.where(head0, x, 0.0), jnp.where(head0, 0.0, x)], axis=0)

    rows = [slice(c * CHUNK, (c + 1) * CHUNK) for c in range(N_CHUNKS)]
    cur = {"half": 0, "slot_p": 0, "slot_m": 1}

    def block_rows(c):
        start = cur["half"] * RWKV_BLOCK + c * CHUNK
        return slice(start, start + CHUNK)

    pv = [dict() for _ in range(N_CHUNKS)]

    def prep_lora_matmuls(c):
        d = pv[c]
        win = block_rows(c)
        wa_out = _dot(lora_ref[0, win, 0:LANES], wa2_ref[...].astype(BF16))
        d["lw"] = -(DECAY_SCALE * _sigmoid(w0_ref[...] + wa_out[:, 0:LANES]))
        d["a"] = _sigmoid(a0_ref[...] + wa_out[:, LANES:])
        g = _dot(lora_ref[0, win, LANES:LANES + GATE_PAD], g2_ref[...].astype(BF16))
        z = z_ref[0, block_rows(c), :]
        d["gz"] = g * (z * _sigmoid(z))

    def prep_key_norm(c):
        d = pv[c]
        k = k_ref[0, block_rows(c), :]
        kk = k * kk_ref[...]
        kk = kk * lax.rsqrt(jnp.maximum(_head_sum(kk * kk, head0), KK_NORM_FLOOR ** 2))
        d["kk"] = kk
        d["k2"] = k * (1.0 + (d["a"] - 1.0) * ka_ref[...])
        d["b"] = kk * d["a"]
        parts = _dot(tril, jnp.concatenate(_bf16_pieces(d["lw"], 3), axis=1))
        d["cs"] = parts[:, 0:LANES] + parts[:, LANES:2 * LANES] + parts[:, 2 * LANES:]

    def prep_exp(c):
        d = pv[c]
        d["e_pos"] = jnp.exp(d["cs"])
        d["e_neg"] = jnp.exp(-d["cs"])
        d["e_excl"] = jnp.exp(d["cs"] - d["lw"])
        d["decay_end"] = d["e_pos"][last, :]

    def prep_store_ar(c):
        d = pv[c]
        idx = cur["slot_p"] * N_CHUNKS + c
        pm_a[idx] = bd(-d["kk"] * d["e_excl"]).astype(BF16)
        pm_r[idx] = bd(r_ref[0, block_rows(c), :] * d["e_pos"])

    def prep_store_kbv(c):
        d = pv[c]
        slot_p = cur["slot_p"]
        idx = slot_p * N_CHUNKS + c
        pm_kb[idx] = jnp.concatenate([d["k2"] * d["e_neg"], d["b"] * d["e_neg"]],
                                     axis=0).astype(BF16)
        pm_v[idx] = bd(v_ref[0, block_rows(c), :]).astype(BF16)

    def prep_store_pc(c):
        d = pv[c]
        slot_p = cur["slot_p"]
        pc_r[slot_p, rows[c], :] = r_ref[0, block_rows(c), :]
        pc_k[slot_p, rows[c], :] = d["k2"]
        pc_v[slot_p, rows[c], :] = v_ref[0, block_rows(c), :]
        pc_gz[slot_p, rows[c], :] = d["gz"]

    def prep_store_bk(c):
        d = pv[c]
        idx = cur["slot_p"] * N_CHUNKS + c
        e_end = d["decay_end"] * d["e_neg"]
        bkd_t = jnp.concatenate([bd(d["b"] * e_end), bd(d["k2"] * e_end),
                                 jnp.broadcast_to(d["decay_end"], (LANES, LANES))], axis=0).T
        pm_bk[idx] = bkd_t[:, 0:2 * LANES].astype(BF16)
        pm_d[idx] = bkd_t[:, 2 * LANES:]

    prep_groups = [(prep_lora_matmuls, 110),
                   (prep_key_norm, 150), (prep_exp, 50), (prep_store_ar, 56),
                   (prep_store_kbv, 50), (prep_store_bk, 75)]

    mm = [dict() for _ in range(N_CHUNKS)]

    def mm_scores(c):
        idx = cur["slot_m"] * N_CHUNKS + c
        kb = pm_kb[idx]
        k_t, b_t = kb[0:CHUNK, :], kb[CHUNK:, :]
        lhs = jnp.concatenate([pm_a[idx], pm_r[idx].astype(BF16)], axis=0)
        x = _dot_nt(lhs, jnp.concatenate([k_t, k_t, b_t, b_t], axis=0))
        d = mm[c]
        d["a_ak"] = jnp.where(strict, x[0:LANES, 0:LANES], 0.0).astype(BF16)
        a_ab = jnp.where(strict, x[0:LANES, LANES:], 0.0)
        d["a_r"] = jnp.concatenate([jnp.where(incl, x[LANES:, 0:LANES], 0.0),
                                    jnp.where(incl, x[LANES:, LANES:], 0.0)],
                                   axis=1).astype(BF16)
        d["q"] = a_ab.astype(BF16)
        d["p"] = eye + a_ab

    def mm_square(c):
        d = mm[c]
        d["q"] = _dot(d["q"], d["q"]).astype(BF16)

    def mm_double(c):
        d = mm[c]
        qp = _dot(d["q"], jnp.concatenate([d["q"], d["p"].astype(BF16)], axis=1))
        d["q"] = qp[:, 0:LANES].astype(BF16)
        d["p"] = d["p"] + qp[:, LANES:]

    def mm_inverse(c):
        d = mm[c]
        d["t"] = (d["p"] + _dot(d["q"], d["p"].astype(BF16))).astype(BF16)

    def mm_akv(c):
        d = mm[c]
        d["ak_v"] = _dot(d["a_ak"], pm_v[cur["slot_m"] * N_CHUNKS + c]).astype(BF16)

    def mm_wu(c):
        d = mm[c]
        rhs = jnp.concatenate([pm_a[cur["slot_m"] * N_CHUNKS + c], d["ak_v"]], axis=1)
        d["wu"] = _dot(d["t"], rhs).astype(BF16)

    def mm_yw(c):
        d = mm[c]
        idx = cur["slot_m"] * N_CHUNKS + c
        wu = d["wu"]
        rhs = jnp.concatenate(
            [jnp.concatenate([pm_v[idx], zero_blk], axis=1),
             jnp.concatenate([wu[:, LANES:], wu[:, 0:LANES]], axis=1)], axis=0)
        yw = _dot(d["a_r"], rhs)
        mc_y0[idx] = yw[:, 0:LANES]
        mc_rw[idx] = (pm_r[idx] + yw[:, LANES:]).astype(BF16)

    def mm_mg(c):
        idx = cur["slot_m"] * N_CHUNKS + c
        rhs = jnp.concatenate([mm[c]["wu"],
                               jnp.concatenate([zero_blk, pm_v[idx]], axis=1)], axis=0)
        mg = _dot(pm_bk[idx], rhs)
        mc_m[idx] = mg[:, 0:LANES].astype(BF16)
        mc_g[idx] = mg[:, LANES:]
        mc_d[idx] = pm_d[idx]

    matmul_stages = [mm_scores, mm_square, mm_double, mm_double, mm_double, mm_double,
                     mm_inverse, mm_akv, mm_wu, mm_yw, mm_mg]

    chain = {"hs": jnp.where(first_chain, 0.0, state[...])}

    cv = [dict() for _ in range(N_CHUNKS)]

    def chain_core(c):
        idx = cur["slot_p"] * N_CHUNKS + c
        hs = chain["hs"]
        hs_b = hs.astype(BF16)
        y = _dot(mc_rw[idx], hs_b) + mc_y0[idx]
        chain["hs"] = hs * mc_d[idx] + _dot(mc_m[idx], hs_b) + mc_g[idx]
        cv[c]["y"] = y[0:CHUNK, :] + y[CHUNK:, :]

    def chain_mean(c):
        slot_p, rws, d = cur["slot_p"], rows[c], cv[c]
        d["yc"] = d["y"] - _head_sum(d["y"], head0) * (1.0 / HEAD_DIM)
        d["bonus"] = _head_sum(pc_r[slot_p, rws, :] * pc_k[slot_p, rws, :] * rk_ref[...],
                               head0) * pc_v[slot_p, rws, :]

    def chain_finish(c):
        slot_p, rws, d = cur["slot_p"], rows[c], cv[c]
        var = _head_sum(d["yc"] * d["yc"], head0) * (1.0 / HEAD_DIM)
        yn = d["yc"] * lax.rsqrt(var + LNX_EPS) * lng_ref[...] + lnb_ref[...]
        o_ref[0, block_rows(c), :] = ((yn + d["bonus"]) * pc_gz[slot_p, rws, :]
                                      ).astype(o_ref.dtype)

    n_rounds = len(matmul_stages)
    per_round = -(-N_CHUNKS // (n_rounds - 3))
    core_gap = N_CHUNKS // per_round

    def chain_cores_of(rnd):
        return range(min(rnd * per_round, N_CHUNKS), min((rnd + 1) * per_round, N_CHUNKS))

    n_slots = (n_rounds - 1) * N_CHUNKS
    total = float(N_CHUNKS * sum(w for _, w in prep_groups))
    prep_slots = [[] for _ in range(n_slots)]
    done = 0.0
    for fn, weight in prep_groups:
        for c in range(N_CHUNKS):
            prep_slots[min(int(done / total * n_slots), n_slots - 1)].append((fn, c))
            done += weight
    for half in (0, 1):
        cur.update(half=half, slot_p=half, slot_m=1 - half)
        for rnd, mm_stage in enumerate(matmul_stages):
            if rnd >= 1:
                for chunk in chain_cores_of(rnd - 1):
                    chain_mean(chunk)
            if rnd >= 2:
                for chunk in chain_cores_of(rnd - 2):
                    chain_finish(chunk)
            cores = list(chain_cores_of(rnd))
            for c in range(N_CHUNKS):
                if c % core_gap == 0 and c // core_gap < len(cores):
                    chain_core(cores[c // core_gap])
                mm_stage(c)
                if rnd == n_rounds - 1:
                    prep_store_pc(c)
                else:
                    for fn, pc in prep_slots[rnd * N_CHUNKS + c]:
                        fn(pc)
    state[...] = chain["hs"]


def _rwkv(proj3, lora3, w0, a0, k_k, k_a, r_k, lnx_g, lnx_b, wa2, g2p):
    batch, seq, _ = proj3.shape
    bps = seq // RWKV_BLOCK
    assert bps % 2 == 0
    wps = bps // 2
    n_total = batch * N_PAIRS * wps
    blk = (1, 2 * RWKV_BLOCK, LANES)

    def split(n):
        return n // (N_PAIRS * wps), n % wps, (n // wps) % N_PAIRS

    def prep_n(s):
        return jnp.minimum(s, n_total - 1)

    def chain_n(s):
        return jnp.maximum(s - 1, 0)

    def col(base):
        def index(s):
            b, i, p = split(prep_n(s))
            return b, i, base // LANES + p
        return index

    def lora_index(s):
        b, i, _ = split(prep_n(s))
        return b, i, 0

    def out_index(s):
        return split(chain_n(s))

    vec_prep = pl.BlockSpec((1, LANES), lambda s: (0, split(prep_n(s))[2]))
    vec_chain = pl.BlockSpec((1, LANES), lambda s: (0, split(chain_n(s))[2]))
    n_slots = 2 * N_CHUNKS
    return pl.pallas_call(
        functools.partial(_rwkv_kernel, blocks_per_stream=bps),
        grid=(n_total + 1,),
        in_specs=[
            pl.BlockSpec(blk, col(COL_R)),
            pl.BlockSpec(blk, col(COL_RK)),
            pl.BlockSpec(blk, col(COL_RV)),
            pl.BlockSpec(blk, col(COL_ZR)),
            pl.BlockSpec((1, 2 * RWKV_BLOCK, LORA_PAD), lora_index),
            vec_prep, vec_prep, vec_prep, vec_prep,
            pl.BlockSpec((LANES, 2 * LANES), lambda s: (0, split(prep_n(s))[2])),
            pl.BlockSpec((GATE_PAD, LANES), lambda s: (0, split(prep_n(s))[2])),
            vec_chain, vec_chain, vec_chain,
        ],
        out_specs=pl.BlockSpec(blk, out_index),
        out_shape=jax.ShapeDtypeStruct((batch, seq, RWKV_WIDTH), BF16),
        scratch_shapes=[
            pltpu.VMEM((LANES, LANES), F32),
            pltpu.VMEM((n_slots, LANES, LANES), BF16),
            pltpu.VMEM((n_slots, LANES, LANES), F32),
            pltpu.VMEM((n_slots, LANES, LANES), BF16),
            pltpu.VMEM((n_slots, LANES, 2 * LANES), BF16),
            pltpu.VMEM((n_slots, LANES, LANES), F32),
            pltpu.VMEM((n_slots, LANES, LANES), BF16),
            pltpu.VMEM((2, RWKV_BLOCK, LANES), F32),
            pltpu.VMEM((2, RWKV_BLOCK, LANES), F32),
            pltpu.VMEM((2, RWKV_BLOCK, LANES), F32),
            pltpu.VMEM((2, RWKV_BLOCK, LANES), F32),
            pltpu.VMEM((n_slots, LANES, LANES), BF16),
            pltpu.VMEM((n_slots, LANES, LANES), F32),
            pltpu.VMEM((n_slots, LANES, LANES), BF16),
            pltpu.VMEM((n_slots, LANES, LANES), F32),
            pltpu.VMEM((n_slots, LANES, LANES), F32),
        ],
        compiler_params=pltpu.CompilerParams(
            dimension_semantics=("arbitrary",),
            vmem_limit_bytes=VMEM_LIMIT),
        name="rwkv7_time_mix",
    )(proj3, proj3, proj3, proj3, lora3, w0, a0, k_k, k_a, wa2, g2p, r_k, lnx_g, lnx_b)


def _out_proj_kernel(att_ref, rwk_ref, wa_ref, wr_ref, x_ref, g_ref, o_ref, *, normalize):
    y = x_ref[...] + _dot(att_ref[...], wa_ref[...]) + _dot(rwk_ref[...], wr_ref[...])
    if normalize:
        ms = jnp.mean(y * y, axis=-1, keepdims=True)
        y = y * lax.rsqrt(ms + NORM_EPS) * g_ref[...]
    o_ref[...] = y


def _out_proj(att2, rwk2, w_out_b, x2, g, normalize):
    tokens = x2.shape[0]
    return pl.pallas_call(
        functools.partial(_out_proj_kernel, normalize=normalize),
        grid=(tokens // OUT_TM,),
        in_specs=[
            pl.BlockSpec((OUT_TM, ATT_WIDTH), lambda i: (i, 0)),
            pl.BlockSpec((OUT_TM, RWKV_WIDTH), lambda i: (i, 0)),
            pl.BlockSpec((ATT_WIDTH, D_MODEL), lambda i: (0, 0)),
            pl.BlockSpec((RWKV_WIDTH, D_MODEL), lambda i: (ATT_WIDTH // RWKV_WIDTH, 0)),
            pl.BlockSpec((OUT_TM, D_MODEL), lambda i: (i, 0)),
            pl.BlockSpec((1, D_MODEL), lambda i: (0, 0)),
        ],
        out_specs=pl.BlockSpec((OUT_TM, D_MODEL), lambda i: (i, 0)),
        out_shape=jax.ShapeDtypeStruct((tokens, D_MODEL), F32),
        compiler_params=pltpu.CompilerParams(
            dimension_semantics=("arbitrary",),
            vmem_limit_bytes=VMEM_LIMIT),
        name="out_proj",
    )(att2, rwk2, w_out_b, w_out_b, x2, g)


def _rope_tables(seq):
    inv_freq = ROPE_THETA ** (-jnp.arange(0, HEAD_DIM, 2, dtype=F32) / HEAD_DIM)
    ang = jnp.arange(seq, dtype=jnp.int32).astype(F32)[:, None] * inv_freq[None, :]
    cos, sin = jnp.cos(ang), jnp.sin(ang)
    reps = LANES // HEAD_DIM
    cos_t = jnp.tile(jnp.concatenate([cos, cos], axis=1), (1, reps))
    sin_t = jnp.tile(jnp.concatenate([-sin, sin], axis=1), (1, reps))
    return cos_t, sin_t


def _layer(x, norm_g, w_in, shift_mu, w0, w2, a0, a2, g2, k_k, k_a, r_k,
           lnx_g, lnx_b, w_out, cos_t, sin_t):
    batch, seq, _ = x.shape
    a_w, r_w = ATT_WIDTH, RWKV_WIDTH
    lo = 4 * a_w + 3 * r_w
    n_lora = DECAY_LORA + AAA_LORA + GATE_LORA
    assert lo == COL_ZR
    w_all = w_in.astype(BF16)
    w_gate = w_all[:, lo + n_lora:]
    w_lora = jnp.pad(w_all[:, lo:lo + n_lora], ((0, 0), (0, LORA_PAD - n_lora)))
    mu_main = shift_mu[None, 0:3 * r_w]
    mu_lora = jnp.pad(shift_mu[None, 3 * r_w:], ((0, 0), (0, LORA_PAD - n_lora)))
    w2p = jnp.pad(w2, ((0, LANES - DECAY_LORA), (0, 0))).reshape(LANES, N_PAIRS, LANES)
    a2p = jnp.pad(a2, ((DECAY_LORA, LANES - DECAY_LORA - AAA_LORA), (0, 0))
                  ).reshape(LANES, N_PAIRS, LANES)
    wa2 = jnp.concatenate([w2p, a2p], axis=2).reshape(LANES, 2 * r_w)
    g2p = jnp.pad(g2, ((0, GATE_PAD - GATE_LORA), (0, 0)))

    x2 = x.reshape(batch * seq, D_MODEL)
    proj, lora = _in_proj(x2, norm_g[None, :], w_all, w_gate, w_lora, cos_t, sin_t,
                          mu_main, mu_lora, seq)
    proj3 = proj.reshape(batch, seq, PROJ_WIDTH)
    lora3 = lora.reshape(batch, seq, LORA_PAD)
    att = _attention(proj3)
    rwk = _rwkv(proj3, lora3, w0[None, :], a0[None, :],
                k_k[None, :], k_a[None, :], r_k.reshape(1, r_w), lnx_g[None, :],
                lnx_b[None, :], wa2, g2p)
    w_out_b = w_out.astype(BF16)
    return att.reshape(batch * seq, a_w), rwk.reshape(batch * seq, r_w), w_out_b, x2


def kernel(x, norm_g, w_in, shift_mu, w0, w2, a0, a2, g2, k_k, k_a, r_k,
           lnx_g, lnx_b, w_out, final_g):
    batch, seq, _ = x.shape
    depth = norm_g.shape[0]
    assert seq % SUPER == 0 and seq % IN_TM == 0 and (batch * seq) % OUT_TM == 0
    cos_t, sin_t = _rope_tables(seq)
    for l in range(depth):
        att2, rwk2, w_out_b, x2 = _layer(
            x, norm_g[l], w_in[l], shift_mu[l], w0[l], w2[l], a0[l], a2[l], g2[l],
            k_k[l], k_a[l], r_k[l], lnx_g[l], lnx_b[l], w_out[l], cos_t, sin_t)
        y2 = _out_proj(att2, rwk2, w_out_b, x2, final_g[None, :],
                       normalize=(l == depth - 1))
        x = y2.reshape(batch, seq, D_MODEL)
    return x
```

```python
import functools

import jax
import jax.numpy as jnp
from jax import lax
from jax.experimental import pallas as pl
from jax.experimental.pallas import tpu as pltpu

F32 = jnp.float32
BF16 = jnp.bfloat16

D_MODEL = 2048
HEAD_DIM = 64
ATT_WIDTH = 1024
RWKV_WIDTH = 1024
DIL_PATTERNS = ((128, 1), (512, 4), (2048, 16))
ROPE_THETA = 10000.0
DECAY_LORA = 64
AAA_LORA = 64
GATE_LORA = 160
NORM_EPS = 1e-5
LNX_EPS = 64e-5
DECAY_SCALE = 0.6065306597126334
KK_NORM_FLOOR = 1e-12
LOG2_E = 1.4426950408889634

LANES = 128
SUBLANES = 8
N_PAIRS = ATT_WIDTH // LANES
N_BACK = 128
Q_BLOCK = 128
SUPER = 2048
CHUNK = 64
RWKV_BLOCK = 1024
N_CHUNKS = RWKV_BLOCK // CHUNK
LORA_PAD = 512
GATE_PAD = 256

COL_Q, COL_K, COL_V, COL_ZA = 0, 1024, 2048, 3072
COL_R, COL_RK, COL_RV, COL_ZR = 4096, 5120, 6144, 7168
PROJ_WIDTH = 8192

IN_TM, IN_TN = 1024, 1024
IN_SUB = 512
N_ROPE_STEPS = (COL_V - COL_Q) // IN_TN
N_Q_STEPS = (COL_K - COL_Q) // IN_TN
N_ALIGNED_STEPS = COL_ZR // IN_TN
N_SHIFT_LO = COL_R // IN_TN
OUT_TM = 512
VMEM_LIMIT = 56 * 1024 * 1024


def _dot(a, b):
    return jnp.dot(a, b, preferred_element_type=F32)


def _dot_nt(a, b):
    return lax.dot_general(a, b, (((1,), (1,)), ((), ())), preferred_element_type=F32)


def _sigmoid(x):
    return 0.5 * jnp.tanh(0.5 * x) + 0.5


def _bf16_pieces(x, terms):
    pieces = []
    for _ in range(terms):
        piece = x.astype(BF16)
        pieces.append(piece)
        x = x - piece.astype(F32)
    return pieces


def _head_sum(x, head0):
    first = jnp.sum(jnp.where(head0, x, 0.0), axis=-1, keepdims=True)
    second = jnp.sum(jnp.where(head0, 0.0, x), axis=-1, keepdims=True)
    return jnp.where(head0, first, second)


def _in_proj_kernel(x_ref, g_ref, w_ref, wz_ref, wl_ref, cos_ref, sin_ref,
                    mu_ref, mul_ref, o_ref, lora_ref, h_ref, carry_ref, carry_lora_ref,
                    *, pos_blocks):
    i = pl.program_id(0)
    j = pl.program_id(1)
    subs = [slice(s * IN_SUB, (s + 1) * IN_SUB) for s in range(IN_TN // IN_SUB)]
    seq_start = (i % pos_blocks) == 0
    last_row = slice(IN_TM - 1, IN_TM)

    def token_shift(u, carry_row, mu):
        row0 = lax.broadcasted_iota(jnp.int32, u.shape, 0) == 0
        prev = jnp.where(row0, jnp.where(seq_start, 0.0, carry_row), pltpu.roll(u, 1, 0))
        return u + (prev - u) * mu

    @pl.when((i == 0) & (j == 0))
    def _():
        carry_ref[...] = jnp.zeros_like(carry_ref)
        carry_lora_ref[...] = jnp.zeros_like(carry_lora_ref)

    @pl.when(j == 0)
    def _():
        x = x_ref[...]
        ms = jnp.mean(x * x, axis=-1, keepdims=True)
        h_ref[...] = (x * lax.rsqrt(ms + NORM_EPS) * g_ref[...]).astype(BF16)
        lora = _dot(h_ref[...], wl_ref[...])
        shifted = token_shift(lora, carry_lora_ref[0:1, :], mul_ref[...])
        carry_lora_ref[0:1, :] = lora[last_row, :]
        lane = lax.broadcasted_iota(jnp.int32, (IN_TM, LANES), 1)
        wa = shifted[:, 0:LANES]
        lora_ref[:, 0:LANES] = jnp.where(lane < DECAY_LORA, jnp.tanh(wa), wa).astype(BF16)
        lora_ref[:, LANES:LANES + GATE_PAD] = _sigmoid(
            shifted[:, LANES:LANES + GATE_PAD]).astype(BF16)
        lora_ref[:, LANES + GATE_PAD:] = jnp.zeros((IN_TM, LORA_PAD - LANES - GATE_PAD), BF16)

    @pl.when(j < N_ROPE_STEPS)
    def _():
        reps = IN_SUB // LANES
        cos = jnp.concatenate([cos_ref[...]] * reps, axis=1)
        sin = jnp.concatenate([sin_ref[...]] * reps, axis=1)
        lane = lax.broadcasted_iota(jnp.int32, (IN_TM, IN_SUB), 1)
        first_half = (lane % HEAD_DIM) < (HEAD_DIM // 2)
        scale = jnp.where(j < N_Q_STEPS, LOG2_E * HEAD_DIM ** -0.5, 1.0).astype(F32)
        for sub in subs:
            acc = _dot(h_ref[...], w_ref[:, sub])
            partner = jnp.where(first_half,
                                pltpu.roll(acc, IN_SUB - HEAD_DIM // 2, 1),
                                pltpu.roll(acc, HEAD_DIM // 2, 1))
            o_ref[:, sub] = (acc * cos + partner * sin) * scale

    @pl.when((j >= N_ROPE_STEPS) & (j < N_SHIFT_LO))
    def _():
        for sub in subs:
            o_ref[:, sub] = _dot(h_ref[...], w_ref[:, sub])

    @pl.when((j >= N_SHIFT_LO) & (j < N_ALIGNED_STEPS))
    def _():
        slot = j - N_SHIFT_LO
        for sub in subs:
            acc = _dot(h_ref[...], w_ref[:, sub])
            o_ref[:, sub] = token_shift(acc, carry_ref[slot, 0:1, sub], mu_ref[:, sub])
            carry_ref[slot, 0:1, sub] = acc[last_row, :]

    @pl.when(j >= N_ALIGNED_STEPS)
    def _():
        for sub in subs:
            o_ref[:, sub] = _dot(h_ref[...], wz_ref[:, sub])


def _in_proj(x2, g, w_all, w_gate, w_lora, cos_t, sin_t, mu_main, mu_lora, seq):
    tokens = x2.shape[0]
    pos_blocks = seq // IN_TM
    n_shift = N_ALIGNED_STEPS - N_SHIFT_LO

    def mu_index(i, j):
        return 0, jnp.clip(j - N_SHIFT_LO, 0, n_shift - 1)

    return pl.pallas_call(
        functools.partial(_in_proj_kernel, pos_blocks=pos_blocks),
        grid=(tokens // IN_TM, PROJ_WIDTH // IN_TN),
        in_specs=[
            pl.BlockSpec((IN_TM, D_MODEL), lambda i, j: (i, 0)),
            pl.BlockSpec((1, D_MODEL), lambda i, j: (0, 0)),
            pl.BlockSpec((D_MODEL, IN_TN), lambda i, j: (0, jnp.minimum(j, N_ALIGNED_STEPS - 1))),
            pl.BlockSpec((D_MODEL, IN_TN), lambda i, j: (0, 0)),
            pl.BlockSpec((D_MODEL, LORA_PAD), lambda i, j: (0, 0)),
            pl.BlockSpec((IN_TM, LANES), lambda i, j: (i % pos_blocks, 0)),
            pl.BlockSpec((IN_TM, LANES), lambda i, j: (i % pos_blocks, 0)),
            pl.BlockSpec((1, IN_TN), mu_index),
            pl.BlockSpec((1, LORA_PAD), lambda i, j: (0, 0)),
        ],
        out_specs=[
            pl.BlockSpec((IN_TM, IN_TN), lambda i, j: (i, j)),
            pl.BlockSpec((IN_TM, LORA_PAD), lambda i, j: (i, 0)),
        ],
        out_shape=[
            jax.ShapeDtypeStruct((tokens, PROJ_WIDTH), F32),
            jax.ShapeDtypeStruct((tokens, LORA_PAD), BF16),
        ],
        scratch_shapes=[
            pltpu.VMEM((IN_TM, D_MODEL), BF16),
            pltpu.VMEM((n_shift, SUBLANES, IN_TN), F32),
            pltpu.VMEM((SUBLANES, LORA_PAD), F32),
        ],
        compiler_params=pltpu.CompilerParams(
            dimension_semantics=("arbitrary", "arbitrary"),
            vmem_limit_bytes=VMEM_LIMIT),
        name="in_proj",
    )(x2, g, w_all, w_gate, w_lora, cos_t, sin_t, mu_main, mu_lora)


def _attention_kernel(q_ref, kc_ref, kp_ref, vc_ref, vp_ref, z_ref, o_ref,
                      k4, v4, q4, acc_s, m_s, den_s, res):
    i = pl.program_id(2)
    slab = 2 * SUPER // 4
    qslab = SUPER // 4
    for b in range(4):
        q4[b * qslab:(b + 1) * qslab, :] = q_ref[0, pl.ds(b, qslab, stride=4), :]
        for dst, prev_ref, cur_ref in ((k4, kp_ref, kc_ref), (v4, vp_ref, vc_ref)):
            dst[b * slab:b * slab + slab // 2, :] = prev_ref[0, pl.ds(b, slab // 2, stride=4), :]
            dst[b * slab + slab // 2:(b + 1) * slab, :] = cur_ref[0, pl.ds(b, slab // 2,
                                                                          stride=4), :]

    qi = lax.broadcasted_iota(jnp.int32, (Q_BLOCK, 2 * Q_BLOCK), 0)
    ki = lax.broadcasted_iota(jnp.int32, (Q_BLOCK, 2 * Q_BLOCK), 1)
    rel = Q_BLOCK + qi - ki
    band = (rel >= 0) & (rel <= N_BACK)
    cur = ki >= Q_BLOCK
    lane = lax.broadcasted_iota(jnp.int32, (Q_BLOCK, LANES), 1)
    head0 = lane < HEAD_DIM

    def rows(start, size, stride):
        return pl.ds(start, size, stride=stride) if stride > 1 else pl.ds(start, size)

    def window(buf4, cur_ref, prev_ref, q_start, dil):
        k_start = SUPER + q_start - Q_BLOCK * dil
        if dil > 1:
            return buf4[rows((k_start % 4) * slab + k_start // 4, 2 * Q_BLOCK, dil // 4), :]
        if k_start >= SUPER:
            return cur_ref[0, pl.ds(k_start - SUPER, 2 * Q_BLOCK), :]
        return jnp.concatenate([prev_ref[0, pl.ds(k_start, Q_BLOCK), :],
                                cur_ref[0, pl.ds(q_start, Q_BLOCK), :]], axis=0)

    def block(pat, dil, q_start):
        if dil > 1:
            rows_q = rows((q_start % 4) * qslab + q_start // 4, Q_BLOCK, dil // 4)
            qs = q4[rows_q, :]
        else:
            rows_q = pl.ds(q_start, Q_BLOCK)
            qs = q_ref[0, rows_q, :]
        kw = window(k4, kc_ref, kp_ref, q_start, dil).astype(BF16)
        vw = window(v4, vc_ref, vp_ref, q_start, dil).astype(BF16)
        vw1 = jnp.concatenate([vw, jnp.ones((2 * Q_BLOCK, LANES), BF16)], axis=1)
        if q_start >= Q_BLOCK * dil:
            valid = band
        else:
            valid = band & (cur | (i > 0))
        accs, ms, dens = [], [], []
        for hm in (head0, ~head0):
            qh = jnp.where(hm, qs, 0.0).astype(BF16)
            s = _dot_nt(qh, kw)
            s = jnp.where(valid, s, -jnp.inf)
            m = jnp.max(s, axis=-1, keepdims=True)
            p = jnp.exp2(s - m)
            pv = _dot(p.astype(BF16), vw1)
            dens.append(pv[:, LANES:])
            ms.append(m)
            accs.append(pv[:, 0:LANES])
        acc_s[pat, rows_q, :] = jnp.where(head0, accs[0], accs[1])
        m_s[pat, rows_q, :] = jnp.where(head0, ms[0], ms[1])
        den_s[pat, rows_q, :] = jnp.where(head0, dens[0], dens[1])

    for pat, (window_len, dil) in enumerate(DIL_PATTERNS):
        span = Q_BLOCK * dil
        for s_idx in range(SUPER // span):
            for r in range(dil):
                block(pat, dil, s_idx * span + r)

    for b in range(4):
        natural = pl.ds(b, qslab, stride=4)
        grouped = pl.ds(b * qslab, qslab)
        m0, m1, m2 = m_s[0, natural, :], m_s[1, grouped, :], m_s[2, grouped, :]
        mx = jnp.maximum(jnp.maximum(m0, m1), m2)
        e0, e1, e2 = jnp.exp2(m0 - mx), jnp.exp2(m1 - mx), jnp.exp2(m2 - mx)
        num = e0 * acc_s[0, natural, :] + e1 * acc_s[1, grouped, :] + e2 * acc_s[2, grouped, :]
        den = e0 * den_s[0, natural, :] + e1 * den_s[1, grouped, :] + e2 * den_s[2, grouped, :]
        z = z_ref[0, natural, :]
        res[natural, :] = num / den * (z * _sigmoid(z))
    o_ref[0] = res[...].astype(o_ref.dtype)


def _attention(proj3):
    batch, seq, _ = proj3.shape
    blk = (1, SUPER, LANES)

    def col(base):
        return lambda b, p, i: (b, i, base // LANES + p)

    def col_prev(base):
        return lambda b, p, i: (b, jnp.maximum(i - 1, 0), base // LANES + p)

    return pl.pallas_call(
        _attention_kernel,
        grid=(batch, N_PAIRS, seq // SUPER),
        in_specs=[
            pl.BlockSpec(blk, col(COL_Q)),
            pl.BlockSpec(blk, col(COL_K)),
            pl.BlockSpec(blk, col_prev(COL_K)),
            pl.BlockSpec(blk, col(COL_V)),
            pl.BlockSpec(blk, col_prev(COL_V)),
            pl.BlockSpec(blk, col(COL_ZA)),
        ],
        out_specs=pl.BlockSpec(blk, lambda b, p, i: (b, i, p)),
        out_shape=jax.ShapeDtypeStruct((batch, seq, ATT_WIDTH), BF16),
        scratch_shapes=[
            pltpu.VMEM((2 * SUPER, LANES), F32),
            pltpu.VMEM((2 * SUPER, LANES), F32),
            pltpu.VMEM((SUPER, LANES), F32),
            pltpu.VMEM((len(DIL_PATTERNS), SUPER, LANES), F32),
            pltpu.VMEM((len(DIL_PATTERNS), SUPER, LANES), F32),
            pltpu.VMEM((len(DIL_PATTERNS), SUPER, LANES), F32),
            pltpu.VMEM((SUPER, LANES), F32),
        ],
        compiler_params=pltpu.CompilerParams(
            dimension_semantics=("arbitrary", "arbitrary", "arbitrary"),
            vmem_limit_bytes=VMEM_LIMIT),
        name="dilated_attention",
    )(proj3, proj3, proj3, proj3, proj3, proj3)


def _rwkv_kernel(r_ref, k_ref, v_ref, z_ref, lora_ref,
                 w0_ref, a0_ref, kk_ref, ka_ref, wa2_ref, g2_ref,
                 rk_ref, lng_ref, lnb_ref,
                 o_ref,
                 state,
                 pm_a, pm_r, pm_kb, pm_bk, pm_d, pm_v,
                 pc_r, pc_k, pc_v, pc_gz,
                 mc_rw, mc_y0, mc_m, mc_g, mc_d,
                 *, blocks_per_stream):
    t = pl.program_id(0)
    first_chain = ((2 * jnp.maximum(t - 1, 0)) % blocks_per_stream) == 0

    @pl.when(t == 0)
    def _():
        for ref in (state, pm_a, pm_r, pm_kb, pm_bk, pm_d, pm_v,
                    pc_r, pc_k, pc_v, pc_gz, mc_rw, mc_y0, mc_m, mc_g, mc_d):
            ref[...] = jnp.zeros_like(ref)

    rr = lax.broadcasted_iota(jnp.int32, (LANES, LANES), 0)
    cc = lax.broadcasted_iota(jnp.int32, (LANES, LANES), 1)
    same_head = (rr // HEAD_DIM) == (cc // HEAD_DIM)
    strict = same_head & ((cc % CHUNK) < (rr % CHUNK))
    incl = same_head & ((cc % CHUNK) <= (rr % CHUNK))
    eye = (rr == cc).astype(F32)
    tril = (lax.broadcasted_iota(jnp.int32, (CHUNK, CHUNK), 1)
            <= lax.broadcasted_iota(jnp.int32, (CHUNK, CHUNK), 0)).astype(BF16)
    head0 = lax.broadcasted_iota(jnp.int32, (CHUNK, LANES), 1) < HEAD_DIM
    zero_blk = jnp.zeros((LANES, LANES), BF16)
    last = slice(CHUNK - 1, CHUNK)

    def bd(x):
        return jnp.concatenate([jnp.where(head0, x, 0.0), jnp.where(head0, 0.0, x)], axis=0)

    rows = [slice(c * CHUNK, (c + 1) * CHUNK) for c in range(N_CHUNKS)]
    cur = {"half": 0, "slot_p": 0, "slot_m": 1}

    def block_rows(c):
        start = cur["half"] * RWKV_BLOCK + c * CHUNK
        return slice(start, start + CHUNK)

    pv = [dict() for _ in range(N_CHUNKS)]

    def prep_lora_matmuls(c):
        d = pv[c]
        win = block_rows(c)
        wa_out = _dot(lora_ref[0, win, 0:LANES], wa2_ref[...].astype(BF16))
        d["lw"] = -(DECAY_SCALE * _sigmoid(w0_ref[...] + wa_out[:, 0:LANES]))
        d["a"] = _sigmoid(a0_ref[...] + wa_out[:, LANES:])
        g = _dot(lora_ref[0, win, LANES:LANES + GATE_PAD], g2_ref[...].astype(BF16))
        z = z_ref[0, block_rows(c), :]
        d["gz"] = g * (z * _sigmoid(z))

    def prep_key_norm(c):
        d = pv[c]
        k = k_ref[0, block_rows(c), :]
        kk = k * kk_ref[...]
        kk = kk * lax.rsqrt(jnp.maximum(_head_sum(kk * kk, head0), KK_NORM_FLOOR ** 2))
        d["kk"] = kk
        d["k2"] = k * (1.0 + (d["a"] - 1.0) * ka_ref[...])
        d["b"] = kk * d["a"]
        parts = _dot(tril, jnp.concatenate(_bf16_pieces(d["lw"], 3), axis=1))
        d["cs"] = parts[:, 0:LANES] + parts[:, LANES:2 * LANES] + parts[:, 2 * LANES:]

    def prep_exp(c):
        d = pv[c]
        d["e_pos"] = jnp.exp(d["cs"])
        d["e_neg"] = jnp.exp(-d["cs"])
        d["e_excl"] = jnp.exp(d["cs"] - d["lw"])
        d["decay_end"] = d["e_pos"][last, :]

    def prep_store_ar(c):
        d = pv[c]
        idx = cur["slot_p"] * N_CHUNKS + c
        pm_a[idx] = bd(-d["kk"] * d["e_excl"]).astype(BF16)
        pm_r[idx] = bd(r_ref[0, block_rows(c), :] * d["e_pos"])

    def prep_store_kbv(c):
        d = pv[c]
        slot_p = cur["slot_p"]
        idx = slot_p * N_CHUNKS + c
        pm_kb[idx] = jnp.concatenate([d["k2"] * d["e_neg"], d["b"] * d["e_neg"]],
                                     axis=0).astype(BF16)
        pm_v[idx] = bd(v_ref[0, block_rows(c), :]).astype(BF16)

    def prep_store_pc(c):
        d = pv[c]
        slot_p = cur["slot_p"]
        pc_r[slot_p, rows[c], :] = r_ref[0, block_rows(c), :]
        pc_k[slot_p, rows[c], :] = d["k2"]
        pc_v[slot_p, rows[c], :] = v_ref[0, block_rows(c), :]
        pc_gz[slot_p, rows[c], :] = d["gz"]

    def prep_store_bk(c):
        d = pv[c]
        idx = cur["slot_p"] * N_CHUNKS + c
        e_end = d["decay_end"] * d["e_neg"]
        bkd_t = jnp.concatenate([bd(d["b"] * e_end), bd(d["k2"] * e_end),
                                 jnp.broadcast_to(d["decay_end"], (LANES, LANES))], axis=0).T
        pm_bk[idx] = bkd_t[:, 0:2 * LANES].astype(BF16)
        pm_d[idx] = bkd_t[:, 2 * LANES:]

    prep_groups = [(prep_lora_matmuls, 110),
                   (prep_key_norm, 150), (prep_exp, 50), (prep_store_ar, 56),
                   (prep_store_kbv, 50), (prep_store_bk, 75)]

    mm = [dict() for _ in range(N_CHUNKS)]

    def mm_scores(c):
        idx = cur["slot_m"] * N_CHUNKS + c
        kb = pm_kb[idx]
        k_t, b_t = kb[0:CHUNK, :], kb[CHUNK:, :]
        lhs = jnp.concatenate([pm_a[idx], pm_r[idx].astype(BF16)], axis=0)
        x = _dot_nt(lhs, jnp.concatenate([k_t, k_t, b_t, b_t], axis=0))
        d = mm[c]
        d["a_ak"] = jnp.where(strict, x[0:LANES, 0:LANES], 0.0).astype(BF16)
        a_ab = jnp.where(strict, x[0:LANES, LANES:], 0.0)
        d["a_r"] = jnp.concatenate([jnp.where(incl, x[LANES:, 0:LANES], 0.0),
                                    jnp.where(incl, x[LANES:, LANES:], 0.0)],
                                   axis=1).astype(BF16)
        d["q"] = a_ab.astype(BF16)
        d["p"] = eye + a_ab

    def mm_square(c):
        d = mm[c]
        d["q"] = _dot(d["q"], d["q"]).astype(BF16)

    def mm_double(c):
        d = mm[c]
        qp = _dot(d["q"], jnp.concatenate([d["q"], d["p"].astype(BF16)], axis=1))
        d["q"] = qp[:, 0:LANES].astype(BF16)
        d["p"] = d["p"] + qp[:, LANES:]

    def mm_inverse(c):
        d = mm[c]
        d["t"] = (d["p"] + _dot(d["q"], d["p"].astype(BF16))).astype(BF16)

    def mm_akv(c):
        d = mm[c]
        d["ak_v"] = _dot(d["a_ak"], pm_v[cur["slot_m"] * N_CHUNKS + c]).astype(BF16)

    def mm_wu(c):
        d = mm[c]
        rhs = jnp.concatenate([pm_a[cur["slot_m"] * N_CHUNKS + c], d["ak_v"]], axis=1)
        d["wu"] = _dot(d["t"], rhs).astype(BF16)

    def mm_yw(c):
        d = mm[c]
        idx = cur["slot_m"] * N_CHUNKS + c
        wu = d["wu"]
        rhs = jnp.concatenate(
            [jnp.concatenate([pm_v[idx], zero_blk], axis=1),
             jnp.concatenate([wu[:, LANES:], wu[:, 0:LANES]], axis=1)], axis=0)
        yw = _dot(d["a_r"], rhs)
        mc_y0[idx] = yw[:, 0:LANES]
        mc_rw[idx] = (pm_r[idx] + yw[:, LANES:]).astype(BF16)

    def mm_mg(c):
        idx = cur["slot_m"] * N_CHUNKS + c
        rhs = jnp.concatenate([mm[c]["wu"],
                               jnp.concatenate([zero_blk, pm_v[idx]], axis=1)], axis=0)
        mg = _dot(pm_bk[idx], rhs)
        mc_m[idx] = mg[:, 0:LANES].astype(BF16)
        mc_g[idx] = mg[:, LANES:]
        mc_d[idx] = pm_d[idx]

    matmul_stages = [mm_scores, mm_square, mm_double, mm_double, mm_double, mm_double,
                     mm_inverse, mm_akv, mm_wu, mm_yw, mm_mg]

    chain = {"hs": jnp.where(first_chain, 0.0, state[...])}

    cv = [dict() for _ in range(N_CHUNKS)]

    def chain_core(c):
        idx = cur["slot_p"] * N_CHUNKS + c
        hs = chain["hs"]
        hs_b = hs.astype(BF16)
        y = _dot(mc_rw[idx], hs_b) + mc_y0[idx]
        chain["hs"] = hs * mc_d[idx] + _dot(mc_m[idx], hs_b) + mc_g[idx]
        cv[c]["y"] = y[0:CHUNK, :] + y[CHUNK:, :]

    def chain_mean(c):
        slot_p, rws, d = cur["slot_p"], rows[c], cv[c]
        d["yc"] = d["y"] - _head_sum(d["y"], head0) * (1.0 / HEAD_DIM)
        d["bonus"] = _head_sum(pc_r[slot_p, rws, :] * pc_k[slot_p, rws, :] * rk_ref[...],
                               head0) * pc_v[slot_p, rws, :]

    def chain_finish(c):
        slot_p, rws, d = cur["slot_p"], rows[c], cv[c]
        var = _head_sum(d["yc"] * d["yc"], head0) * (1.0 / HEAD_DIM)
        yn = d["yc"] * lax.rsqrt(var + LNX_EPS) * lng_ref[...] + lnb_ref[...]
        o_ref[0, block_rows(c), :] = ((yn + d["bonus"]) * pc_gz[slot_p, rws, :]
                                      ).astype(o_ref.dtype)

    n_rounds = len(matmul_stages)
    per_round = -(-N_CHUNKS // (n_rounds - 3))
    core_gap = N_CHUNKS // per_round

    def chain_cores_of(rnd):
        return range(min(rnd * per_round, N_CHUNKS), min((rnd + 1) * per_round, N_CHUNKS))

    n_slots = (n_rounds - 1) * N_CHUNKS
    total = float(N_CHUNKS * sum(w for _, w in prep_groups))
    prep_slots = [[] for _ in range(n_slots)]
    done = 0.0
    for fn, weight in prep_groups:
        for c in range(N_CHUNKS):
            prep_slots[min(int(done / total * n_slots), n_slots - 1)].append((fn, c))
            done += weight
    for half in (0, 1):
        cur.update(half=half, slot_p=half, slot_m=1 - half)
        for rnd, mm_stage in enumerate(matmul_stages):
            if rnd >= 1:
                for chunk in chain_cores_of(rnd - 1):
                    chain_mean(chunk)
            if rnd >= 2:
                for chunk in chain_cores_of(rnd - 2):
                    chain_finish(chunk)
            cores = list(chain_cores_of(rnd))
            for c in range(N_CHUNKS):
                if c % core_gap == 0 and c // core_gap < len(cores):
                    chain_core(cores[c // core_gap])
                mm_stage(c)
                if rnd == n_rounds - 1:
                    prep_store_pc(c)
                else:
                    for fn, pc in prep_slots[rnd * N_CHUNKS + c]:
                        fn(pc)
    state[...] = chain["hs"]


def _rwkv(proj3, lora3, w0, a0, k_k, k_a, r_k, lnx_g, lnx_b, wa2, g2p):
    batch, seq, _ = proj3.shape
    bps = seq // RWKV_BLOCK
    assert bps % 2 == 0
    wps = bps // 2
    n_total = batch * N_PAIRS * wps
    blk = (1, 2 * RWKV_BLOCK, LANES)

    def split(n):
        return n // (N_PAIRS * wps), n % wps, (n // wps) % N_PAIRS

    def prep_n(s):
        return jnp.minimum(s, n_total - 1)

    def chain_n(s):
        return jnp.maximum(s - 1, 0)

    def col(base):
        def index(s):
            b, i, p = split(prep_n(s))
            return b, i, base // LANES + p
        return index

    def lora_index(s):
        b, i, _ = split(prep_n(s))
        return b, i, 0

    def out_index(s):
        return split(chain_n(s))

    vec_prep = pl.BlockSpec((1, LANES), lambda s: (0, split(prep_n(s))[2]))
    vec_chain = pl.BlockSpec((1, LANES), lambda s: (0, split(chain_n(s))[2]))
    n_slots = 2 * N_CHUNKS
    return pl.pallas_call(
        functools.partial(_rwkv_kernel, blocks_per_stream=bps),
        grid=(n_total + 1,),
        in_specs=[
            pl.BlockSpec(blk, col(COL_R)),
            pl.BlockSpec(blk, col(COL_RK)),
            pl.BlockSpec(blk, col(COL_RV)),
            pl.BlockSpec(blk, col(COL_ZR)),
            pl.BlockSpec((1, 2 * RWKV_BLOCK, LORA_PAD), lora_index),
            vec_prep, vec_prep, vec_prep, vec_prep,
            pl.BlockSpec((LANES, 2 * LANES), lambda s: (0, split(prep_n(s))[2])),
            pl.BlockSpec((GATE_PAD, LANES), lambda s: (0, split(prep_n(s))[2])),
            vec_chain, vec_chain, vec_chain,
        ],
        out_specs=pl.BlockSpec(blk, out_index),
        out_shape=jax.ShapeDtypeStruct((batch, seq, RWKV_WIDTH), BF16),
        scratch_shapes=[
            pltpu.VMEM((LANES, LANES), F32),
            pltpu.VMEM((n_slots, LANES, LANES), BF16),
            pltpu.VMEM((n_slots, LANES, LANES), F32),
            pltpu.VMEM((n_slots, LANES, LANES), BF16),
            pltpu.VMEM((n_slots, LANES, 2 * LANES), BF16),
            pltpu.VMEM((n_slots, LANES, LANES), F32),
            pltpu.VMEM((n_slots, LANES, LANES), BF16),
            pltpu.VMEM((2, RWKV_BLOCK, LANES), F32),
            pltpu.VMEM((2, RWKV_BLOCK, LANES), F32),
            pltpu.VMEM((2, RWKV_BLOCK, LANES), F32),
            pltpu.VMEM((2, RWKV_BLOCK, LANES), F32),
            pltpu.VMEM((n_slots, LANES, LANES), BF16),
            pltpu.VMEM((n_slots, LANES, LANES), F32),
            pltpu.VMEM((n_slots, LANES, LANES), BF16),
            pltpu.VMEM((n_slots, LANES, LANES), F32),
            pltpu.VMEM((n_slots, LANES, LANES), F32),
        ],
        compiler_params=pltpu.CompilerParams(
            dimension_semantics=("arbitrary",),
            vmem_limit_bytes=VMEM_LIMIT),
        name="rwkv7_time_mix",
    )(proj3, proj3, proj3, proj3, lora3, w0, a0, k_k, k_a, wa2, g2p, r_k, lnx_g, lnx_b)


def _out_proj_kernel(att_ref, rwk_ref, wa_ref, wr_ref, x_ref, g_ref, o_ref, *, normalize):
    y = x_ref[...] + _dot(att_ref[...], wa_ref[...]) + _dot(rwk_ref[...], wr_ref[...])
    if normalize:
        ms = jnp.mean(y * y, axis=-1, keepdims=True)
        y = y * lax.rsqrt(ms + NORM_EPS) * g_ref[...]
    o_ref[...] = y


def _out_proj(att2, rwk2, w_out_b, x2, g, normalize):
    tokens = x2.shape[0]
    return pl.pallas_call(
        functools.partial(_out_proj_kernel, normalize=normalize),
        grid=(tokens // OUT_TM,),
        in_specs=[
            pl.BlockSpec((OUT_TM, ATT_WIDTH), lambda i: (i, 0)),
            pl.BlockSpec((OUT_TM, RWKV_WIDTH), lambda i: (i, 0)),
            pl.BlockSpec((ATT_WIDTH, D_MODEL), lambda i: (0, 0)),
            pl.BlockSpec((RWKV_WIDTH, D_MODEL), lambda i: (ATT_WIDTH // RWKV_WIDTH, 0)),
            pl.BlockSpec((OUT_TM, D_MODEL), lambda i: (i, 0)),
            pl.BlockSpec((1, D_MODEL), lambda i: (0, 0)),
        ],
        out_specs=pl.BlockSpec((OUT_TM, D_MODEL), lambda i: (i, 0)),
        out_shape=jax.ShapeDtypeStruct((tokens, D_MODEL), F32),
        compiler_params=pltpu.CompilerParams(
            dimension_semantics=("arbitrary",),
            vmem_limit_bytes=VMEM_LIMIT),
        name="out_proj",
    )(att2, rwk2, w_out_b, w_out_b, x2, g)


def _rope_tables(seq):
    inv_freq = ROPE_THETA ** (-jnp.arange(0, HEAD_DIM, 2, dtype=F32) / HEAD_DIM)
    ang = jnp.arange(seq, dtype=jnp.int32).astype(F32)[:, None] * inv_freq[None, :]
    cos, sin = jnp.cos(ang), jnp.sin(ang)
    reps = LANES // HEAD_DIM
    cos_t = jnp.tile(jnp.concatenate([cos, cos], axis=1), (1, reps))
    sin_t = jnp.tile(jnp.concatenate([-sin, sin], axis=1), (1, reps))
    return cos_t, sin_t


def _layer(x, norm_g, w_in, shift_mu, w0, w2, a0, a2, g2, k_k, k_a, r_k,
           lnx_g, lnx_b, w_out, cos_t, sin_t):
    batch, seq, _ = x.shape
    a_w, r_w = ATT_WIDTH, RWKV_WIDTH
    lo = 4 * a_w + 3 * r_w
    n_lora = DECAY_LORA + AAA_LORA + GATE_LORA
    assert lo == COL_ZR
    w_all = w_in.astype(BF16)
    w_gate = w_all[:, lo + n_lora:]
    w_lora = jnp.pad(w_all[:, lo:lo + n_lora], ((0, 0), (0, LORA_PAD - n_lora)))
    mu_main = shift_mu[None, 0:3 * r_w]
    mu_lora = jnp.pad(shift_mu[None, 3 * r_w:], ((0, 0), (0, LORA_PAD - n_lora)))
    w2p = jnp.pad(w2, ((0, LANES - DECAY_LORA), (0, 0))).reshape(LANES, N_PAIRS, LANES)
    a2p = jnp.pad(a2, ((DECAY_LORA, LANES - DECAY_LORA - AAA_LORA), (0, 0))
                  ).reshape(LANES, N_PAIRS, LANES)
    wa2 = jnp.concatenate([w2p, a2p], axis=2).reshape(LANES, 2 * r_w)
    g2p = jnp.pad(g2, ((0, GATE_PAD - GATE_LORA), (0, 0)))

    x2 = x.reshape(batch * seq, D_MODEL)
    proj, lora = _in_proj(x2, norm_g[None, :], w_all, w_gate, w_lora, cos_t, sin_t,
                          mu_main, mu_lora, seq)
    proj3 = proj.reshape(batch, seq, PROJ_WIDTH)
    lora3 = lora.reshape(batch, seq, LORA_PAD)
    att = _attention(proj3)
    rwk = _rwkv(proj3, lora3, w0[None, :], a0[None, :],
                k_k[None, :], k_a[None, :], r_k.reshape(1, r_w), lnx_g[None, :],
                lnx_b[None, :], wa2, g2p)
    w_out_b = w_out.astype(BF16)
    return att.reshape(batch * seq, a_w), rwk.reshape(batch * seq, r_w), w_out_b, x2


def kernel(x, norm_g, w_in, shift_mu, w0, w2, a0, a2, g2, k_k, k_a, r_k,
           lnx_g, lnx_b, w_out, final_g):
    batch, seq, _ = x.shape
    depth = norm_g.shape[0]
    assert seq % SUPER == 0 and seq % IN_TM == 0 and (batch * seq) % OUT_TM == 0
    cos_t, sin_t = _rope_tables(seq)
    for l in range(depth):
        att2, rwk2, w_out_b, x2 = _layer(
            x, norm_g[l], w_in[l], shift_mu[l], w0[l], w2[l], a0[l], a2[l], g2[l],
            k_k[l], k_a[l], r_k[l], lnx_g[l], lnx_b[l], w_out[l], cos_t, sin_t)
        y2 = _out_proj(att2, rwk2, w_out_b, x2, final_g[None, :],
                       normalize=(l == depth - 1))
        x = y2.reshape(batch, seq, D_MODEL)
    return x
```

```python
import functools

import jax
import jax.numpy as jnp
from jax import lax
from jax.experimental import pallas as pl
from jax.experimental.pallas import tpu as pltpu

F32 = jnp.float32
BF16 = jnp.bfloat16

D_MODEL = 2048
HEAD_DIM = 64
ATT_WIDTH = 1024
RWKV_WIDTH = 1024
DIL_PATTERNS = ((128, 1), (512, 4), (2048, 16))
ROPE_THETA = 10000.0
DECAY_LORA = 64
AAA_LORA = 64
GATE_LORA = 160
NORM_EPS = 1e-5
LNX_EPS = 64e-5
DECAY_SCALE = 0.6065306597126334
KK_NORM_FLOOR = 1e-12
LOG2_E = 1.4426950408889634

LANES = 128
SUBLANES = 8
N_PAIRS = ATT_WIDTH // LANES
N_BACK = 128
Q_BLOCK = 128
SUPER = 2048
CHUNK = 64
RWKV_BLOCK = 1024
N_CHUNKS = RWKV_BLOCK // CHUNK
LORA_PAD = 512
GATE_PAD = 256

COL_Q, COL_K, COL_V, COL_ZA = 0, 1024, 2048, 3072
COL_R, COL_RK, COL_RV, COL_ZR = 4096, 5120, 6144, 7168
PROJ_WIDTH = 8192

IN_TM, IN_TN = 1024, 1024
IN_SUB = 512
N_ROPE_STEPS = (COL_V - COL_Q) // IN_TN
N_Q_STEPS = (COL_K - COL_Q) // IN_TN
N_ALIGNED_STEPS = COL_ZR // IN_TN
N_SHIFT_LO = COL_R // IN_TN
OUT_TM = 512
VMEM_LIMIT = 56 * 1024 * 1024


def _dot(a, b):
    return jnp.dot(a, b, preferred_element_type=F32)


def _dot_nt(a, b):
    return lax.dot_general(a, b, (((1,), (1,)), ((), ())), preferred_element_type=F32)


def _sigmoid(x):
    return 0.5 * jnp.tanh(0.5 * x) + 0.5


def _bf16_pieces(x, terms):
    pieces = []
    for _ in range(terms):
        piece = x.astype(BF16)
        pieces.append(piece)
        x = x - piece.astype(F32)
    return pieces


def _head_sum(x, head0):
    first = jnp.sum(jnp.where(head0, x, 0.0), axis=-1, keepdims=True)
    second = jnp.sum(jnp.where(head0, 0.0, x), axis=-1, keepdims=True)
    return jnp.where(head0, first, second)


def _in_proj_kernel(x_ref, g_ref, w_ref, wz_ref, wl_ref, cos_ref, sin_ref,
                    mu_ref, mul_ref, o_ref, lora_ref, h_ref, carry_ref, carry_lora_ref,
                    *, pos_blocks):
    i = pl.program_id(0)
    j = pl.program_id(1)
    subs = [slice(s * IN_SUB, (s + 1) * IN_SUB) for s in range(IN_TN // IN_SUB)]
    seq_start = (i % pos_blocks) == 0
    last_row = slice(IN_TM - 1, IN_TM)

    def token_shift(u, carry_row, mu):
        row0 = lax.broadcasted_iota(jnp.int32, u.shape, 0) == 0
        prev = jnp.where(row0, jnp.where(seq_start, 0.0, carry_row), pltpu.roll(u, 1, 0))
        return u + (prev - u) * mu

    @pl.when((i == 0) & (j == 0))
    def _():
        carry_ref[...] = jnp.zeros_like(carry_ref)
        carry_lora_ref[...] = jnp.zeros_like(carry_lora_ref)

    @pl.when(j == 0)
    def _():
        x = x_ref[...]
        ms = jnp.mean(x * x, axis=-1, keepdims=True)
        h_ref[...] = (x * lax.rsqrt(ms + NORM_EPS) * g_ref[...]).astype(BF16)
        lora = _dot(h_ref[...], wl_ref[...])
        shifted = token_shift(lora, carry_lora_ref[0:1, :], mul_ref[...])
        carry_lora_ref[0:1, :] = lora[last_row, :]
        lane = lax.broadcasted_iota(jnp.int32, (IN_TM, LANES), 1)
        wa = shifted[:, 0:LANES]
        lora_ref[:, 0:LANES] = jnp.where(lane < DECAY_LORA, jnp.tanh(wa), wa).astype(BF16)
        lora_ref[:, LANES:LANES + GATE_PAD] = _sigmoid(
            shifted[:, LANES:LANES + GATE_PAD]).astype(BF16)
        lora_ref[:, LANES + GATE_PAD:] = jnp.zeros((IN_TM, LORA_PAD - LANES - GATE_PAD), BF16)

    @pl.when(j < N_ROPE_STEPS)
    def _():
        reps = IN_SUB // LANES
        cos = jnp.concatenate([cos_ref[...]] * reps, axis=1)
        sin = jnp.concatenate([sin_ref[...]] * reps, axis=1)
        lane = lax.broadcasted_iota(jnp.int32, (IN_TM, IN_SUB), 1)
        first_half = (lane % HEAD_DIM) < (HEAD_DIM // 2)
        scale = jnp.where(j < N_Q_STEPS, LOG2_E * HEAD_DIM ** -0.5, 1.0).astype(F32)
        for sub in subs:
            acc = _dot(h_ref[...], w_ref[:, sub])
            partner = jnp.where(first_half,
                                pltpu.roll(acc, IN_SUB - HEAD_DIM // 2, 1),
                                pltpu.roll(acc, HEAD_DIM // 2, 1))
            o_ref[:, sub] = (acc * cos + partner * sin) * scale

    @pl.when((j >= N_ROPE_STEPS) & (j < N_SHIFT_LO))
    def _():
        for sub in subs:
            o_ref[:, sub] = _dot(h_ref[...], w_ref[:, sub])

    @pl.when((j >= N_SHIFT_LO) & (j < N_ALIGNED_STEPS))
    def _():
        slot = j - N_SHIFT_LO
        for sub in subs:
            acc = _dot(h_ref[...], w_ref[:, sub])
            o_ref[:, sub] = token_shift(acc, carry_ref[slot, 0:1, sub], mu_ref[:, sub])
            carry_ref[slot, 0:1, sub] = acc[last_row, :]

    @pl.when(j >= N_ALIGNED_STEPS)
    def _():
        for sub in subs:
            o_ref[:, sub] = _dot(h_ref[...], wz_ref[:, sub])


def _in_proj(x2, g, w_all, w_gate, w_lora, cos_t, sin_t, mu_main, mu_lora, seq):
    tokens = x2.shape[0]
    pos_blocks = seq // IN_TM
    n_shift = N_ALIGNED_STEPS - N_SHIFT_LO

    def mu_index(i, j):
        return 0, jnp.clip(j - N_SHIFT_LO, 0, n_shift - 1)

    return pl.pallas_call(
        functools.partial(_in_proj_kernel, pos_blocks=pos_blocks),
        grid=(tokens // IN_TM, PROJ_WIDTH // IN_TN),
        in_specs=[
            pl.BlockSpec((IN_TM, D_MODEL), lambda i, j: (i, 0)),
            pl.BlockSpec((1, D_MODEL), lambda i, j: (0, 0)),
            pl.BlockSpec((D_MODEL, IN_TN), lambda i, j: (0, jnp.minimum(j, N_ALIGNED_STEPS - 1))),
            pl.BlockSpec((D_MODEL, IN_TN), lambda i, j: (0, 0)),
            pl.BlockSpec((D_MODEL, LORA_PAD), lambda i, j: (0, 0)),
            pl.BlockSpec((IN_TM, LANES), lambda i, j: (i % pos_blocks, 0)),
            pl.BlockSpec((IN_TM, LANES), lambda i, j: (i % pos_blocks, 0)),
            pl.BlockSpec((1, IN_TN), mu_index),
            pl.BlockSpec((1, LORA_PAD), lambda i, j: (0, 0)),
        ],
        out_specs=[
            pl.BlockSpec((IN_TM, IN_TN), lambda i, j: (i, j)),
            pl.BlockSpec((IN_TM, LORA_PAD), lambda i, j: (i, 0)),
        ],
        out_shape=[
            jax.ShapeDtypeStruct((tokens, PROJ_WIDTH), F32),
            jax.ShapeDtypeStruct((tokens, LORA_PAD), BF16),
        ],
        scratch_shapes=[
            pltpu.VMEM((IN_TM, D_MODEL), BF16),
            pltpu.VMEM((n_shift, SUBLANES, IN_TN), F32),
            pltpu.VMEM((SUBLANES, LORA_PAD), F32),
        ],
        compiler_params=pltpu.CompilerParams(
            dimension_semantics=("arbitrary", "arbitrary"),
            vmem_limit_bytes=VMEM_LIMIT),
        name="in_proj",
    )(x2, g, w_all, w_gate, w_lora, cos_t, sin_t, mu_main, mu_lora)


def _attention_kernel(q_ref, kc_ref, kp_ref, vc_ref, vp_ref, z_ref, o_ref,
                      k4, v4, q4, acc_s, m_s, den_s, res):
    i = pl.program_id(2)
    slab = 2 * SUPER // 4
    qslab = SUPER // 4
    for b in range(4):
        q4[b * qslab:(b + 1) * qslab, :] = q_ref[0, pl.ds(b, qslab, stride=4), :]
        for dst, prev_ref, cur_ref in ((k4, kp_ref, kc_ref), (v4, vp_ref, vc_ref)):
            dst[b * slab:b * slab + slab // 2, :] = prev_ref[0, pl.ds(b, slab // 2, stride=4), :]
            dst[b * slab + slab // 2:(b + 1) * slab, :] = cur_ref[0, pl.ds(b, slab // 2,
                                                                          stride=4), :]

    qi = lax.broadcasted_iota(jnp.int32, (Q_BLOCK, 2 * Q_BLOCK), 0)
    ki = lax.broadcasted_iota(jnp.int32, (Q_BLOCK, 2 * Q_BLOCK), 1)
    rel = Q_BLOCK + qi - ki
    band = (rel >= 0) & (rel <= N_BACK)
    cur = ki >= Q_BLOCK
    lane = lax.broadcasted_iota(jnp.int32, (Q_BLOCK, LANES), 1)
    head0 = lane < HEAD_DIM

    def rows(start, size, stride):
        return pl.ds(start, size, stride=stride) if stride > 1 else pl.ds(start, size)

    def window(buf4, cur_ref, prev_ref, q_start, dil):
        k_start = SUPER + q_start - Q_BLOCK * dil
        if dil > 1:
            return buf4[rows((k_start % 4) * slab + k_start // 4, 2 * Q_BLOCK, dil // 4), :]
        if k_start >= SUPER:
            return cur_ref[0, pl.ds(k_start - SUPER, 2 * Q_BLOCK), :]
        return jnp.concatenate([prev_ref[0, pl.ds(k_start, Q_BLOCK), :],
                                cur_ref[0, pl.ds(q_start, Q_BLOCK), :]], axis=0)

    def block(pat, dil, q_start):
        if dil > 1:
            rows_q = rows((q_start % 4) * qslab + q_start // 4, Q_BLOCK, dil // 4)
            qs = q4[rows_q, :]
        else:
            rows_q = pl.ds(q_start, Q_BLOCK)
            qs = q_ref[0, rows_q, :]
        kw = window(k4, kc_ref, kp_ref, q_start, dil).astype(BF16)
        vw = window(v4, vc_ref, vp_ref, q_start, dil).astype(BF16)
        vw1 = jnp.concatenate([vw, jnp.ones((2 * Q_BLOCK, LANES), BF16)], axis=1)
        if q_start >= Q_BLOCK * dil:
            valid = band
        else:
            valid = band & (cur | (i > 0))
        accs, ms, dens = [], [], []
        for hm in (head0, ~head0):
            qh = jnp.where(hm, qs, 0.0).astype(BF16)
            s = _dot_nt(qh, kw)
            s = jnp.where(valid, s, -jnp.inf)
            m = jnp.max(s, axis=-1, keepdims=True)
            p = jnp.exp2(s - m)
            pv = _dot(p.astype(BF16), vw1)
            dens.append(pv[:, LANES:])
            ms.append(m)
            accs.append(pv[:, 0:LANES])
        acc_s[pat, rows_q, :] = jnp.where(head0, accs[0], accs[1])
        m_s[pat, rows_q, :] = jnp.where(head0, ms[0], ms[1])
        den_s[pat, rows_q, :] = jnp.where(head0, dens[0], dens[1])

    per_pattern = []
    for pat, (window_len, dil) in enumerate(DIL_PATTERNS):
        span = Q_BLOCK * dil
        per_pattern.append([(pat, dil, s_idx * span + r)
                            for s_idx in range(SUPER // span) for r in range(dil)])
    for group in zip(*per_pattern):
        for args in group:
            block(*args)

    for b in range(4):
        natural = pl.ds(b, qslab, stride=4)
        grouped = pl.ds(b * qslab, qslab)
        m0, m1, m2 = m_s[0, natural, :], m_s[1, grouped, :], m_s[2, grouped, :]
        mx = jnp.maximum(jnp.maximum(m0, m1), m2)
        e0, e1, e2 = jnp.exp2(m0 - mx), jnp.exp2(m1 - mx), jnp.exp2(m2 - mx)
        num = e0 * acc_s[0, natural, :] + e1 * acc_s[1, grouped, :] + e2 * acc_s[2, grouped, :]
        den = e0 * den_s[0, natural, :] + e1 * den_s[1, grouped, :] + e2 * den_s[2, grouped, :]
        z = z_ref[0, natural, :]
        res[natural, :] = num / den * (z * _sigmoid(z))
    o_ref[0] = res[...].astype(o_ref.dtype)


def _attention(proj3):
    batch, seq, _ = proj3.shape
    blk = (1, SUPER, LANES)

    def col(base):
        return lambda b, p, i: (b, i, base // LANES + p)

    def col_prev(base):
        return lambda b, p, i: (b, jnp.maximum(i - 1, 0), base // LANES + p)

    return pl.pallas_call(
        _attention_kernel,
        grid=(batch, N_PAIRS, seq // SUPER),
        in_specs=[
            pl.BlockSpec(blk, col(COL_Q)),
            pl.BlockSpec(blk, col(COL_K)),
            pl.BlockSpec(blk, col_prev(COL_K)),
            pl.BlockSpec(blk, col(COL_V)),
            pl.BlockSpec(blk, col_prev(COL_V)),
            pl.BlockSpec(blk, col(COL_ZA)),
        ],
        out_specs=pl.BlockSpec(blk, lambda b, p, i: (b, i, p)),
        out_shape=jax.ShapeDtypeStruct((batch, seq, ATT_WIDTH), BF16),
        scratch_shapes=[
            pltpu.VMEM((2 * SUPER, LANES), F32),
            pltpu.VMEM((2 * SUPER, LANES), F32),
            pltpu.VMEM((SUPER, LANES), F32),
            pltpu.VMEM((len(DIL_PATTERNS), SUPER, LANES), F32),
            pltpu.VMEM((len(DIL_PATTERNS), SUPER, LANES), F32),
            pltpu.VMEM((len(DIL_PATTERNS), SUPER, LANES), F32),
            pltpu.VMEM((SUPER, LANES), F32),
        ],
        compiler_params=pltpu.CompilerParams(
            dimension_semantics=("arbitrary", "arbitrary", "arbitrary"),
            vmem_limit_bytes=VMEM_LIMIT),
        name="dilated_attention",
    )(proj3, proj3, proj3, proj3, proj3, proj3)


def _rwkv_kernel(r_ref, k_ref, v_ref, z_ref, lora_ref,
                 w0_ref, a0_ref, kk_ref, ka_ref, wa2_ref, g2_ref,
                 rk_ref, lng_ref, lnb_ref,
                 o_ref,
                 state,
                 pm_a, pm_r, pm_kb, pm_bk, pm_d, pm_v,
                 pc_r, pc_k, pc_v, pc_gz,
                 mc_rw, mc_y0, mc_m, mc_g, mc_d,
                 *, blocks_per_stream):
    t = pl.program_id(0)
    first_chain = ((2 * jnp.maximum(t - 1, 0)) % blocks_per_stream) == 0

    @pl.when(t == 0)
    def _():
        for ref in (state, pm_a, pm_r, pm_kb, pm_bk, pm_d, pm_v,
                    pc_r, pc_k, pc_v, pc_gz, mc_rw, mc_y0, mc_m, mc_g, mc_d):
            ref[...] = jnp.zeros_like(ref)

    rr = lax.broadcasted_iota(jnp.int32, (LANES, LANES), 0)
    cc = lax.broadcasted_iota(jnp.int32, (LANES, LANES), 1)
    same_head = (rr // HEAD_DIM) == (cc // HEAD_DIM)
    strict = same_head & ((cc % CHUNK) < (rr % CHUNK))
    incl = same_head & ((cc % CHUNK) <= (rr % CHUNK))
    eye = (rr == cc).astype(F32)
    tril = (lax.broadcasted_iota(jnp.int32, (CHUNK, CHUNK), 1)
            <= lax.broadcasted_iota(jnp.int32, (CHUNK, CHUNK), 0)).astype(BF16)
    head0 = lax.broadcasted_iota(jnp.int32, (CHUNK, LANES), 1) < HEAD_DIM
    zero_blk = jnp.zeros((LANES, LANES), BF16)
    last = slice(CHUNK - 1, CHUNK)

    def bd(x):
        return jnp.concatenate([jnp.where(head0, x, 0.0), jnp.where(head0, 0.0, x)], axis=0)

    rows = [slice(c * CHUNK, (c + 1) * CHUNK) for c in range(N_CHUNKS)]
    cur = {"half": 0, "slot_p": 0, "slot_m": 1}

    def block_rows(c):
        start = cur["half"] * RWKV_BLOCK + c * CHUNK
        return slice(start, start + CHUNK)

    pv = [dict() for _ in range(N_CHUNKS)]

    def prep_lora_matmuls(c):
        d = pv[c]
        win = block_rows(c)
        wa_out = _dot(lora_ref[0, win, 0:LANES], wa2_ref[...].astype(BF16))
        d["lw"] = -(DECAY_SCALE * _sigmoid(w0_ref[...] + wa_out[:, 0:LANES]))
        d["a"] = _sigmoid(a0_ref[...] + wa_out[:, LANES:])
        g = _dot(lora_ref[0, win, LANES:LANES + GATE_PAD], g2_ref[...].astype(BF16))
        z = z_ref[0, block_rows(c), :]
        d["gz"] = g * (z * _sigmoid(z))

    def prep_key_norm(c):
        d = pv[c]
        k = k_ref[0, block_rows(c), :]
        kk = k * kk_ref[...]
        kk = kk * lax.rsqrt(jnp.maximum(_head_sum(kk * kk, head0), KK_NORM_FLOOR ** 2))
        d["kk"] = kk
        d["k2"] = k * (1.0 + (d["a"] - 1.0) * ka_ref[...])
        d["b"] = kk * d["a"]
        parts = _dot(tril, jnp.concatenate(_bf16_pieces(d["lw"], 3), axis=1))
        d["cs"] = parts[:, 0:LANES] + parts[:, LANES:2 * LANES] + parts[:, 2 * LANES:]

    def prep_exp(c):
        d = pv[c]
        d["e_pos"] = jnp.exp(d["cs"])
        d["e_neg"] = jnp.exp(-d["cs"])
        d["e_excl"] = jnp.exp(d["cs"] - d["lw"])
        d["decay_end"] = d["e_pos"][last, :]

    def prep_store_ar(c):
        d = pv[c]
        idx = cur["slot_p"] * N_CHUNKS + c
        pm_a[idx] = bd(-d["kk"] * d["e_excl"]).astype(BF16)
        pm_r[idx] = bd(r_ref[0, block_rows(c), :] * d["e_pos"])

    def prep_store_kbv(c):
        d = pv[c]
        slot_p = cur["slot_p"]
        idx = slot_p * N_CHUNKS + c
        pm_kb[idx] = jnp.concatenate([d["k2"] * d["e_neg"], d["b"] * d["e_neg"]],
                                     axis=0).astype(BF16)
        pm_v[idx] = bd(v_ref[0, block_rows(c), :]).astype(BF16)

    def prep_store_pc(c):
        d = pv[c]
        slot_p = cur["slot_p"]
        pc_r[slot_p, rows[c], :] = r_ref[0, block_rows(c), :]
        pc_k[slot_p, rows[c], :] = d["k2"]
        pc_v[slot_p, rows[c], :] = v_ref[0, block_rows(c), :]
        pc_gz[slot_p, rows[c], :] = d["gz"]

    def prep_store_bk(c):
        d = pv[c]
        idx = cur["slot_p"] * N_CHUNKS + c
        e_end = d["decay_end"] * d["e_neg"]
        bkd_t = jnp.concatenate([bd(d["b"] * e_end), bd(d["k2"] * e_end),
                                 jnp.broadcast_to(d["decay_end"], (LANES, LANES))], axis=0).T
        pm_bk[idx] = bkd_t[:, 0:2 * LANES].astype(BF16)
        pm_d[idx] = bkd_t[:, 2 * LANES:]

    prep_groups = [(prep_lora_matmuls, 110),
                   (prep_key_norm, 150), (prep_exp, 50), (prep_store_ar, 56),
                   (prep_store_kbv, 50), (prep_store_bk, 75)]

    mm = [dict() for _ in range(N_CHUNKS)]

    def mm_scores(c):
        idx = cur["slot_m"] * N_CHUNKS + c
        kb = pm_kb[idx]
        k_t, b_t = kb[0:CHUNK, :], kb[CHUNK:, :]
        lhs = jnp.concatenate([pm_a[idx], pm_r[idx].astype(BF16)], axis=0)
        x = _dot_nt(lhs, jnp.concatenate([k_t, k_t, b_t, b_t], axis=0))
        d = mm[c]
        d["a_ak"] = jnp.where(strict, x[0:LANES, 0:LANES], 0.0).astype(BF16)
        a_ab = jnp.where(strict, x[0:LANES, LANES:], 0.0)
        d["a_r"] = jnp.concatenate([jnp.where(incl, x[LANES:, 0:LANES], 0.0),
                                    jnp.where(incl, x[LANES:, LANES:], 0.0)],
                                   axis=1).astype(BF16)
        d["q"] = a_ab.astype(BF16)
        d["p"] = eye + a_ab

    def mm_square(c):
        d = mm[c]
        d["q"] = _dot(d["q"], d["q"]).astype(BF16)

    def mm_double(c):
        d = mm[c]
        qp = _dot(d["q"], jnp.concatenate([d["q"], d["p"].astype(BF16)], axis=1))
        d["q"] = qp[:, 0:LANES].astype(BF16)
        d["p"] = d["p"] + qp[:, LANES:]

    def mm_inverse(c):
        d = mm[c]
        d["t"] = (d["p"] + _dot(d["q"], d["p"].astype(BF16))).astype(BF16)

    def mm_akv(c):
        d = mm[c]
        d["ak_v"] = _dot(d["a_ak"], pm_v[cur["slot_m"] * N_CHUNKS + c]).astype(BF16)

    def mm_wu(c):
        d = mm[c]
        rhs = jnp.concatenate([pm_a[cur["slot_m"] * N_CHUNKS + c], d["ak_v"]], axis=1)
        d["wu"] = _dot(d["t"], rhs).astype(BF16)

    def mm_yw(c):
        d = mm[c]
        idx = cur["slot_m"] * N_CHUNKS + c
        wu = d["wu"]
        rhs = jnp.concatenate(
            [jnp.concatenate([pm_v[idx], zero_blk], axis=1),
             jnp.concatenate([wu[:, LANES:], wu[:, 0:LANES]], axis=1)], axis=0)
        yw = _dot(d["a_r"], rhs)
        mc_y0[idx] = yw[:, 0:LANES]
        mc_rw[idx] = (pm_r[idx] + yw[:, LANES:]).astype(BF16)

    def mm_mg(c):
        idx = cur["slot_m"] * N_CHUNKS + c
        rhs = jnp.concatenate([mm[c]["wu"],
                               jnp.concatenate([zero_blk, pm_v[idx]], axis=1)], axis=0)
        mg = _dot(pm_bk[idx], rhs)
        mc_m[idx] = mg[:, 0:LANES].astype(BF16)
        mc_g[idx] = mg[:, LANES:]
        mc_d[idx] = pm_d[idx]

    matmul_stages = [mm_scores, mm_square, mm_double, mm_double, mm_double, mm_double,
                     mm_inverse, mm_akv, mm_wu, mm_yw, mm_mg]

    chain = {"hs": jnp.where(first_chain, 0.0, state[...])}

    cv = [dict() for _ in range(N_CHUNKS)]

    def chain_core(c):
        idx = cur["slot_p"] * N_CHUNKS + c
        hs = chain["hs"]
        hs_b = hs.astype(BF16)
        y = _dot(mc_rw[idx], hs_b) + mc_y0[idx]
        chain["hs"] = hs * mc_d[idx] + _dot(mc_m[idx], hs_b) + mc_g[idx]
        cv[c]["y"] = y[0:CHUNK, :] + y[CHUNK:, :]

    def chain_mean(c):
        slot_p, rws, d = cur["slot_p"], rows[c], cv[c]
        d["yc"] = d["y"] - _head_sum(d["y"], head0) * (1.0 / HEAD_DIM)
        d["bonus"] = _head_sum(pc_r[slot_p, rws, :] * pc_k[slot_p, rws, :] * rk_ref[...],
                               head0) * pc_v[slot_p, rws, :]

    def chain_finish(c):
        slot_p, rws, d = cur["slot_p"], rows[c], cv[c]
        var = _head_sum(d["yc"] * d["yc"], head0) * (1.0 / HEAD_DIM)
        yn = d["yc"] * lax.rsqrt(var + LNX_EPS) * lng_ref[...] + lnb_ref[...]
        o_ref[0, block_rows(c), :] = ((yn + d["bonus"]) * pc_gz[slot_p, rws, :]
                                      ).astype(o_ref.dtype)

    n_rounds = len(matmul_stages)
    per_round = -(-N_CHUNKS // (n_rounds - 3))
    core_gap = N_CHUNKS // per_round

    def chain_cores_of(rnd):
        return range(min(rnd * per_round, N_CHUNKS), min((rnd + 1) * per_round, N_CHUNKS))

    n_slots = (n_rounds - 1) * N_CHUNKS
    total = float(N_CHUNKS * sum(w for _, w in prep_groups))
    prep_slots = [[] for _ in range(n_slots)]
    done = 0.0
    for fn, weight in prep_groups:
        for c in range(N_CHUNKS):
            prep_slots[min(int(done / total * n_slots), n_slots - 1)].append((fn, c))
            done += weight
    for half in (0, 1):
        cur.update(half=half, slot_p=half, slot_m=1 - half)
        for rnd, mm_stage in enumerate(matmul_stages):
            if rnd >= 1:
                for chunk in chain_cores_of(rnd - 1):
                    chain_mean(chunk)
            if rnd >= 2:
                for chunk in chain_cores_of(rnd - 2):
                    chain_finish(chunk)
            cores = list(chain_cores_of(rnd))
            for c in range(N_CHUNKS):
                if c % core_gap == 0 and c // core_gap < len(cores):
                    chain_core(cores[c // core_gap])
                mm_stage(c)
                if rnd == n_rounds - 1:
                    prep_store_pc(c)
                else:
                    for fn, pc in prep_slots[rnd * N_CHUNKS + c]:
                        fn(pc)
    state[...] = chain["hs"]


def _rwkv(proj3, lora3, w0, a0, k_k, k_a, r_k, lnx_g, lnx_b, wa2, g2p):
    batch, seq, _ = proj3.shape
    bps = seq // RWKV_BLOCK
    assert bps % 2 == 0
    wps = bps // 2
    n_total = batch * N_PAIRS * wps
    blk = (1, 2 * RWKV_BLOCK, LANES)

    def split(n):
        return n // (N_PAIRS * wps), n % wps, (n // wps) % N_PAIRS

    def prep_n(s):
        return jnp.minimum(s, n_total - 1)

    def chain_n(s):
        return jnp.maximum(s - 1, 0)

    def col(base):
        def index(s):
            b, i, p = split(prep_n(s))
            return b, i, base // LANES + p
        return index

    def lora_index(s):
        b, i, _ = split(prep_n(s))
        return b, i, 0

    def out_index(s):
        return split(chain_n(s))

    vec_prep = pl.BlockSpec((1, LANES), lambda s: (0, split(prep_n(s))[2]))
    vec_chain = pl.BlockSpec((1, LANES), lambda s: (0, split(chain_n(s))[2]))
    n_slots = 2 * N_CHUNKS
    return pl.pallas_call(
        functools.partial(_rwkv_kernel, blocks_per_stream=bps),
        grid=(n_total + 1,),
        in_specs=[
            pl.BlockSpec(blk, col(COL_R)),
            pl.BlockSpec(blk, col(COL_RK)),
            pl.BlockSpec(blk, col(COL_RV)),
            pl.BlockSpec(blk, col(COL_ZR)),
            pl.BlockSpec((1, 2 * RWKV_BLOCK, LORA_PAD), lora_index),
            vec_prep, vec_prep, vec_prep, vec_prep,
            pl.BlockSpec((LANES, 2 * LANES), lambda s: (0, split(prep_n(s))[2])),
            pl.BlockSpec((GATE_PAD, LANES), lambda s: (0, split(prep_n(s))[2])),
            vec_chain, vec_chain, vec_chain,
        ],
        out_specs=pl.BlockSpec(blk, out_index),
        out_shape=jax.ShapeDtypeStruct((batch, seq, RWKV_WIDTH), BF16),
        scratch_shapes=[
            pltpu.VMEM((LANES, LANES), F32),
            pltpu.VMEM((n_slots, LANES, LANES), BF16),
            pltpu.VMEM((n_slots, LANES, LANES), F32),
            pltpu.VMEM((n_slots, LANES, LANES), BF16),
            pltpu.VMEM((n_slots, LANES, 2 * LANES), BF16),
            pltpu.VMEM((n_slots, LANES, LANES), F32),
            pltpu.VMEM((n_slots, LANES, LANES), BF16),
            pltpu.VMEM((2, RWKV_BLOCK, LANES), F32),
            pltpu.VMEM((2, RWKV_BLOCK, LANES), F32),
            pltpu.VMEM((2, RWKV_BLOCK, LANES), F32),
            pltpu.VMEM((2, RWKV_BLOCK, LANES), F32),
            pltpu.VMEM((n_slots, LANES, LANES), BF16),
            pltpu.VMEM((n_slots, LANES, LANES), F32),
            pltpu.VMEM((n_slots, LANES, LANES), BF16),
            pltpu.VMEM((n_slots, LANES, LANES), F32),
            pltpu.VMEM((n_slots, LANES, LANES), F32),
        ],
        compiler_params=pltpu.CompilerParams(
            dimension_semantics=("arbitrary",),
            vmem_limit_bytes=VMEM_LIMIT),
        name="rwkv7_time_mix",
    )(proj3, proj3, proj3, proj3, lora3, w0, a0, k_k, k_a, wa2, g2p, r_k, lnx_g, lnx_b)


def _out_proj_kernel(att_ref, rwk_ref, wa_ref, wr_ref, x_ref, g_ref, o_ref, *, normalize):
    y = x_ref[...] + _dot(att_ref[...], wa_ref[...]) + _dot(rwk_ref[...], wr_ref[...])
    if normalize:
        ms = jnp.mean(y * y, axis=-1, keepdims=True)
        y = y * lax.rsqrt(ms + NORM_EPS) * g_ref[...]
    o_ref[...] = y


def _out_proj(att2, rwk2, w_out_b, x2, g, normalize):
    tokens = x2.shape[0]
    return pl.pallas_call(
        functools.partial(_out_proj_kernel, normalize=normalize),
        grid=(tokens // OUT_TM,),
        in_specs=[
            pl.BlockSpec((OUT_TM, ATT_WIDTH), lambda i: (i, 0)),
            pl.BlockSpec((OUT_TM, RWKV_WIDTH), lambda i: (i, 0)),
            pl.BlockSpec((ATT_WIDTH, D_MODEL), lambda i: (0, 0)),
            pl.BlockSpec((RWKV_WIDTH, D_MODEL), lambda i: (ATT_WIDTH // RWKV_WIDTH, 0)),
            pl.BlockSpec((OUT_TM, D_MODEL), lambda i: (i, 0)),
            pl.BlockSpec((1, D_MODEL), lambda i: (0, 0)),
        ],
        out_specs=pl.BlockSpec((OUT_TM, D_MODEL), lambda i: (i, 0)),
        out_shape=jax.ShapeDtypeStruct((tokens, D_MODEL), F32),
        compiler_params=pltpu.CompilerParams(
            dimension_semantics=("arbitrary",),
            vmem_limit_bytes=VMEM_LIMIT),
        name="out_proj",
    )(att2, rwk2, w_out_b, w_out_b, x2, g)


def _rope_tables(seq):
    inv_freq = ROPE_THETA ** (-jnp.arange(0, HEAD_DIM, 2, dtype=F32) / HEAD_DIM)
    ang = jnp.arange(seq, dtype=jnp.int32).astype(F32)[:, None] * inv_freq[None, :]
    cos, sin = jnp.cos(ang), jnp.sin(ang)
    reps = LANES // HEAD_DIM
    cos_t = jnp.tile(jnp.concatenate([cos, cos], axis=1), (1, reps))
    sin_t = jnp.tile(jnp.concatenate([-sin, sin], axis=1), (1, reps))
    return cos_t, sin_t


def _layer(x, norm_g, w_in, shift_mu, w0, w2, a0, a2, g2, k_k, k_a, r_k,
           lnx_g, lnx_b, w_out, cos_t, sin_t):
    batch, seq, _ = x.shape
    a_w, r_w = ATT_WIDTH, RWKV_WIDTH
    lo = 4 * a_w + 3 * r_w
    n_lora = DECAY_LORA + AAA_LORA + GATE_LORA
    assert lo == COL_ZR
    w_all = w_in.astype(BF16)
    w_gate = w_all[:, lo + n_lora:]
    w_lora = jnp.pad(w_all[:, lo:lo + n_lora], ((0, 0), (0, LORA_PAD - n_lora)))
    mu_main = shift_mu[None, 0:3 * r_w]
    mu_lora = jnp.pad(shift_mu[None, 3 * r_w:], ((0, 0), (0, LORA_PAD - n_lora)))
    w2p = jnp.pad(w2, ((0, LANES - DECAY_LORA), (0, 0))).reshape(LANES, N_PAIRS, LANES)
    a2p = jnp.pad(a2, ((DECAY_LORA, LANES - DECAY_LORA - AAA_LORA), (0, 0))
                  ).reshape(LANES, N_PAIRS, LANES)
    wa2 = jnp.concatenate([w2p, a2p], axis=2).reshape(LANES, 2 * r_w)
    g2p = jnp.pad(g2, ((0, GATE_PAD - GATE_LORA), (0, 0)))

    x2 = x.reshape(batch * seq, D_MODEL)
    proj, lora = _in_proj(x2, norm_g[None, :], w_all, w_gate, w_lora, cos_t, sin_t,
                          mu_main, mu_lora, seq)
    proj3 = proj.reshape(batch, seq, PROJ_WIDTH)
    lora3 = lora.reshape(batch, seq, LORA_PAD)
    att = _attention(proj3)
    rwk = _rwkv(proj3, lora3, w0[None, :], a0[None, :],
                k_k[None, :], k_a[None, :], r_k.reshape(1, r_w), lnx_g[None, :],
                lnx_b[None, :], wa2, g2p)
    w_out_b = w_out.astype(BF16)
    return att.reshape(batch * seq, a_w), rwk.reshape(batch * seq, r_w), w_out_b, x2


def kernel(x, norm_g, w_in, shift_mu, w0, w2, a0, a2, g2, k_k, k_a, r_k,
           lnx_g, lnx_b, w_out, final_g):
    batch, seq, _ = x.shape
    depth = norm_g.shape[0]
    assert seq % SUPER == 0 and seq % IN_TM == 0 and (batch * seq) % OUT_TM == 0
    cos_t, sin_t = _rope_tables(seq)
    for l in range(depth):
        att2, rwk2, w_out_b, x2 = _layer(
            x, norm_g[l], w_in[l], shift_mu[l], w0[l], w2[l], a0[l], a2[l], g2[l],
            k_k[l], k_a[l], r_k[l], lnx_g[l], lnx_b[l], w_out[l], cos_t, sin_t)
        y2 = _out_proj(att2, rwk2, w_out_b, x2, final_g[None, :],
                       normalize=(l == depth - 1))
        x = y2.reshape(batch, seq, D_MODEL)
    return x
```
